```python
import math
import jax, jax.numpy as jnp
from jax import lax
import numpy as np

D_MODEL = 1024
BATCH = 32
SEQ = 2048
DEPTH = 2

HEAD_DIM = 64
NSA_HEADS = 8
NSA_GROUPS = 2
NSA_HPG = NSA_HEADS // NSA_GROUPS
CMP_LEN = 32
CMP_STRIDE = 16
CMP_HIDDEN = 2 * HEAD_DIM
SLC_BLOCK = 64
SLC_TOPN = 8
WINDOW = 256
FORCE_BONUS = 1.0e4
DIFF_HEADS = 4
FOX_HEADS = 8
N_BRANCH = 3
BRANCH_WIDTH = 512
D_FF = 2816
CONV_W = 3
Q_BLOCK = 128
EPS = 1e-6
NEG = -1.0e30

NSA_Q = NSA_HEADS * HEAD_DIM
NSA_KV = 3 * 2 * NSA_GROUPS * HEAD_DIM
NSA_GATE = 3 * NSA_HEADS
DIFF_Q = DIFF_HEADS * 2 * HEAD_DIM
DIFF_K = DIFF_HEADS * 2 * HEAD_DIM
DIFF_V = DIFF_HEADS * 2 * HEAD_DIM
FOX_Q = FOX_HEADS * HEAD_DIM
FOX_K = FOX_HEADS * HEAD_DIM
FOX_V = FOX_HEADS * HEAD_DIM
FOX_F = FOX_HEADS
MERGE_G = N_BRANCH * D_MODEL
IN_SIZES = (NSA_Q, NSA_KV, NSA_GATE, DIFF_Q, DIFF_K, DIFF_V, FOX_Q, FOX_K, FOX_V, FOX_F, MERGE_G)
D_IN = NSA_Q + NSA_KV + NSA_GATE + DIFF_Q + DIFF_K + DIFF_V + FOX_Q + FOX_K + FOX_V + FOX_F + MERGE_G

kernel_name = "hybrid_nsa_diff_fox_convffn"


def rms_norm(x, g):
    xf = x.astype(jnp.float32)
    y = xf * lax.rsqrt(jnp.mean(xf * xf, axis=-1, keepdims=True) + EPS)
    return (y * g.astype(jnp.float32)).astype(x.dtype)


def alibi_slopes(n):
    return jnp.asarray(2.0 ** (-8.0 * np.arange(1, n + 1) / n), jnp.float32)


def masked_softmax(logits, mask):
    p = jax.nn.softmax(jnp.where(mask, logits, NEG), axis=-1)
    return jnp.where(mask, p, 0.0)


def unblock(y):
    y = jnp.moveaxis(y, 0, 1)
    return y.reshape((y.shape[0], y.shape[1] * y.shape[2]) + y.shape[3:])


def compress(raw, pe, w1, w2, idx):
    B = raw.shape[0]
    G, hd = raw.shape[2], raw.shape[3]
    nc, L = idx.shape
    blocks = raw[:, idx] + pe[:, None, :]
    flat = jnp.transpose(blocks, (0, 1, 3, 2, 4)).reshape(B, nc, G, L * hd)
    return jax.nn.gelu(flat @ w1) @ w2


def nsa_attention(q, k_cmp, v_cmp, k_slc, v_slc, k_win, v_win, gate_logits, cmp_end):
    B, S, H, hd = q.shape
    G, R = NSA_GROUPS, NSA_HPG
    nc = k_cmp.shape[1]
    ns = S // SLC_BLOCK
    n_sel = min(SLC_TOPN, ns)
    slopes = alibi_slopes(H).reshape(G, R)
    c_start = jnp.arange(nc) * CMP_STRIDE
    s_start = jnp.arange(ns) * SLC_BLOCK
    overlap = ((c_start[:, None] < s_start[None, :] + SLC_BLOCK)
               & (c_start[:, None] + CMP_LEN > s_start[None, :])).astype(jnp.float32)
    kb = jnp.transpose(k_slc.reshape(B, ns, SLC_BLOCK, G, hd), (0, 3, 1, 2, 4))
    vb = jnp.transpose(v_slc.reshape(B, ns, SLC_BLOCK, G, hd), (0, 3, 1, 2, 4))
    pad = ((0, 0), (WINDOW, 0), (0, 0), (0, 0))
    kw_pad = jnp.pad(k_win, pad)
    vw_pad = jnp.pad(v_win, pad)
    gather = jax.vmap(jax.vmap(lambda blocks, ix: blocks[ix]))
    blk_ids = jnp.arange(ns)
    sl = slopes[None, :, :, None]

    def chunk(c):
        t0 = c * Q_BLOCK
        tpos = t0 + jnp.arange(Q_BLOCK)
        qg = lax.dynamic_slice_in_dim(q, t0, Q_BLOCK, axis=1).reshape(B, Q_BLOCK, G, R, hd)
        dist_c = tpos[:, None] - cmp_end[None, :]
        s_c = jnp.einsum('btgrd,bngd->bgrtn', qg, k_cmp).astype(jnp.float32)
        s_c = s_c - sl[..., None] * dist_c.astype(jnp.float32)
        p_c = masked_softmax(s_c, dist_c >= 0)
        o_c = jnp.einsum('bgrtn,bngd->btgrd', p_c.astype(v_cmp.dtype), v_cmp)
        imp = jnp.einsum('bgrtn,nj->bgtj', p_c, overlap)
        cur = tpos // SLC_BLOCK
        causal_blk = blk_ids[None, :] <= cur[:, None]
        forced = ((blk_ids[None, :] == 0) | (blk_ids[None, :] == cur[:, None])
                  | (blk_ids[None, :] == cur[:, None] - 1))
        score = jnp.where(causal_blk, jnp.where(forced, imp + FORCE_BONUS, imp), NEG)
        sel = lax.top_k(score, n_sel)[1]
        ks = gather(kb, sel)
        vs = gather(vb, sel)
        spos = sel[..., None] * SLC_BLOCK + jnp.arange(SLC_BLOCK)
        dist_s = (tpos[None, None, :, None, None] - spos)[:, :, None]
        s_s = jnp.einsum('btgrd,bgtnkd->bgrtnk', qg, ks).astype(jnp.float32)
        s_s = s_s - sl[..., None, None] * dist_s.astype(jnp.float32)
        m = n_sel * SLC_BLOCK
        p_s = masked_softmax(s_s.reshape(B, G, R, Q_BLOCK, m),
                             (dist_s >= 0).reshape(B, G, 1, Q_BLOCK, m))
        o_s = jnp.einsum('bgrtm,bgtmd->btgrd', p_s.astype(vs.dtype),
                         vs.reshape(B, G, Q_BLOCK, m, hd))
        kw = lax.dynamic_slice_in_dim(kw_pad, t0, Q_BLOCK + WINDOW, axis=1)
        vw = lax.dynamic_slice_in_dim(vw_pad, t0, Q_BLOCK + WINDOW, axis=1)
        wpos = t0 - WINDOW + jnp.arange(Q_BLOCK + WINDOW)
        dist_w = tpos[:, None] - wpos[None, :]
        mask_w = (wpos[None, :] >= 0) & (dist_w >= 0) & (dist_w < WINDOW)
        s_w = jnp.einsum('btgrd,bsgd->bgrts', qg, kw).astype(jnp.float32)
        s_w = s_w - sl[..., None] * dist_w.astype(jnp.float32)
        p_w = masked_softmax(s_w, mask_w)
        o_w = jnp.einsum('bgrts,bsgd->btgrd', p_w.astype(vw.dtype), vw)
        gc = jax.nn.sigmoid(lax.dynamic_slice_in_dim(gate_logits, t0, Q_BLOCK, axis=1)
                            .reshape(B, Q_BLOCK, 3, G, R, 1))
        o = gc[:, :, 0] * o_c + gc[:, :, 1] * o_s + gc[:, :, 2] * o_w
        return o.reshape(B, Q_BLOCK, H * hd)

    return unblock(lax.map(chunk, jnp.arange(S // Q_BLOCK)))


def diff_attention(q, k, v, lam):
    B, S, H = q.shape[0], q.shape[1], q.shape[2]
    slopes = alibi_slopes(H)[None, :, None, None, None]
    kpos = jnp.arange(S)

    def chunk(c):
        t0 = c * Q_BLOCK
        tpos = t0 + jnp.arange(Q_BLOCK)
        qc = lax.dynamic_slice_in_dim(q, t0, Q_BLOCK, axis=1)
        dist = tpos[:, None] - kpos[None, :]
        s = jnp.einsum('bthcd,bshcd->bhcts', qc, k).astype(jnp.float32)
        s = s - slopes * dist.astype(jnp.float32)
        p = masked_softmax(s, dist >= 0)
        a = p[:, :, 0] - lam * p[:, :, 1]
        return jnp.einsum('bhts,bshe->bthe', a.astype(v.dtype), v)

    return unblock(lax.map(chunk, jnp.arange(S // Q_BLOCK)))


def forgetting_attention(q, k, v, log_f):
    S = q.shape[1]
    F = jnp.transpose(jnp.cumsum(log_f, axis=1), (0, 2, 1))
    kpos = jnp.arange(S)

    def chunk(c):
        t0 = c * Q_BLOCK
        tpos = t0 + jnp.arange(Q_BLOCK)
        qc = lax.dynamic_slice_in_dim(q, t0, Q_BLOCK, axis=1)
        Fc = lax.dynamic_slice_in_dim(F, t0, Q_BLOCK, axis=2)
        s = jnp.einsum('bthd,bshd->bhts', qc, k).astype(jnp.float32)
        s = s + Fc[..., :, None] - F[..., None, :]
        p = masked_softmax(s, tpos[:, None] >= kpos[None, :])
        return jnp.einsum('bhts,bshd->bthd', p.astype(v.dtype), v)

    return unblock(lax.map(chunk, jnp.arange(S // Q_BLOCK)))


def conv_ffn(h, w_up, conv_w, conv_b, w_down):
    S = h.shape[1]
    u, g = jnp.split(h @ w_up, 2, axis=-1)
    up = jnp.pad(u, ((0, 0), (CONV_W - 1, 0), (0, 0)))
    uc = conv_b
    for j in range(CONV_W):
        uc = uc + conv_w[j] * up[:, j:j + S]
    return (jax.nn.gelu(uc) * g) @ w_down


def setup_inputs(seed: int = 0) -> dict:
    key = jax.random.key(seed)
    ks = jax.random.split(key, 24)
    f32 = jnp.float32
    n = lambda k, shape, scale: (jax.random.normal(k, shape, f32) * scale)
    gain = lambda k, shape: 1.0 + 0.02 * jax.random.normal(k, shape, f32)
    L, hd = CMP_LEN, HEAD_DIM
    return {
        "x": jax.random.normal(ks[0], (BATCH, SEQ, D_MODEL), f32),
        "attn_norm_g": gain(ks[1], (DEPTH, D_MODEL)),
        "w_in": n(ks[2], (DEPTH, D_MODEL, D_IN), D_MODEL ** -0.5),
        "nsa_q_g": gain(ks[3], (DEPTH, hd)),
        "nsa_k_g": gain(ks[4], (DEPTH, 3, hd)),
        "cmp_pe": n(ks[5], (DEPTH, 2, L, hd), 0.02),
        "cmp_w1": n(ks[6], (DEPTH, 2, L * hd, CMP_HIDDEN), (L * hd) ** -0.5),
        "cmp_w2": n(ks[7], (DEPTH, 2, CMP_HIDDEN, hd), CMP_HIDDEN ** -0.5),
        "diff_q_g": gain(ks[8], (DEPTH, hd)),
        "diff_k_g": gain(ks[9], (DEPTH, hd)),
        "diff_lam": n(ks[10], (DEPTH, 4, hd), 0.1),
        "diff_subln_g": gain(ks[11], (DEPTH, 2 * hd)),
        "fox_q_g": gain(ks[12], (DEPTH, hd)),
        "fox_k_g": gain(ks[13], (DEPTH, hd)),
        "fox_b": jax.random.uniform(ks[14], (DEPTH, FOX_HEADS), f32, 1.0, 4.0),
        "w_br": n(ks[15], (DEPTH, N_BRANCH, BRANCH_WIDTH, D_MODEL), BRANCH_WIDTH ** -0.5),
        "w_o": n(ks[16], (DEPTH, D_MODEL, D_MODEL), D_MODEL ** -0.5),
        "ffn_norm_g": gain(ks[17], (DEPTH, D_MODEL)),
        "w_up": n(ks[18], (DEPTH, D_MODEL, 2 * D_FF), D_MODEL ** -0.5),
        "conv_w": n(ks[19], (DEPTH, CONV_W, D_FF), CONV_W ** -0.5),
        "conv_b": n(ks[20], (DEPTH, D_FF), 0.02),
        "w_down": n(ks[21], (DEPTH, D_FF, D_MODEL), D_FF ** -0.5),
    }


def reference(x, attn_norm_g, w_in, nsa_q_g, nsa_k_g, cmp_pe, cmp_w1, cmp_w2,
              diff_q_g, diff_k_g, diff_lam, diff_subln_g, fox_q_g, fox_k_g, fox_b,
              w_br, w_o, ffn_norm_g, w_up, conv_w, conv_b, w_down):
    B, S, D = x.shape
    hd = HEAD_DIM
    scale = hd ** -0.5
    split_points = [int(v) for v in np.cumsum(IN_SIZES)[:-1]]
    nc = (S - CMP_LEN) // CMP_STRIDE + 1
    cmp_idx = jnp.arange(nc)[:, None] * CMP_STRIDE + jnp.arange(CMP_LEN)[None, :]
    cmp_end = cmp_idx[:, -1]
    h = x
    for l in range(DEPTH):
        a = rms_norm(h, attn_norm_g[l])
        (nq, nkv, ngate, dq, dk, dv, fq, fk, fv, ff, mg) = jnp.split(a @ w_in[l], split_points, axis=-1)

        qa = rms_norm(nq.reshape(B, S, NSA_HEADS, hd), nsa_q_g[l]) * scale
        kv = nkv.reshape(B, S, 3, 2, NSA_GROUPS, hd)
        k_cmp = rms_norm(compress(kv[:, :, 0, 0], cmp_pe[l, 0], cmp_w1[l, 0], cmp_w2[l, 0], cmp_idx),
                         nsa_k_g[l, 0])
        v_cmp = compress(kv[:, :, 0, 1], cmp_pe[l, 1], cmp_w1[l, 1], cmp_w2[l, 1], cmp_idx)
        k_slc = rms_norm(kv[:, :, 1, 0], nsa_k_g[l, 1])
        k_win = rms_norm(kv[:, :, 2, 0], nsa_k_g[l, 2])
        o_a = nsa_attention(qa, k_cmp, v_cmp, k_slc, kv[:, :, 1, 1], k_win, kv[:, :, 2, 1],
                            ngate.reshape(B, S, 3, NSA_HEADS), cmp_end)

        lam_init = 0.8 - 0.6 * math.exp(-0.3 * l)
        lv = diff_lam[l].astype(jnp.float32)
        lam = jnp.exp(jnp.sum(lv[0] * lv[1])) - jnp.exp(jnp.sum(lv[2] * lv[3])) + lam_init
        qb = rms_norm(dq.reshape(B, S, DIFF_HEADS, 2, hd), diff_q_g[l]) * scale
        kb = rms_norm(dk.reshape(B, S, DIFF_HEADS, 2, hd), diff_k_g[l])
        ob = diff_attention(qb, kb, dv.reshape(B, S, DIFF_HEADS, 2 * hd), lam)
        o_b = (rms_norm(ob, diff_subln_g[l]) * (1.0 - lam_init)).reshape(B, S, BRANCH_WIDTH)

        qc = rms_norm(fq.reshape(B, S, FOX_HEADS, hd), fox_q_g[l]) * scale
        kc = rms_norm(fk.reshape(B, S, FOX_HEADS, hd), fox_k_g[l])
        log_f = jax.nn.log_sigmoid((ff + fox_b[l]).astype(jnp.float32))
        o_c = forgetting_attention(qc, kc, fv.reshape(B, S, FOX_HEADS, hd), log_f).reshape(B, S, BRANCH_WIDTH)

        gates = jax.nn.sigmoid(mg.reshape(B, S, N_BRANCH, D))
        merged = (gates[:, :, 0] * (o_a @ w_br[l, 0]) + gates[:, :, 1] * (o_b @ w_br[l, 1])
                  + gates[:, :, 2] * (o_c @ w_br[l, 2]))
        h = h + merged @ w_o[l]

        h = h + conv_ffn(rms_norm(h, ffn_norm_g[l]), w_up[l], conv_w[l], conv_b[l], w_down[l])
    return h
```

```python
import functools
import math

import numpy as np
import jax
import jax.numpy as jnp
from jax import lax
from jax.experimental import pallas as pl
from jax.experimental.pallas import tpu as pltpu

F32 = jnp.float32
BF16 = jnp.bfloat16

HEAD_DIM = 64
NSA_HEADS = 8
NSA_GROUPS = 2
CMP_LEN = 32
CMP_STRIDE = 16
SLC_BLOCK = 64
SLC_TOPN = 8
WINDOW = 256
FORCE_BONUS = 1.0e4
DIFF_HEADS = 4
FOX_HEADS = 8
CONV_W = 3
EPS = 1e-6
NEG = -1.0e30

LANES = 128
ATT_TILE = 256
VMEM_LIMIT = 56 * 1024 * 1024

COL_MG = 0
COL_NSA_Q = 24
COL_DIFF_Q, COL_DIFF_K, COL_DIFF_V = 28, 32, 36
COL_FOX_Q, COL_FOX_K, COL_FOX_V = 40, 44, 48
COL_K_SLC, COL_V_SLC, COL_K_WIN, COL_V_WIN, COL_K_CMP, COL_V_CMP = 52, 53, 54, 55, 56, 57
PROJ_UNITS = 60
SMALL_GATE0 = 0
SMALL_FF0 = 24


def _cparams(sem):
    return pltpu.CompilerParams(dimension_semantics=sem, vmem_limit_bytes=VMEM_LIMIT)


def _split2(x):
    hi = x.astype(BF16)
    lo = (x - hi.astype(F32)).astype(BF16)
    return hi, lo


def _dot01(x, m):
    hi, lo = _split2(x)
    return (jnp.dot(hi, m, preferred_element_type=F32)
            + jnp.dot(lo, m, preferred_element_type=F32))


def _dot01_3(x, m):
    hi = x.astype(BF16)
    r1 = x - hi.astype(F32)
    mid = r1.astype(BF16)
    lo = (r1 - mid.astype(F32)).astype(BF16)
    return (jnp.dot(hi, m, preferred_element_type=F32)
            + jnp.dot(mid, m, preferred_element_type=F32)
            + jnp.dot(lo, m, preferred_element_type=F32))


def _qk(q, k):
    return lax.dot_general(q, k, (((1,), (1,)), ((), ())), preferred_element_type=F32)


def _headnorm_pair(x, bd, gain):
    ss = _dot01(x * x, bd)
    return x * lax.rsqrt(ss * (1.0 / HEAD_DIM) + EPS) * gain


def _lane_halves(shape):
    lane = lax.broadcasted_iota(jnp.int32, shape, len(shape) - 1)
    return lane < HEAD_DIM


def _online(s, v, m, l, acc):
    m_new = jnp.maximum(m, jnp.max(s, axis=-1, keepdims=True))
    alpha = jnp.exp(m - m_new)
    p = jnp.exp(s - m_new)
    l = alpha * l + jnp.sum(p, axis=-1, keepdims=True)
    acc = alpha * acc + jnp.dot(p.astype(BF16), v, preferred_element_type=F32)
    return m_new, l, acc


def _flash_init(tq):
    return (jnp.full((tq, 1), NEG, F32), jnp.zeros((tq, 1), F32), jnp.zeros((tq, LANES), F32))


def _proj_kernel(x_ref, g_ref, w_ref, ws_ref, o_ref, os_ref, a_scr):
    j = pl.program_id(1)

    @pl.when(j == 0)
    def _():
        x = x_ref[...]
        inv = lax.rsqrt(jnp.mean(x * x, axis=-1, keepdims=True) + EPS)
        a = (x * inv * g_ref[...]).astype(BF16)
        a_scr[...] = a
        os_ref[...] = jnp.dot(a, ws_ref[...], preferred_element_type=F32)

    o_ref[...] = jnp.dot(a_scr[...], w_ref[...], preferred_element_type=F32).astype(o_ref.dtype)


def _proj(x2d, g, w, ws, *, tm=1024, tn=1536):
    t, d = x2d.shape
    n = w.shape[1]
    return pl.pallas_call(
        _proj_kernel,
        grid=(t // tm, n // tn),
        in_specs=[
            pl.BlockSpec((tm, d), lambda i, j: (i, 0)),
            pl.BlockSpec((1, d), lambda i, j: (0, 0)),
            pl.BlockSpec((d, tn), lambda i, j: (0, j)),
            pl.BlockSpec((d, LANES), lambda i, j: (0, 0)),
        ],
        out_specs=[
            pl.BlockSpec((tm, tn), lambda i, j: (i, j)),
            pl.BlockSpec((tm, LANES), lambda i, j: (i, 0)),
        ],
        out_shape=[jax.ShapeDtypeStruct((t, n), BF16), jax.ShapeDtypeStruct((t, LANES), F32)],
        scratch_shapes=[pltpu.VMEM((tm, d), BF16)],
        compiler_params=_cparams(("parallel", "arbitrary")),
        name="proj",
    )(x2d, g, w, ws)


def _compress_kernel(r_ref, w1_ref, pe_ref, w1f_ref, w2_ref, kg_ref, kc_ref, vc_ref):
    outs = []
    for kv in range(2):
        c1 = jnp.dot(pe_ref[kv], w1f_ref[kv], preferred_element_type=F32)[0:1]
        per_group = []
        for g in range(NSA_GROUPS):
            ab = jnp.dot(r_ref[0, kv * NSA_GROUPS + g], w1_ref[kv], preferred_element_type=F32)
            pre = ab[:, :LANES] + pltpu.roll(ab[:, LANES:], LANES - 1, 0) + c1
            hid = jax.nn.gelu(pre).astype(BF16)
            o = jnp.dot(hid, w2_ref[kv], preferred_element_type=F32)
            if kv == 0:
                o = o * lax.rsqrt(jnp.mean(o * o, axis=-1, keepdims=True) + EPS) * kg_ref[...]
            per_group.append(o)
        outs.append(jnp.concatenate(per_group, axis=-1))
    kc_ref[0] = outs[0].astype(BF16)
    vc_ref[0] = outs[1].astype(BF16)


def _compress(r, w1cat, pe8, w1f, w2, kg):
    b = r.shape[0]
    nrow = r.shape[2]
    full = lambda a: pl.BlockSpec(a.shape, lambda i: (0,) * a.ndim)
    return pl.pallas_call(
        _compress_kernel,
        grid=(b,),
        in_specs=[pl.BlockSpec((1,) + r.shape[1:], lambda i: (i, 0, 0, 0)),
                  full(w1cat), full(pe8), full(w1f), full(w2), full(kg)],
        out_specs=[pl.BlockSpec((1, nrow, LANES), lambda i: (i, 0, 0)),
                   pl.BlockSpec((1, nrow, LANES), lambda i: (i, 0, 0))],
        out_shape=[jax.ShapeDtypeStruct((b, nrow, LANES), BF16),
                   jax.ShapeDtypeStruct((b, nrow, LANES), BF16)],
        compiler_params=_cparams(("parallel",)),
        name="compress",
    )(r, w1cat, pe8, w1f, w2, kg)


def _fcum_kernel(s_ref, fb_ref, u_ref, o_ref, *, chunk):
    z = s_ref[0] + fb_ref[...]
    lf = jax.nn.log_sigmoid(z)
    lft = lf.T[SMALL_FF0:SMALL_FF0 + FOX_HEADS]
    seq = lft.shape[1]
    carry = jnp.zeros((FOX_HEADS, 1), F32)
    for c in range(seq // chunk):
        fc = _dot01_3(lft[:, c * chunk:(c + 1) * chunk], u_ref[...]) + carry
        o_ref[0, :, c * chunk:(c + 1) * chunk] = -fc
        carry = fc[:, chunk - 1:chunk]


def _fcum(small3, fb_row, u):
    b, s, _ = small3.shape
    chunk = u.shape[0]
    return pl.pallas_call(
        functools.partial(_fcum_kernel, chunk=chunk),
        grid=(b,),
        in_specs=[pl.BlockSpec((1, s, LANES), lambda i: (i, 0, 0)),
                  pl.BlockSpec((1, LANES), lambda i: (0, 0)),
                  pl.BlockSpec(u.shape, lambda i: (0, 0))],
        out_specs=pl.BlockSpec((1, FOX_HEADS, s), lambda i: (i, 0, 0)),
        out_shape=jax.ShapeDtypeStruct((b, FOX_HEADS, s), F32),
        compiler_params=_cparams(("parallel",)),
        name="fcum",
    )(small3, fb_row, u)


def _fox_kernel(q_ref, k_ref, v_ref, nf_ref, qg_ref, kg_ref, bd_ref, o_ref, kn_scr, *, tq):
    p = pl.program_id(1)
    i = pl.program_id(2)
    bd = bd_ref[...]

    @pl.when(i == 0)
    def _():
        kn_scr[...] = _headnorm_pair(k_ref[0].astype(F32), bd, kg_ref[...]).astype(BF16)

    q = _headnorm_pair(q_ref[0].astype(F32), bd, qg_ref[...])
    low = _lane_halves((tq, LANES))
    qm = (jnp.where(low, q, 0.0).astype(BF16), jnp.where(low, 0.0, q).astype(BF16))

    def tile(j, carry, masked):
        off = pl.multiple_of(j * tq, tq)
        kt = kn_scr[pl.ds(off, tq), :]
        vt = v_ref[0, pl.ds(off, tq), :]
        out = []
        for e in range(2):
            cb = nf_ref[0, pl.ds(2 * p + e, 1), pl.ds(off, tq)]
            s = _qk(qm[e], kt) + cb
            if masked:
                row = lax.broadcasted_iota(jnp.int32, (tq, tq), 0)
                col = lax.broadcasted_iota(jnp.int32, (tq, tq), 1)
                s = jnp.where(col <= row, s, NEG)
            out.extend(_online(s, vt, *carry[3 * e:3 * e + 3]))
        return tuple(out)

    carry = _flash_init(tq) + _flash_init(tq)
    carry = lax.fori_loop(0, i, lambda j, c: tile(j, c, False), carry)
    m0, l0, a0, m1, l1, a1 = tile(i, carry, True)
    o_ref[0] = jnp.where(low, a0 / l0, a1 / l1).astype(o_ref.dtype)


def _fox(proj3, negf, qg, kg, bd, *, tq=ATT_TILE):
    b, s, _ = proj3.shape
    npair = FOX_HEADS // 2
    const = lambda a: pl.BlockSpec(a.shape, lambda bi, p, i: (0,) * a.ndim)
    return pl.pallas_call(
        functools.partial(_fox_kernel, tq=tq),
        grid=(b, npair, s // tq),
        in_specs=[
            pl.BlockSpec((1, tq, LANES), lambda bi, p, i: (bi, i, COL_FOX_Q + p)),
            pl.BlockSpec((1, s, LANES), lambda bi, p, i: (bi, 0, COL_FOX_K + p)),
            pl.BlockSpec((1, s, LANES), lambda bi, p, i: (bi, 0, COL_FOX_V + p)),
            pl.BlockSpec((1, FOX_HEADS, s), lambda bi, p, i: (bi, 0, 0)),
            const(qg), const(kg), const(bd),
        ],
        out_specs=pl.BlockSpec((1, tq, LANES), lambda bi, p, i: (bi, i, p)),
        out_shape=jax.ShapeDtypeStruct((b, s, npair * LANES), BF16),
        scratch_shapes=[pltpu.VMEM((s, LANES), BF16)],
        compiler_params=_cparams(("parallel", "parallel", "arbitrary")),
        name="fox",
    )(proj3, proj3, proj3, negf, qg, kg, bd)


def _diff_kernel(q_ref, k_ref, v_ref, lam_ref, sl_ref, qg_ref, kg_ref, sg_ref, bd_ref, o_ref,
                 kn_scr, *, tq, lam_init):
    h = pl.program_id(1)
    i = pl.program_id(2)
    bd = bd_ref[...]

    @pl.when(i == 0)
    def _():
        kn_scr[...] = _headnorm_pair(k_ref[0].astype(F32), bd, kg_ref[...]).astype(BF16)

    q = _headnorm_pair(q_ref[0].astype(F32), bd, qg_ref[...])
    low = _lane_halves((tq, LANES))
    qm = (jnp.where(low, q, 0.0).astype(BF16), jnp.where(low, 0.0, q).astype(BF16))
    slope = sl_ref[pl.ds(h, 1), :]
    lane_pos = lax.broadcasted_iota(jnp.int32, (1, tq), 1)

    def tile(j, carry, masked):
        off = pl.multiple_of(j * tq, tq)
        kt = kn_scr[pl.ds(off, tq), :]
        vt = v_ref[0, pl.ds(off, tq), :]
        cb = slope * (lane_pos + off).astype(F32)
        out = []
        for e in range(2):
            s = _qk(qm[e], kt) + cb
            if masked:
                row = lax.broadcasted_iota(jnp.int32, (tq, tq), 0)
                col = lax.broadcasted_iota(jnp.int32, (tq, tq), 1)
                s = jnp.where(col <= row, s, NEG)
            out.extend(_online(s, vt, *carry[3 * e:3 * e + 3]))
        return tuple(out)

    carry = _flash_init(tq) + _flash_init(tq)
    carry = lax.fori_loop(0, i, lambda j, c: tile(j, c, False), carry)
    m0, l0, a0, m1, l1, a1 = tile(i, carry, True)

    lv = lam_ref[...]
    lam = (jnp.exp(jnp.sum(lv[0:1] * lv[1:2], axis=-1, keepdims=True))
           - jnp.exp(jnp.sum(lv[2:3] * lv[3:4], axis=-1, keepdims=True)) + lam_init)
    ob = a0 / l0 - lam * (a1 / l1)
    ob = ob * lax.rsqrt(jnp.mean(ob * ob, axis=-1, keepdims=True) + EPS) * sg_ref[...]
    o_ref[0] = (ob * (1.0 - lam_init)).astype(o_ref.dtype)


def _diff(proj3, lam_p, slopes, qg, kg, sg, bd, lam_init, *, tq=ATT_TILE):
    b, s, _ = proj3.shape
    const = lambda a: pl.BlockSpec(a.shape, lambda bi, h, i: (0,) * a.ndim)
    return pl.pallas_call(
        functools.partial(_diff_kernel, tq=tq, lam_init=lam_init),
        grid=(b, DIFF_HEADS, s // tq),
        in_specs=[
            pl.BlockSpec((1, tq, LANES), lambda bi, h, i: (bi, i, COL_DIFF_Q + h)),
            pl.BlockSpec((1, s, LANES), lambda bi, h, i: (bi, 0, COL_DIFF_K + h)),
            pl.BlockSpec((1, s, LANES), lambda bi, h, i: (bi, 0, COL_DIFF_V + h)),
            const(lam_p), const(slopes), const(qg), const(kg), const(sg), const(bd),
        ],
        out_specs=pl.BlockSpec((1, tq, LANES), lambda bi, h, i: (bi, i, h)),
        out_shape=jax.ShapeDtypeStruct((b, s, DIFF_HEADS * LANES), BF16),
        scratch_shapes=[pltpu.VMEM((s, LANES), BF16)],
        compiler_params=_cparams(("parallel", "parallel", "arbitrary")),
        name="diff",
    )(proj3, proj3, proj3, lam_p, slopes, qg, kg, sg, bd)


def _nsa_select(imp_t, t0, tq):
    ns = imp_t.shape[0]
    blk = lax.broadcasted_iota(jnp.int32, (ns, tq), 0)
    cur = (t0 + lax.broadcasted_iota(jnp.int32, (ns, tq), 1)) // SLC_BLOCK
    forced = (blk == 0) | (blk == cur) | (blk == cur - 1)
    score = jnp.where(blk <= cur, jnp.where(forced, imp_t + FORCE_BONUS, imp_t), NEG)
    rank = jnp.zeros((ns, tq), F32)
    for b in range(ns):
        sb = score[b:b + 1, :]
        beats = (sb > score) | ((sb == score) & (blk > b))
        rank = rank + jnp.where(beats, 1.0, 0.0)
    return jnp.where(rank < float(SLC_TOPN), 1.0, 0.0)


def _nsa_kernel(q_ref, sm_ref, kc_ref, vc_ref, ks_ref, vs_ref, kw_ref, vw_ref,
                qg_ref, ksg_ref, kwg_ref, bd_ref, ot_ref, e_ref, eg_ref, o_ref,
                ksn_scr, kwn_scr, mb_scr, *, tq, nc):
    i = pl.program_id(1)
    t0 = i * tq
    bd = bd_ref[...]
    npair = NSA_HEADS // 2

    @pl.when(i == 0)
    def _():
        ksn_scr[...] = _headnorm_pair(ks_ref[0].astype(F32), bd, ksg_ref[...]).astype(BF16)
        kwn_scr[...] = _headnorm_pair(kw_ref[0].astype(F32), bd, kwg_ref[...]).astype(BF16)

    low = _lane_halves((tq, LANES))
    qms = []
    for r in range(npair):
        q = _headnorm_pair(q_ref[0, :, r * LANES:(r + 1) * LANES].astype(F32), bd, qg_ref[...])
        qms.append((jnp.where(low, q, 0.0).astype(BF16), jnp.where(low, 0.0, q).astype(BF16)))
    slopes = [[2.0 ** -(r + 1 + 4 * e) for e in range(2)] for r in range(npair)]

    tcol = t0 + lax.broadcasted_iota(jnp.int32, (tq, LANES), 0)
    ncol = lax.broadcasted_iota(jnp.int32, (tq, LANES), 1)
    cend = ncol * CMP_STRIDE + (CMP_LEN - 1)
    cmask = (tcol >= cend) & (ncol < nc)
    cend_f = (lax.broadcasted_iota(jnp.int32, (1, LANES), 1) * CMP_STRIDE + (CMP_LEN - 1)).astype(F32)
    kc = kc_ref[0]
    vc = vc_ref[0]
    psum = [jnp.zeros((tq, LANES), F32), jnp.zeros((tq, LANES), F32)]
    o_cmp = []
    for r in range(npair):
        halves = []
        for e in range(2):
            s = jnp.where(cmask, _qk(qms[r][e], kc) + slopes[r][e] * cend_f, NEG)
            m = jnp.max(s, axis=-1, keepdims=True)
            pe = jnp.where(cmask, jnp.exp(s - m), 0.0)
            l = jnp.sum(pe, axis=-1, keepdims=True)
            pn = pe / jnp.where(l > 0.0, l, 1.0)
            psum[e] = psum[e] + pn
            halves.append(jnp.dot(pn.astype(BF16), vc, preferred_element_type=F32))
        o_cmp.append(jnp.where(low, halves[0], halves[1]))

    for g in range(NSA_GROUPS):
        hi = psum[g].astype(BF16)
        r1 = psum[g] - hi.astype(F32)
        mid = r1.astype(BF16)
        lo = (r1 - mid.astype(F32)).astype(BF16)
        imp_t = _qk(ot_ref[...], hi) + _qk(ot_ref[...], mid) + _qk(ot_ref[...], lo)
        sel = _nsa_select(imp_t, t0, tq)
        sel = jnp.concatenate([sel, jnp.zeros((LANES - sel.shape[0], tq), F32)], axis=0)
        notsel = (sel.T - 1.0).astype(BF16)
        mb_scr[g] = jnp.dot(notsel, e_ref[...], preferred_element_type=F32)

    lane_pos = lax.broadcasted_iota(jnp.int32, (1, tq), 1)
    row = lax.broadcasted_iota(jnp.int32, (tq, tq), 0)
    col = lax.broadcasted_iota(jnp.int32, (tq, tq), 1)
    gate = jax.nn.sigmoid(sm_ref[0])

    for r in range(npair):
        def slc_tile(j, carry, masked, r=r):
            off = pl.multiple_of(j * tq, tq)
            kt = ksn_scr[pl.ds(off, tq), :]
            vt = vs_ref[0, pl.ds(off, tq), :]
            pos = (lane_pos + off).astype(F32)
            out = []
            for e in range(2):
                s = _qk(qms[r][e], kt) + slopes[r][e] * pos + mb_scr[e, :, pl.ds(off, tq)]
                if masked:
                    s = jnp.where(col <= row, s, NEG)
                out.extend(_online(s, vt, *carry[3 * e:3 * e + 3]))
            return tuple(out)

        carry = _flash_init(tq) + _flash_init(tq)
        carry = lax.fori_loop(0, i, lambda j, c: slc_tile(j, c, False), carry)
        m0, l0, a0, m1, l1, a1 = slc_tile(i, carry, True)
        o_slc = jnp.where(low, a0 / l0, a1 / l1)

        def win_tile(j, carry, wmask, r=r):
            off = pl.multiple_of(j * tq, tq)
            kt = kwn_scr[pl.ds(off, tq), :]
            vt = vw_ref[0, pl.ds(off, tq), :]
            pos = (lane_pos + off).astype(F32)
            out = []
            for e in range(2):
                s = jnp.where(wmask, _qk(qms[r][e], kt) + slopes[r][e] * pos, NEG)
                out.extend(_online(s, vt, *carry[3 * e:3 * e + 3]))
            return tuple(out)

        carry = win_tile(i, _flash_init(tq) + _flash_init(tq), col <= row)
        m0, l0, a0, m1, l1, a1 = win_tile(jnp.maximum(i - 1, 0), carry, (col > row) & (i > 0))
        o_win = jnp.where(low, a0 / l0, a1 / l1)

        sl = slice(r * LANES, (r + 1) * LANES)
        out = (_dot01(gate, eg_ref[0, :, sl]) * o_cmp[r]
               + _dot01(gate, eg_ref[1, :, sl]) * o_slc
               + _dot01(gate, eg_ref[2, :, sl]) * o_win)
        o_ref[0, :, sl] = out.astype(o_ref.dtype)


def _nsa(proj3, small3, kc, vc, qg, ksg, kwg, bd, ot, emat, eg, *, tq=ATT_TILE):
    assert tq == WINDOW
    b, s, _ = proj3.shape
    nc = (s - CMP_LEN) // CMP_STRIDE + 1
    const = lambda a: pl.BlockSpec(a.shape, lambda bi, i: (0,) * a.ndim)
    kvspec = lambda c: pl.BlockSpec((1, s, LANES), lambda bi, i: (bi, 0, c))
    return pl.pallas_call(
        functools.partial(_nsa_kernel, tq=tq, nc=nc),
        grid=(b, s // tq),
        in_specs=[
            pl.BlockSpec((1, tq, 4 * LANES), lambda bi, i: (bi, i, COL_NSA_Q // 4)),
            pl.BlockSpec((1, tq, LANES), lambda bi, i: (bi, i, 0)),
            pl.BlockSpec((1,) + kc.shape[1:], lambda bi, i: (bi, 0, 0)),
            pl.BlockSpec((1,) + vc.shape[1:], lambda bi, i: (bi, 0, 0)),
            kvspec(COL_K_SLC), kvspec(COL_V_SLC), kvspec(COL_K_WIN), kvspec(COL_V_WIN),
            const(qg), const(ksg), const(kwg), const(bd), const(ot), const(emat), const(eg),
        ],
        out_specs=pl.BlockSpec((1, tq, 4 * LANES), lambda bi, i: (bi, i, 0)),
        out_shape=jax.ShapeDtypeStruct((b, s, 4 * LANES), BF16),
        scratch_shapes=[pltpu.VMEM((s, LANES), BF16), pltpu.VMEM((s, LANES), BF16),
                        pltpu.VMEM((NSA_GROUPS, tq, s), F32)],
        compiler_params=_cparams(("parallel", "arbitrary")),
        name="nsa",
    )(proj3, small3, kc, vc, proj3, proj3, proj3, proj3, qg, ksg, kwg, bd, ot, emat, eg)


def _merge_kernel(oa_ref, ob_ref, oc_ref, g0_ref, g1_ref, g2_ref, h_ref, wbr_ref, wo_ref, o_ref):
    merged = None
    for c, (o_r, g_r) in enumerate(((oa_ref, g0_ref), (ob_ref, g1_ref), (oc_ref, g2_ref))):
        y = jnp.dot(o_r[...], wbr_ref[c], preferred_element_type=F32)
        term = jax.nn.sigmoid(g_r[...].astype(F32)) * y
        merged = term if merged is None else merged + term
    o_ref[...] = h_ref[...] + jnp.dot(merged.astype(BF16), wo_ref[...], preferred_element_type=F32)


def _merge(oa, ob, oc, proj2, h2, wbr, wo, *, tm=512):
    t, d = h2.shape
    bw = oa.shape[1]
    row = lambda w: pl.BlockSpec((tm, w), lambda i: (i, 0))
    gate = lambda c: pl.BlockSpec((tm, d), lambda i: (i, COL_MG * LANES // d + c))
    return pl.pallas_call(
        _merge_kernel,
        grid=(t // tm,),
        in_specs=[row(bw), row(bw), row(bw), gate(0), gate(1), gate(2), row(d),
                  pl.BlockSpec(wbr.shape, lambda i: (0, 0, 0)),
                  pl.BlockSpec(wo.shape, lambda i: (0, 0))],
        out_specs=row(d),
        out_shape=jax.ShapeDtypeStruct((t, d), F32),
        compiler_params=_cparams(("parallel",)),
        name="merge",
    )(oa, ob, oc, proj2, proj2, proj2, h2, wbr, wo)


HALO = 16


def _ffn_kernel(x_ref, xh_ref, g_ref, wu_ref, wg_ref, cw_ref, cb_ref, wd_ref, o_ref,
                a_scr, u_scr, acc_scr, *, tm, tiles_per_seq):
    i = pl.program_id(0)
    j = pl.program_id(1)
    nj = pl.num_programs(1)

    @pl.when(j == 0)
    def _():
        def norm(x):
            return (x * lax.rsqrt(jnp.mean(x * x, axis=-1, keepdims=True) + EPS)
                    * g_ref[...]).astype(BF16)
        a_scr[0:HALO, :] = norm(xh_ref[...])
        a_scr[HALO:, :] = norm(x_ref[...])
        acc_scr[...] = jnp.zeros_like(acc_scr)

    u = jnp.dot(a_scr[...], wu_ref[...], preferred_element_type=F32)
    rows = lax.broadcasted_iota(jnp.int32, u.shape, 0)
    seq_start = (i % tiles_per_seq) == 0
    u_scr[...] = jnp.where((rows < HALO) & seq_start, 0.0, u)
    cw = cw_ref[...]
    uc = cb_ref[...]
    for t in range(CONV_W):
        uc = uc + cw[t:t + 1, :] * u_scr[pl.ds(HALO - (CONV_W - 1) + t, tm), :]
    gt = jnp.dot(a_scr[HALO:, :], wg_ref[...], preferred_element_type=F32)
    act = (jax.nn.gelu(uc) * gt).astype(BF16)
    acc_scr[...] += jnp.dot(act, wd_ref[...], preferred_element_type=F32)

    @pl.when(j == nj - 1)
    def _():
        o_ref[...] = x_ref[...] + acc_scr[...]


def _ffn(h2, g, wup, cw, cb, wd, seq, *, tm=512, tf=1408):
    t, d = h2.shape
    dff = wd.shape[0]
    nf = dff // tf
    return pl.pallas_call(
        functools.partial(_ffn_kernel, tm=tm, tiles_per_seq=seq // tm),
        grid=(t // tm, nf),
        in_specs=[
            pl.BlockSpec((tm, d), lambda i, j: (i, 0)),
            pl.BlockSpec((HALO, d), lambda i, j: (jnp.maximum(i * (tm // HALO) - 1, 0), 0)),
            pl.BlockSpec((1, d), lambda i, j: (0, 0)),
            pl.BlockSpec((d, tf), lambda i, j: (0, j)),
            pl.BlockSpec((d, tf), lambda i, j: (0, nf + j)),
            pl.BlockSpec((CONV_W, tf), lambda i, j: (0, j)),
            pl.BlockSpec((1, tf), lambda i, j: (0, j)),
            pl.BlockSpec((tf, d), lambda i, j: (j, 0)),
        ],
        out_specs=pl.BlockSpec((tm, d), lambda i, j: (i, 0)),
        out_shape=jax.ShapeDtypeStruct((t, d), F32),
        scratch_shapes=[pltpu.VMEM((HALO + tm, d), BF16), pltpu.VMEM((HALO + tm, tf), F32),
                        pltpu.VMEM((tm, d), F32)],
        compiler_params=_cparams(("parallel", "arbitrary")),
        name="ffn",
    )(h2, h2, g, wup, wup, cw, cb, wd)


def _nsa_head_order():
    hpg = NSA_HEADS // NSA_GROUPS
    return [h for r in range(hpg) for h in (r, hpg + r)]


def _constants(seq):
    ns = seq // SLC_BLOCK
    nc = (seq - CMP_LEN) // CMP_STRIDE + 1
    bd = np.kron(np.eye(2), np.ones((HEAD_DIM, HEAD_DIM))).astype(np.float32)
    c_start = np.arange(LANES) * CMP_STRIDE
    s_start = np.arange(ns) * SLC_BLOCK
    ot = ((c_start[None, :] < s_start[:, None] + SLC_BLOCK)
          & (c_start[None, :] + CMP_LEN > s_start[:, None])
          & (np.arange(LANES)[None, :] < nc)).astype(np.float32)
    emat = np.zeros((LANES, seq), np.float32)
    emat[np.arange(seq) // SLC_BLOCK, np.arange(seq)] = -NEG
    order = _nsa_head_order()
    eg = np.zeros((3, LANES, NSA_HEADS * HEAD_DIM), np.float32)
    for c in range(3):
        for slot, h in enumerate(order):
            eg[c, SMALL_GATE0 + c * NSA_HEADS + h, slot * HEAD_DIM:(slot + 1) * HEAD_DIM] = 1.0
    u = (np.arange(ATT_TILE)[:, None] <= np.arange(ATT_TILE)[None, :]).astype(np.float32)
    dslopes = 2.0 ** (-8.0 * np.arange(1, DIFF_HEADS + 1) / DIFF_HEADS)
    sl = np.zeros((8, ATT_TILE), np.float32)
    sl[:DIFF_HEADS] = dslopes[:, None]
    as_bf = lambda a: jnp.asarray(a, BF16)
    return dict(bd=as_bf(bd), ot=as_bf(ot), emat=as_bf(emat), eg=as_bf(eg), u=as_bf(u),
                dslopes=jnp.asarray(sl, F32))


def _pack_w_in(w):
    d = w.shape[0]
    hd = HEAD_DIM
    sizes = [512, 768, 24, 512, 512, 512, 512, 512, 512, 8, 3 * d]
    offs = np.concatenate([[0], np.cumsum(sizes)])
    nq, nkv, ngate, dq, dk, dv, fq, fk, fv, ff, mg = [w[:, offs[k]:offs[k + 1]] for k in range(11)]
    nq = jnp.concatenate([nq[:, h * hd:(h + 1) * hd] for h in _nsa_head_order()], axis=1)
    kv = lambda c, k: nkv[:, (c * 2 + k) * LANES:(c * 2 + k + 1) * LANES]
    cols = [mg, nq, dq, dk, dv, fq, fk, fv,
            kv(1, 0), kv(1, 1), kv(2, 0), kv(2, 1), kv(0, 0), kv(0, 1)]
    main = jnp.concatenate(cols, axis=1)
    main = jnp.pad(main, ((0, 0), (0, PROJ_UNITS * LANES - main.shape[1])))
    small = jnp.pad(jnp.concatenate([ngate, ff], axis=1), ((0, 0), (0, LANES - 32)))
    return main.astype(BF16), small.astype(BF16)


def kernel(x, attn_norm_g, w_in, nsa_q_g, nsa_k_g, cmp_pe, cmp_w1, cmp_w2, diff_q_g, diff_k_g,
           diff_lam, diff_subln_g, fox_q_g, fox_k_g, fox_b, w_br, w_o, ffn_norm_g, w_up, conv_w,
           conv_b, w_down):
    b, s, d = x.shape
    depth = w_in.shape[0]
    hd = HEAD_DIM
    scale = hd ** -0.5
    cst = _constants(s)
    rows16 = s // CMP_STRIDE
    assert rows16 == LANES and s % ATT_TILE == 0, "NSA kernel keeps all compressed blocks in one 128-lane tile"
    order = _nsa_head_order()
    tile2 = lambda g: jnp.tile(g, 2).reshape(1, LANES).astype(F32)

    h = x.reshape(b * s, d)
    for l in range(depth):
        wmain, wsmall = _pack_w_in(w_in[l])
        proj, small = _proj(h, attn_norm_g[l].reshape(1, d), wmain, wsmall)
        proj3 = proj.reshape(b, s, PROJ_UNITS * LANES)
        small3 = small.reshape(b, s, LANES)

        raw = proj3[:, :, COL_K_CMP * LANES:(COL_V_CMP + 1) * LANES]
        raw = raw.reshape(b, rows16, CMP_STRIDE, 2 * NSA_GROUPS, hd)
        raw = jnp.transpose(raw, (0, 3, 1, 2, 4)).reshape(b, 2 * NSA_GROUPS, rows16, CMP_STRIDE * hd)
        half = CMP_STRIDE * hd
        w1cat = jnp.concatenate([cmp_w1[l][:, :half], cmp_w1[l][:, half:]], axis=2).astype(BF16)
        pe8 = jnp.broadcast_to(cmp_pe[l].reshape(2, 1, CMP_LEN * hd), (2, 8, CMP_LEN * hd)).astype(BF16)
        kc, vc = _compress(raw, w1cat, pe8, cmp_w1[l].astype(BF16), cmp_w2[l].astype(BF16),
                           nsa_k_g[l, 0].reshape(1, hd))

        fb_row = jnp.zeros((1, LANES), F32).at[0, SMALL_FF0:SMALL_FF0 + FOX_HEADS].set(fox_b[l])
        negf = _fcum(small3, fb_row, cst["u"])

        o_a = _nsa(proj3, small3, kc, vc, tile2(nsa_q_g[l] * scale), tile2(nsa_k_g[l, 1]),
                   tile2(nsa_k_g[l, 2]), cst["bd"], cst["ot"], cst["emat"], cst["eg"])
        lam_init = 0.8 - 0.6 * math.exp(-0.3 * l)
        o_b = _diff(proj3, diff_lam[l], cst["dslopes"], tile2(diff_q_g[l] * scale),
                    tile2(diff_k_g[l]), diff_subln_g[l].reshape(1, LANES), cst["bd"], lam_init)
        o_c = _fox(proj3, negf, tile2(fox_q_g[l] * scale), tile2(fox_k_g[l]), cst["bd"])

        wbr = w_br[l]
        wbr_a = jnp.concatenate([wbr[0, hh * hd:(hh + 1) * hd] for hh in order], axis=0)
        wbr_p = jnp.stack([wbr_a, wbr[1], wbr[2]]).astype(BF16)
        bw = NSA_HEADS * hd
        h = _merge(o_a.reshape(b * s, bw), o_b.reshape(b * s, bw), o_c.reshape(b * s, bw),
                   proj, h, wbr_p, w_o[l].astype(BF16))
        h = _ffn(h, ffn_norm_g[l].reshape(1, d), w_up[l].astype(BF16), conv_w[l],
                 conv_b[l].reshape(1, -1), w_down[l].astype(BF16), s)
    return h.reshape(b, s, d)
```

```python
import functools
import math

import numpy as np
import jax
import jax.numpy as jnp
from jax import lax
from jax.experimental import pallas as pl
from jax.experimental.pallas import tpu as pltpu

F32 = jnp.float32
BF16 = jnp.bfloat16

HEAD_DIM = 64
NSA_HEADS = 8
NSA_GROUPS = 2
CMP_LEN = 32
CMP_STRIDE = 16
SLC_BLOCK = 64
SLC_TOPN = 8
WINDOW = 256
FORCE_BONUS = 1.0e4
DIFF_HEADS = 4
FOX_HEADS = 8
CONV_W = 3
EPS = 1e-6
NEG = -1.0e30
L2E = 1.4426950408889634

LANES = 128
BLK = 256
KA = 2 * LANES
N_BIAS = 6
MASK0 = 8
VMEM_LIMIT = 56 * 1024 * 1024

COL_MG = 0
COL_NSA_Q = 24
COL_DIFF_Q, COL_DIFF_K, COL_DIFF_V = 28, 32, 36
COL_FOX_Q, COL_FOX_K, COL_FOX_V = 40, 44, 48
COL_K_SLC, COL_V_SLC, COL_K_WIN, COL_V_WIN, COL_K_CMP, COL_V_CMP = 52, 53, 54, 55, 56, 57
PROJ_UNITS = 60
SMALL_GATE0 = 0
SMALL_FF0 = 24


def _cparams(sem):
    return pltpu.CompilerParams(dimension_semantics=sem, vmem_limit_bytes=VMEM_LIMIT)


def _split3(x):
    hi = x.astype(BF16).astype(F32)
    r1 = x - hi
    mid = r1.astype(BF16).astype(F32)
    lo = (r1 - mid).astype(BF16).astype(F32)
    return hi, mid, lo


def _dot01(x, m):
    hi = x.astype(BF16)
    lo = (x - hi.astype(F32)).astype(BF16)
    return (jnp.dot(hi, m, preferred_element_type=F32)
            + jnp.dot(lo, m, preferred_element_type=F32))


def _dot01_left(m, x):
    hi = x.astype(BF16)
    lo = (x - hi.astype(F32)).astype(BF16)
    return (jnp.dot(m, hi, preferred_element_type=F32)
            + jnp.dot(m, lo, preferred_element_type=F32))


def _dot01_3(x, m):
    hi, mid, lo = _split3(x)
    return (jnp.dot(hi.astype(BF16), m, preferred_element_type=F32)
            + jnp.dot(mid.astype(BF16), m, preferred_element_type=F32)
            + jnp.dot(lo.astype(BF16), m, preferred_element_type=F32))


def _qk(q, k):
    return lax.dot_general(q, k, (((1,), (1,)), ((), ())), preferred_element_type=F32)


def _headnorm_pair(x, bd, gain):
    ss = _dot01(x * x, bd)
    return x * lax.rsqrt(ss * (1.0 / HEAD_DIM) + EPS) * gain


def _rows8(vals, width):
    row = lax.broadcasted_iota(jnp.int32, (8, width), 0)
    out = jnp.zeros((8, width), F32)
    for k, v in enumerate(vals):
        out = jnp.where(row == k, v, out)
    return out


def _query_feats(q2, half, extra_rows):
    tq = q2.shape[0]
    qt = q2.T
    row = lax.broadcasted_iota(jnp.int32, (LANES, tq), 0)
    keep = (row < HEAD_DIM) if half == 0 else (row >= HEAD_DIM)
    parts = [jnp.where(keep, qt, 0.0)] + list(extra_rows)
    used = sum(p.shape[0] for p in parts)
    parts.append(jnp.zeros((KA - used, tq), F32))
    return jnp.concatenate(parts, axis=0).astype(BF16)


def _state(dv, nq):
    return (jnp.full((1, nq), NEG, F32), jnp.zeros((1, nq), F32), jnp.zeros((dv, nq), F32))


QK_AHEAD = 2


def _flash_pipeline(blocks, n_chain, dv):
    st = [_state(dv, BLK) for _ in range(n_chain)]
    scores = {}

    def issue(k):
        if k < len(blocks):
            scores[k] = jnp.dot(blocks[k][0](), blocks[k][2](), preferred_element_type=F32)

    for k in range(QK_AHEAD):
        issue(k)
    for k, (_, vt, _, mask, chain) in enumerate(blocks):
        issue(k + QK_AHEAD)
        m, l, acc = st[chain]
        s = scores.pop(k)
        if mask is not None:
            s = jnp.where(mask, s, NEG)
        m_new = jnp.maximum(m, jnp.max(s, axis=0, keepdims=True))
        alpha = jnp.exp2(m - m_new)
        p = jnp.exp2(s - m_new)
        l = alpha * l + jnp.sum(p, axis=0, keepdims=True)
        acc = alpha * acc + jnp.dot(vt(), p.astype(BF16), preferred_element_type=F32)
        st[chain] = (m_new, l, acc)
    return st


def _causal_blocks(nt, kaug_scr, vt_rows, qt_scr):
    key = lax.broadcasted_iota(jnp.int32, (BLK, BLK), 0)
    qry = lax.broadcasted_iota(jnp.int32, (BLK, BLK), 1)
    diag = key <= qry
    blocks = []
    for j in range(nt):
        ks = slice(j * BLK, (j + 1) * BLK)
        for qh in range(j, nt):
            qs = slice(qh * BLK, (qh + 1) * BLK)
            for e in range(2):
                blocks.append((lambda ks=ks: kaug_scr[ks, :],
                               lambda e=e, ks=ks: vt_rows[e](ks),
                               lambda e=e, qs=qs: qt_scr[e, :, qs],
                               diag if qh == j else None, e * nt + qh))
    return blocks


def _proj_kernel(x_ref, g_ref, w_ref, ws_ref, o_ref, os_ref, a_scr):
    j = pl.program_id(1)

    @pl.when(j == 0)
    def _():
        x = x_ref[...]
        inv = lax.rsqrt(jnp.mean(x * x, axis=-1, keepdims=True) + EPS)
        a = (x * inv * g_ref[...]).astype(BF16)
        a_scr[...] = a
        os_ref[...] = jnp.dot(a, ws_ref[...], preferred_element_type=F32)

    o_ref[...] = jnp.dot(a_scr[...], w_ref[...], preferred_element_type=F32).astype(o_ref.dtype)


def _proj(x2d, g, w, ws, *, tm=1024, tn=1536):
    t, d = x2d.shape
    n = w.shape[1]
    return pl.pallas_call(
        _proj_kernel,
        grid=(t // tm, n // tn),
        in_specs=[
            pl.BlockSpec((tm, d), lambda i, j: (i, 0)),
            pl.BlockSpec((1, d), lambda i, j: (0, 0)),
            pl.BlockSpec((d, tn), lambda i, j: (0, j)),
            pl.BlockSpec((d, LANES), lambda i, j: (0, 0)),
        ],
        out_specs=[
            pl.BlockSpec((tm, tn), lambda i, j: (i, j)),
            pl.BlockSpec((tm, LANES), lambda i, j: (i, 0)),
        ],
        out_shape=[jax.ShapeDtypeStruct((t, n), BF16), jax.ShapeDtypeStruct((t, LANES), F32)],
        scratch_shapes=[pltpu.VMEM((tm, d), BF16)],
        compiler_params=_cparams(("parallel", "arbitrary")),
        name="proj",
    )(x2d, g, w, ws)


def _compress_kernel(r_ref, w1_ref, pe_ref, w1f_ref, w2_ref, kg_ref, kc_ref, vc_ref):
    outs = []
    for kv in range(2):
        c1 = jnp.dot(pe_ref[kv], w1f_ref[kv], preferred_element_type=F32)[0:1]
        per_group = []
        for g in range(NSA_GROUPS):
            ab = jnp.dot(r_ref[0, kv * NSA_GROUPS + g], w1_ref[kv], preferred_element_type=F32)
            pre = ab[:, :LANES] + pltpu.roll(ab[:, LANES:], LANES - 1, 0) + c1
            hid = jax.nn.gelu(pre).astype(BF16)
            o = jnp.dot(hid, w2_ref[kv], preferred_element_type=F32)
            if kv == 0:
                o = o * lax.rsqrt(jnp.mean(o * o, axis=-1, keepdims=True) + EPS) * kg_ref[...]
            per_group.append(o)
        outs.append(jnp.concatenate(per_group, axis=-1))
    kc_ref[0] = outs[0].astype(BF16)
    vc_ref[0] = outs[1].astype(BF16)


def _compress(r, w1cat, pe8, w1f, w2, kg):
    b = r.shape[0]
    nrow = r.shape[2]
    full = lambda a: pl.BlockSpec(a.shape, lambda i: (0,) * a.ndim)
    return pl.pallas_call(
        _compress_kernel,
        grid=(b,),
        in_specs=[pl.BlockSpec((1,) + r.shape[1:], lambda i: (i, 0, 0, 0)),
                  full(w1cat), full(pe8), full(w1f), full(w2), full(kg)],
        out_specs=[pl.BlockSpec((1, nrow, LANES), lambda i: (i, 0, 0)),
                   pl.BlockSpec((1, nrow, LANES), lambda i: (i, 0, 0))],
        out_shape=[jax.ShapeDtypeStruct((b, nrow, LANES), BF16),
                   jax.ShapeDtypeStruct((b, nrow, LANES), BF16)],
        compiler_params=_cparams(("parallel",)),
        name="compress",
    )(r, w1cat, pe8, w1f, w2, kg)


def _fcum_kernel(s_ref, fb_ref, u_ref, o_ref, *, chunk):
    z = s_ref[0] + fb_ref[...]
    lf = jax.nn.log_sigmoid(z)
    lft = lf.T[SMALL_FF0:SMALL_FF0 + FOX_HEADS]
    seq = lft.shape[1]
    carry = jnp.zeros((FOX_HEADS, 1), F32)
    for c in range(seq // chunk):
        fc = _dot01_3(lft[:, c * chunk:(c + 1) * chunk], u_ref[...]) + carry
        o_ref[0, :, c * chunk:(c + 1) * chunk] = -fc
        carry = fc[:, chunk - 1:chunk]


def _fcum(small3, fb_row, u):
    b, s, _ = small3.shape
    chunk = u.shape[0]
    return pl.pallas_call(
        functools.partial(_fcum_kernel, chunk=chunk),
        grid=(b,),
        in_specs=[pl.BlockSpec((1, s, LANES), lambda i: (i, 0, 0)),
                  pl.BlockSpec((1, LANES), lambda i: (0, 0)),
                  pl.BlockSpec(u.shape, lambda i: (0, 0))],
        out_specs=pl.BlockSpec((1, FOX_HEADS, s), lambda i: (i, 0, 0)),
        out_shape=jax.ShapeDtypeStruct((b, FOX_HEADS, s), F32),
        compiler_params=_cparams(("parallel",)),
        name="fcum",
    )(small3, fb_row, u)


def _store_query_feats(qt_scr, q_ref, bd, qg_ref, extra_rows):
    q2 = _headnorm_pair(q_ref[0].astype(F32), bd, qg_ref[...])
    for e in range(2):
        qt_scr[e] = _query_feats(q2, e, extra_rows[e])


def _fox_kernel(q_ref, k_ref, v_ref, nf_ref, qg_ref, kg_ref, bd_ref, o_ref,
                kaug_scr, vt_scr, qt_scr):
    p = pl.program_id(1)
    bd = bd_ref[...]
    seq = k_ref.shape[1]
    nt = seq // BLK

    kaug_scr[:, 0:LANES] = _headnorm_pair(k_ref[0].astype(F32), bd, kg_ref[...]).astype(BF16)
    rows = []
    for e in range(2):
        rows.extend(_split3(nf_ref[0, pl.ds(2 * p + e, 1), :] * L2E))
    feats = jnp.concatenate([_rows8(rows, seq), jnp.zeros((LANES - 8, seq), F32)], axis=0)
    kaug_scr[:, LANES:KA] = feats.T.astype(BF16)
    vt_scr[...] = v_ref[0].astype(F32).T.astype(BF16)
    _store_query_feats(qt_scr, q_ref, bd, qg_ref,
                       [[_rows8([0.0] * (3 * e) + [1.0] * 3, seq)] for e in range(2)])

    vt_rows = [lambda ks, e=e: vt_scr[e * HEAD_DIM:(e + 1) * HEAD_DIM, ks] for e in range(2)]
    st = _flash_pipeline(_causal_blocks(nt, kaug_scr, vt_rows, qt_scr), 2 * nt, HEAD_DIM)
    for qh in range(nt):
        ot = jnp.concatenate([st[e * nt + qh][2] / st[e * nt + qh][1] for e in range(2)], axis=0)
        o_ref[0, qh * BLK:(qh + 1) * BLK, :] = ot.T.astype(o_ref.dtype)


def _att_scratch(s):
    return [pltpu.VMEM((s, KA), BF16), pltpu.VMEM((LANES, s), BF16), pltpu.VMEM((2, KA, s), BF16)]


def _fox(proj3, negf, qg, kg, bd):
    b, s, _ = proj3.shape
    npair = FOX_HEADS // 2
    const = lambda a: pl.BlockSpec(a.shape, lambda bi, p: (0,) * a.ndim)
    col = lambda c: pl.BlockSpec((1, s, LANES), lambda bi, p: (bi, 0, c + p))
    return pl.pallas_call(
        _fox_kernel,
        grid=(b, npair),
        in_specs=[col(COL_FOX_Q), col(COL_FOX_K), col(COL_FOX_V),
                  pl.BlockSpec((1, FOX_HEADS, s), lambda bi, p: (bi, 0, 0)),
                  const(qg), const(kg), const(bd)],
        out_specs=pl.BlockSpec((1, s, LANES), lambda bi, p: (bi, 0, p)),
        out_shape=jax.ShapeDtypeStruct((b, s, npair * LANES), BF16),
        scratch_shapes=_att_scratch(s),
        compiler_params=_cparams(("parallel", "parallel")),
        name="fox",
    )(proj3, proj3, proj3, negf, qg, kg, bd)


def _diff_kernel(q_ref, k_ref, v_ref, kx_ref, sl_ref, lam_ref, qg_ref, kg_ref, sg_ref, bd_ref,
                 o_ref, kaug_scr, vt_scr, qt_scr, *, lam_init):
    bd = bd_ref[...]
    nt = k_ref.shape[1] // BLK

    kaug_scr[:, 0:LANES] = _headnorm_pair(k_ref[0].astype(F32), bd, kg_ref[...]).astype(BF16)
    kaug_scr[:, LANES:KA] = kx_ref[...]
    vt_scr[...] = v_ref[0].astype(F32).T.astype(BF16)
    _store_query_feats(qt_scr, q_ref, bd, qg_ref, [[sl_ref[0]]] * 2)

    vt_rows = [lambda ks: vt_scr[:, ks]] * 2
    st = _flash_pipeline(_causal_blocks(nt, kaug_scr, vt_rows, qt_scr), 2 * nt, LANES)

    lv = lam_ref[...]
    lam = (jnp.exp(jnp.sum(lv[0:1] * lv[1:2], axis=-1, keepdims=True))
           - jnp.exp(jnp.sum(lv[2:3] * lv[3:4], axis=-1, keepdims=True)) + lam_init)
    for qh in range(nt):
        (_, l0, a0), (_, l1, a1) = st[qh], st[nt + qh]
        ob = (a0 / l0 - lam * (a1 / l1)).T
        ob = ob * lax.rsqrt(jnp.mean(ob * ob, axis=-1, keepdims=True) + EPS) * sg_ref[...]
        o_ref[0, qh * BLK:(qh + 1) * BLK, :] = (ob * (1.0 - lam_init)).astype(o_ref.dtype)


def _diff(proj3, kx, slrows, lam_p, qg, kg, sg, bd, lam_init):
    b, s, _ = proj3.shape
    const = lambda a: pl.BlockSpec(a.shape, lambda bi, h: (0,) * a.ndim)
    col = lambda c: pl.BlockSpec((1, s, LANES), lambda bi, h: (bi, 0, c + h))
    return pl.pallas_call(
        functools.partial(_diff_kernel, lam_init=lam_init),
        grid=(b, DIFF_HEADS),
        in_specs=[col(COL_DIFF_Q), col(COL_DIFF_K), col(COL_DIFF_V), const(kx),
                  pl.BlockSpec((1, 8, s), lambda bi, h: (h, 0, 0)),
                  const(lam_p), const(qg), const(kg), const(sg), const(bd)],
        out_specs=pl.BlockSpec((1, s, LANES), lambda bi, h: (bi, 0, h)),
        out_shape=jax.ShapeDtypeStruct((b, s, DIFF_HEADS * LANES), BF16),
        scratch_shapes=_att_scratch(s),
        compiler_params=_cparams(("parallel", "parallel")),
        name="diff",
    )(proj3, proj3, proj3, kx, slrows, lam_p, qg, kg, sg, bd)


def _nsa_select(imp_t, t0):
    ns, tq = imp_t.shape
    blk = lax.broadcasted_iota(jnp.int32, (ns, tq), 0)
    cur = (t0 + lax.broadcasted_iota(jnp.int32, (ns, tq), 1)) // SLC_BLOCK
    forced = (blk == 0) | (blk == cur) | (blk == cur - 1)
    score = jnp.where(blk <= cur, jnp.where(forced, imp_t + FORCE_BONUS, imp_t), NEG)
    rank = jnp.zeros((ns, tq), F32)
    for b in range(ns):
        sb = score[b:b + 1, :]
        beats = (sb > score) | ((sb == score) & (blk > b))
        rank = rank + jnp.where(beats, 1.0, 0.0)
    return jnp.where(rank < float(SLC_TOPN), 1.0, 0.0)


def _slope_rows(slope):
    hi, mid, lo = [float(np.float32(v)) for v in _np_split3(slope * L2E)]
    return [hi, hi, mid, mid, lo, lo]


def _np_split3(x):
    x = np.float32(x)
    hi = np.float32(x.astype(BF16))
    mid = np.float32((x - hi).astype(BF16))
    lo = np.float32((x - hi - mid).astype(BF16))
    return hi, mid, lo


CMP_ROWS = 512


def _nsa_prepare(qall_ref, kc_ref, vc_ref, ks_ref, vs_ref, kw_ref, vw_ref, kx_ref, qg_ref, ksg_ref,
                 kwg_ref, bd_ref, ot_ref, ksaug_scr, kwaug_scr, vst_scr, vwt_scr, notsel_scr,
                 ocmp_scr, *, nc):
    bd = bd_ref[...]
    npair = NSA_HEADS // 2
    seq = ks_ref.shape[1]
    ksaug_scr[:, 0:LANES] = _headnorm_pair(ks_ref[0].astype(F32), bd, ksg_ref[...]).astype(BF16)
    kwaug_scr[:, 0:LANES] = _headnorm_pair(kw_ref[0].astype(F32), bd, kwg_ref[...]).astype(BF16)
    ksaug_scr[:, LANES:KA] = kx_ref[...]
    kwaug_scr[:, LANES:KA] = kx_ref[...]
    vst_scr[...] = vs_ref[0].astype(F32).T.astype(BF16)
    vwt_scr[...] = vw_ref[0].astype(F32).T.astype(BF16)

    lane = lax.broadcasted_iota(jnp.int32, (CMP_ROWS, LANES), 1)
    low = lane < HEAD_DIM
    cend = lane * CMP_STRIDE + (CMP_LEN - 1)
    cend_f = (lax.broadcasted_iota(jnp.int32, (1, LANES), 1) * CMP_STRIDE + (CMP_LEN - 1)).astype(F32)
    kc = kc_ref[0]
    vc = vc_ref[0]

    def chunk(c, carry):
        t0 = pl.multiple_of(c * CMP_ROWS, CMP_ROWS)
        tcol = t0 + lax.broadcasted_iota(jnp.int32, (CMP_ROWS, LANES), 0)
        cmask = (tcol >= cend) & (lane < nc)
        psum = [jnp.zeros((CMP_ROWS, LANES), F32), jnp.zeros((CMP_ROWS, LANES), F32)]
        for r in range(npair):
            q2 = _headnorm_pair(qall_ref[0, pl.ds(t0, CMP_ROWS), r * LANES:(r + 1) * LANES].astype(F32),
                                bd, qg_ref[...])
            halves = []
            for e in range(2):
                slope = 2.0 ** -(r + 1 + 4 * e)
                qm = jnp.where(low if e == 0 else ~low, q2, 0.0).astype(BF16)
                s = jnp.where(cmask, _qk(qm, kc) + (slope * L2E) * cend_f, NEG)
                m = jnp.max(s, axis=-1, keepdims=True)
                pe = jnp.where(cmask, jnp.exp2(s - m), 0.0)
                l = jnp.sum(pe, axis=-1, keepdims=True)
                pn = pe / jnp.where(l > 0.0, l, 1.0)
                psum[e] = psum[e] + pn
                halves.append(jnp.dot(pn.astype(BF16), vc, preferred_element_type=F32))
            ocmp_scr[r, pl.ds(t0, CMP_ROWS), :] = jnp.where(low, halves[0], halves[1])
        for g in range(NSA_GROUPS):
            hi, mid, lo = _split3(psum[g])
            imp_t = (_qk(ot_ref[...], hi.astype(BF16)) + _qk(ot_ref[...], mid.astype(BF16))
                     + _qk(ot_ref[...], lo.astype(BF16)))
            notsel_scr[g, :, pl.ds(t0, CMP_ROWS)] = (_nsa_select(imp_t, t0) - 1.0) * -NEG
        return carry

    lax.fori_loop(0, seq // CMP_ROWS, chunk, 0)


def _nsa_kernel(qall_ref, q_ref, sm_ref, kc_ref, vc_ref, ks_ref, vs_ref, kw_ref, vw_ref, kx_ref,
                slr_ref, qg_ref, ksg_ref, kwg_ref, bd_ref, ot_ref, egt_ref, eg_ref, o_ref,
                ksaug_scr, kwaug_scr, vst_scr, vwt_scr, qts_scr, qtw_scr, notsel_scr, ocmp_scr,
                *, nc):
    r = pl.program_id(1)
    bd = bd_ref[...]
    nt = ks_ref.shape[1] // BLK

    @pl.when(r == 0)
    def _():
        _nsa_prepare(qall_ref, kc_ref, vc_ref, ks_ref, vs_ref, kw_ref, vw_ref, kx_ref, qg_ref,
                     ksg_ref, kwg_ref, bd_ref, ot_ref, ksaug_scr, kwaug_scr, vst_scr, vwt_scr,
                     notsel_scr, ocmp_scr, nc=nc)

    q2 = _headnorm_pair(q_ref[0].astype(F32), bd, qg_ref[...])
    for e in range(2):
        qts_scr[e] = _query_feats(q2, e, [slr_ref[0, e], notsel_scr[e]])
        qtw_scr[e] = _query_feats(q2, e, [slr_ref[0, e]])

    key = lax.broadcasted_iota(jnp.int32, (BLK, BLK), 0)
    qry = lax.broadcasted_iota(jnp.int32, (BLK, BLK), 1)
    diag = key <= qry
    above = key > qry
    rows = [slice(e * HEAD_DIM, (e + 1) * HEAD_DIM) for e in range(2)]
    vs_rows = [lambda ks, e=e: vst_scr[rows[e], ks] for e in range(2)]
    slc = _causal_blocks(nt, ksaug_scr, vs_rows, qts_scr)

    def win_block(j, qh, e, mask):
        ks = slice(j * BLK, (j + 1) * BLK)
        qs = slice(qh * BLK, (qh + 1) * BLK)
        return (lambda: kwaug_scr[ks, :], lambda: vwt_scr[rows[e], ks], lambda: qtw_scr[e, :, qs],
                mask, 2 * nt + e * nt + qh)

    blocks = []
    pos = 0
    for j in range(nt):
        n_j = 2 * (nt - j)
        blocks.extend(slc[pos:pos + n_j])
        pos += n_j
        blocks.extend(win_block(j, j, e, diag) for e in range(2))
        if j > 0:
            blocks.extend(win_block(j - 1, j, e, above) for e in range(2))
    st = _flash_pipeline(blocks, 4 * nt, HEAD_DIM)

    gate = jax.nn.sigmoid(sm_ref[0])
    gate_t = gate.T
    g_cmp = _dot01(gate, eg_ref[0])
    g_slc_t = _dot01_left(egt_ref[1], gate_t)
    g_win_t = _dot01_left(egt_ref[2], gate_t)
    for qh in range(nt):
        cols = slice(qh * BLK, (qh + 1) * BLK)
        o_s = jnp.concatenate([st[e * nt + qh][2] / st[e * nt + qh][1] for e in range(2)], axis=0)
        o_w = jnp.concatenate([st[2 * nt + e * nt + qh][2] / st[2 * nt + e * nt + qh][1]
                               for e in range(2)], axis=0)
        mix = (g_slc_t[:, cols] * o_s + g_win_t[:, cols] * o_w).T
        out = mix + g_cmp[cols, :] * ocmp_scr[r, cols, :]
        o_ref[0, cols, :] = out.astype(o_ref.dtype)


def _nsa(proj3, small3, kc, vc, kx, slr, qg, ksg, kwg, bd, ot, egt, eg):
    b, s, _ = proj3.shape
    npair = NSA_HEADS // 2
    ns = s // SLC_BLOCK
    nc = (s - CMP_LEN) // CMP_STRIDE + 1
    const = lambda a: pl.BlockSpec(a.shape, lambda bi, r: (0,) * a.ndim)
    kvspec = lambda c: pl.BlockSpec((1, s, LANES), lambda bi, r: (bi, 0, c))
    return pl.pallas_call(
        functools.partial(_nsa_kernel, nc=nc),
        grid=(b, npair),
        in_specs=[
            pl.BlockSpec((1, s, npair * LANES), lambda bi, r: (bi, 0, COL_NSA_Q // npair)),
            pl.BlockSpec((1, s, LANES), lambda bi, r: (bi, 0, COL_NSA_Q + r)),
            pl.BlockSpec((1, s, LANES), lambda bi, r: (bi, 0, 0)),
            pl.BlockSpec((1,) + kc.shape[1:], lambda bi, r: (bi, 0, 0)),
            pl.BlockSpec((1,) + vc.shape[1:], lambda bi, r: (bi, 0, 0)),
            kvspec(COL_K_SLC), kvspec(COL_V_SLC), kvspec(COL_K_WIN), kvspec(COL_V_WIN),
            const(kx),
            pl.BlockSpec((1, 2, 8, s), lambda bi, r: (r, 0, 0, 0)),
            const(qg), const(ksg), const(kwg), const(bd), const(ot),
            pl.BlockSpec((3, LANES, LANES), lambda bi, r: (0, r, 0)),
            pl.BlockSpec((3, LANES, LANES), lambda bi, r: (0, 0, r)),
        ],
        out_specs=pl.BlockSpec((1, s, LANES), lambda bi, r: (bi, 0, r)),
        out_shape=jax.ShapeDtypeStruct((b, s, npair * LANES), BF16),
        scratch_shapes=[pltpu.VMEM((s, KA), BF16), pltpu.VMEM((s, KA), BF16),
                        pltpu.VMEM((LANES, s), BF16), pltpu.VMEM((LANES, s), BF16),
                        pltpu.VMEM((2, KA, s), BF16), pltpu.VMEM((2, KA, s), BF16),
                        pltpu.VMEM((NSA_GROUPS, ns, s), F32), pltpu.VMEM((npair, s, LANES), F32)],
        compiler_params=_cparams(("parallel", "arbitrary")),
        name="nsa",
    )(proj3, proj3, small3, kc, vc, proj3, proj3, proj3, proj3, kx, slr, qg, ksg, kwg, bd, ot,
      egt, eg)


def _merge_kernel(oa_ref, ob_ref, oc_ref, g0_ref, g1_ref, g2_ref, h_ref, wbr_ref, wo_ref, o_ref):
    merged = None
    for c, (o_r, g_r) in enumerate(((oa_ref, g0_ref), (ob_ref, g1_ref), (oc_ref, g2_ref))):
        y = jnp.dot(o_r[...], wbr_ref[c], preferred_element_type=F32)
        term = jax.nn.sigmoid(g_r[...].astype(F32)) * y
        merged = term if merged is None else merged + term
    o_ref[...] = h_ref[...] + jnp.dot(merged.astype(BF16), wo_ref[...], preferred_element_type=F32)


def _merge(oa, ob, oc, proj2, h2, wbr, wo, *, tm=512):
    t, d = h2.shape
    bw = oa.shape[1]
    row = lambda w: pl.BlockSpec((tm, w), lambda i: (i, 0))
    gate = lambda c: pl.BlockSpec((tm, d), lambda i: (i, COL_MG * LANES // d + c))
    return pl.pallas_call(
        _merge_kernel,
        grid=(t // tm,),
        in_specs=[row(bw), row(bw), row(bw), gate(0), gate(1), gate(2), row(d),
                  pl.BlockSpec(wbr.shape, lambda i: (0, 0, 0)),
                  pl.BlockSpec(wo.shape, lambda i: (0, 0))],
        out_specs=row(d),
        out_shape=jax.ShapeDtypeStruct((t, d), F32),
        compiler_params=_cparams(("parallel",)),
        name="merge",
    )(oa, ob, oc, proj2, proj2, proj2, h2, wbr, wo)


HALO = 16


def _ffn_kernel(x_ref, xh_ref, g_ref, wu_ref, wg_ref, cw_ref, cb_ref, wd_ref, o_ref,
                a_scr, u_scr, acc_scr, *, tm, tiles_per_seq):
    i = pl.program_id(0)
    j = pl.program_id(1)
    nj = pl.num_programs(1)

    @pl.when(j == 0)
    def _():
        def norm(x):
            return (x * lax.rsqrt(jnp.mean(x * x, axis=-1, keepdims=True) + EPS)
                    * g_ref[...]).astype(BF16)
        a_scr[0:HALO, :] = norm(xh_ref[...])
        a_scr[HALO:, :] = norm(x_ref[...])
        acc_scr[...] = jnp.zeros_like(acc_scr)

    u = jnp.dot(a_scr[...], wu_ref[...], preferred_element_type=F32)
    rows = lax.broadcasted_iota(jnp.int32, u.shape, 0)
    seq_start = (i % tiles_per_seq) == 0
    u_scr[...] = jnp.where((rows < HALO) & seq_start, 0.0, u)
    cw = cw_ref[...]
    uc = cb_ref[...]
    for t in range(CONV_W):
        uc = uc + cw[t:t + 1, :] * u_scr[pl.ds(HALO - (CONV_W - 1) + t, tm), :]
    gt = jnp.dot(a_scr[HALO:, :], wg_ref[...], preferred_element_type=F32)
    act = (jax.nn.gelu(uc) * gt).astype(BF16)
    acc_scr[...] += jnp.dot(act, wd_ref[...], preferred_element_type=F32)

    @pl.when(j == nj - 1)
    def _():
        o_ref[...] = x_ref[...] + acc_scr[...]


def _ffn(h2, g, wup, cw, cb, wd, seq, *, tm=512, tf=1408):
    t, d = h2.shape
    dff = wd.shape[0]
    nf = dff // tf
    return pl.pallas_call(
        functools.partial(_ffn_kernel, tm=tm, tiles_per_seq=seq // tm),
        grid=(t // tm, nf),
        in_specs=[
            pl.BlockSpec((tm, d), lambda i, j: (i, 0)),
            pl.BlockSpec((HALO, d), lambda i, j: (jnp.maximum(i * (tm // HALO) - 1, 0), 0)),
            pl.BlockSpec((1, d), lambda i, j: (0, 0)),
            pl.BlockSpec((d, tf), lambda i, j: (0, j)),
            pl.BlockSpec((d, tf), lambda i, j: (0, nf + j)),
            pl.BlockSpec((CONV_W, tf), lambda i, j: (0, j)),
            pl.BlockSpec((1, tf), lambda i, j: (0, j)),
            pl.BlockSpec((tf, d), lambda i, j: (j, 0)),
        ],
        out_specs=pl.BlockSpec((tm, d), lambda i, j: (i, 0)),
        out_shape=jax.ShapeDtypeStruct((t, d), F32),
        scratch_shapes=[pltpu.VMEM((HALO + tm, d), BF16), pltpu.VMEM((HALO + tm, tf), F32),
                        pltpu.VMEM((tm, d), F32)],
        compiler_params=_cparams(("parallel", "arbitrary")),
        name="ffn",
    )(h2, h2, g, wup, wup, cw, cb, wd)


def _nsa_head_order():
    hpg = NSA_HEADS // NSA_GROUPS
    return [h for r in range(hpg) for h in (r, hpg + r)]


def _constants(seq):
    ns = seq // SLC_BLOCK
    nc = (seq - CMP_LEN) // CMP_STRIDE + 1
    assert MASK0 + ns <= LANES and MASK0 >= 8 and N_BIAS <= 8
    bd = np.kron(np.eye(2), np.ones((HEAD_DIM, HEAD_DIM))).astype(np.float32)
    c_start = np.arange(LANES) * CMP_STRIDE
    s_start = np.arange(ns) * SLC_BLOCK
    ot = ((c_start[None, :] < s_start[:, None] + SLC_BLOCK)
          & (c_start[None, :] + CMP_LEN > s_start[:, None])
          & (np.arange(LANES)[None, :] < nc)).astype(np.float32)
    pos = np.arange(seq)
    kx = np.zeros((seq, LANES), np.float32)
    for k in range(N_BIAS):
        kx[:, k] = (pos // 8) * 8 if k % 2 == 0 else pos % 8
    kx[pos, MASK0 + pos // SLC_BLOCK] = 1.0
    order = _nsa_head_order()
    eg = np.zeros((3, LANES, NSA_HEADS * HEAD_DIM), np.float32)
    for c in range(3):
        for slot, h in enumerate(order):
            eg[c, SMALL_GATE0 + c * NSA_HEADS + h, slot * HEAD_DIM:(slot + 1) * HEAD_DIM] = 1.0
    u = (np.arange(BLK)[:, None] <= np.arange(BLK)[None, :]).astype(np.float32)
    dsl = np.zeros((DIFF_HEADS, 8, seq), np.float32)
    for h in range(DIFF_HEADS):
        rows = _slope_rows(2.0 ** (-8.0 * (h + 1) / DIFF_HEADS))
        dsl[h, :N_BIAS, :] = np.asarray(rows, np.float32)[:, None]
    nsl = np.zeros((NSA_HEADS // 2, 2, 8, seq), np.float32)
    for slot, h in enumerate(order):
        rows = _slope_rows(2.0 ** (-8.0 * (h + 1) / NSA_HEADS))
        nsl[slot // 2, slot % 2, :N_BIAS, :] = np.asarray(rows, np.float32)[:, None]
    as_bf = lambda a: jnp.asarray(a, BF16)
    return dict(bd=as_bf(bd), ot=as_bf(ot), kx=as_bf(kx), eg=as_bf(eg),
                egt=as_bf(np.transpose(eg, (0, 2, 1))), u=as_bf(u), dsl=jnp.asarray(dsl, F32),
                nsl=jnp.asarray(nsl, F32))


def _pack_w_in(w):
    d = w.shape[0]
    hd = HEAD_DIM
    sizes = [512, 768, 24, 512, 512, 512, 512, 512, 512, 8, 3 * d]
    offs = np.concatenate([[0], np.cumsum(sizes)])
    nq, nkv, ngate, dq, dk, dv, fq, fk, fv, ff, mg = [w[:, offs[k]:offs[k + 1]] for k in range(11)]
    nq = jnp.concatenate([nq[:, h * hd:(h + 1) * hd] for h in _nsa_head_order()], axis=1)
    kv = lambda c, k: nkv[:, (c * 2 + k) * LANES:(c * 2 + k + 1) * LANES]
    cols = [mg, nq, dq, dk, dv, fq, fk, fv,
            kv(1, 0), kv(1, 1), kv(2, 0), kv(2, 1), kv(0, 0), kv(0, 1)]
    main = jnp.concatenate(cols, axis=1)
    main = jnp.pad(main, ((0, 0), (0, PROJ_UNITS * LANES - main.shape[1])))
    small = jnp.pad(jnp.concatenate([ngate, ff], axis=1), ((0, 0), (0, LANES - 32)))
    return main.astype(BF16), small.astype(BF16)


def kernel(x, attn_norm_g, w_in, nsa_q_g, nsa_k_g, cmp_pe, cmp_w1, cmp_w2, diff_q_g, diff_k_g,
           diff_lam, diff_subln_g, fox_q_g, fox_k_g, fox_b, w_br, w_o, ffn_norm_g, w_up, conv_w,
           conv_b, w_down):
    b, s, d = x.shape
    depth = w_in.shape[0]
    hd = HEAD_DIM
    qscale = hd ** -0.5 * L2E
    cst = _constants(s)
    rows16 = s // CMP_STRIDE
    assert rows16 == LANES and s % CMP_ROWS == 0, "NSA kernel keeps all compressed blocks in one 128-lane tile"
    order = _nsa_head_order()
    tile2 = lambda g: jnp.tile(g, 2).reshape(1, LANES).astype(F32)

    h = x.reshape(b * s, d)
    for l in range(depth):
        wmain, wsmall = _pack_w_in(w_in[l])
        proj, small = _proj(h, attn_norm_g[l].reshape(1, d), wmain, wsmall)
        proj3 = proj.reshape(b, s, PROJ_UNITS * LANES)
        small3 = small.reshape(b, s, LANES)

        raw = proj3[:, :, COL_K_CMP * LANES:(COL_V_CMP + 1) * LANES]
        raw = raw.reshape(b, rows16, CMP_STRIDE, 2 * NSA_GROUPS, hd)
        raw = jnp.transpose(raw, (0, 3, 1, 2, 4)).reshape(b, 2 * NSA_GROUPS, rows16, CMP_STRIDE * hd)
        half = CMP_STRIDE * hd
        w1cat = jnp.concatenate([cmp_w1[l][:, :half], cmp_w1[l][:, half:]], axis=2).astype(BF16)
        pe8 = jnp.broadcast_to(cmp_pe[l].reshape(2, 1, CMP_LEN * hd), (2, 8, CMP_LEN * hd)).astype(BF16)
        kc, vc = _compress(raw, w1cat, pe8, cmp_w1[l].astype(BF16), cmp_w2[l].astype(BF16),
                           nsa_k_g[l, 0].reshape(1, hd))

        fb_row = jnp.zeros((1, LANES), F32).at[0, SMALL_FF0:SMALL_FF0 + FOX_HEADS].set(fox_b[l])
        negf = _fcum(small3, fb_row, cst["u"])

        o_a = _nsa(proj3, small3, kc, vc, cst["kx"], cst["nsl"], tile2(nsa_q_g[l] * qscale),
                   tile2(nsa_k_g[l, 1]), tile2(nsa_k_g[l, 2]), cst["bd"], cst["ot"], cst["egt"],
                   cst["eg"])
        lam_init = 0.8 - 0.6 * math.exp(-0.3 * l)
        o_b = _diff(proj3, cst["kx"], cst["dsl"], diff_lam[l], tile2(diff_q_g[l] * qscale),
                    tile2(diff_k_g[l]), diff_subln_g[l].reshape(1, LANES), cst["bd"], lam_init)
        o_c = _fox(proj3, negf, tile2(fox_q_g[l] * qscale), tile2(fox_k_g[l]), cst["bd"])

        wbr = w_br[l]
        wbr_a = jnp.concatenate([wbr[0, hh * hd:(hh + 1) * hd] for hh in order], axis=0)
        wbr_p = jnp.stack([wbr_a, wbr[1], wbr[2]]).astype(BF16)
        bw = NSA_HEADS * hd
        h = _merge(o_a.reshape(b * s, bw), o_b.reshape(b * s, bw), o_c.reshape(b * s, bw),
                   proj, h, wbr_p, w_o[l].astype(BF16))
        h = _ffn(h, ffn_norm_g[l].reshape(1, d), w_up[l].astype(BF16), conv_w[l],
                 conv_b[l].reshape(1, -1), w_down[l].astype(BF16), s)
    return h.reshape(b, s, d)
```

```python
import functools
import math

import numpy as np
import jax
import jax.numpy as jnp
from jax import lax
from jax.experimental import pallas as pl
from jax.experimental.pallas import tpu as pltpu

F32 = jnp.float32
BF16 = jnp.bfloat16

HEAD_DIM = 64
NSA_HEADS = 8
NSA_GROUPS = 2
CMP_LEN = 32
CMP_STRIDE = 16
SLC_BLOCK = 64
SLC_TOPN = 8
WINDOW = 256
FORCE_BONUS = 1.0e4
DIFF_HEADS = 4
FOX_HEADS = 8
CONV_W = 3
EPS = 1e-6
NEG = -1.0e30
L2E = 1.4426950408889634

LANES = 128
BLK = 256
KA = 2 * LANES
N_BIAS = 6
MASK0 = 8
BOUND0 = 40
VAUG = 16
BOUND_SLACK = 1.02
BOUND_LIMIT = 100.0
VMEM_LIMIT = 56 * 1024 * 1024

COL_MG = 0
COL_NSA_Q = 24
COL_DIFF_Q, COL_DIFF_K, COL_DIFF_V = 28, 32, 36
COL_FOX_Q, COL_FOX_K, COL_FOX_V = 40, 44, 48
COL_K_SLC, COL_V_SLC, COL_K_WIN, COL_V_WIN, COL_K_CMP, COL_V_CMP = 52, 53, 54, 55, 56, 57
PROJ_UNITS = 60
SMALL_GATE0 = 0
SMALL_FF0 = 24


def _cparams(sem):
    return pltpu.CompilerParams(dimension_semantics=sem, vmem_limit_bytes=VMEM_LIMIT)


def _split3(x):
    hi = x.astype(BF16).astype(F32)
    r1 = x - hi
    mid = r1.astype(BF16).astype(F32)
    lo = (r1 - mid).astype(BF16).astype(F32)
    return hi, mid, lo


def _dot01(x, m):
    hi = x.astype(BF16)
    lo = (x - hi.astype(F32)).astype(BF16)
    return (jnp.dot(hi, m, preferred_element_type=F32)
            + jnp.dot(lo, m, preferred_element_type=F32))


def _dot01_left(m, x):
    hi = x.astype(BF16)
    lo = (x - hi.astype(F32)).astype(BF16)
    return (jnp.dot(m, hi, preferred_element_type=F32)
            + jnp.dot(m, lo, preferred_element_type=F32))


def _dot01_3(x, m):
    hi, mid, lo = _split3(x)
    return (jnp.dot(hi.astype(BF16), m, preferred_element_type=F32)
            + jnp.dot(mid.astype(BF16), m, preferred_element_type=F32)
            + jnp.dot(lo.astype(BF16), m, preferred_element_type=F32))


def _qk(q, k):
    return lax.dot_general(q, k, (((1,), (1,)), ((), ())), preferred_element_type=F32)


def _headnorm_pair(x, bd, gain):
    ss = jnp.dot((x * x).astype(BF16), bd, preferred_element_type=F32)
    return x * lax.rsqrt(ss * (1.0 / HEAD_DIM) + EPS) * gain


def _rows8(vals, width):
    row = lax.broadcasted_iota(jnp.int32, (8, width), 0)
    out = jnp.zeros((8, width), F32)
    for k, v in enumerate(vals):
        out = jnp.where(row == k, v, out)
    return out


def _query_feats(qt, half, bias8, mask_rows, bound_row):
    tq = qt.shape[1]
    row = lax.broadcasted_iota(jnp.int32, (LANES, tq), 0)
    keep = (row < HEAD_DIM) if half == 0 else (row >= HEAD_DIM)
    nmask = BOUND0 - MASK0
    parts = [jnp.where(keep, qt, 0.0), bias8,
             mask_rows if mask_rows is not None else jnp.zeros((nmask, tq), F32),
             _rows8(_split3(-bound_row), tq) if bound_row is not None else jnp.zeros((8, tq), F32),
             jnp.zeros((LANES - BOUND0 - 8, tq), F32)]
    return jnp.concatenate(parts, axis=0).astype(BF16)


def _logit_bound(qg_ref, kg_refs):
    kmax = None
    for kg_ref in kg_refs:
        k = jnp.max(jnp.abs(kg_ref[...]), axis=-1, keepdims=True)
        kmax = k if kmax is None else jnp.maximum(kmax, k)
    return jnp.max(jnp.abs(qg_ref[...]), axis=-1, keepdims=True) * kmax * (HEAD_DIM * BOUND_SLACK)


def _store_values_t(vt_view, rows, seq):
    dv = rows.shape[0]
    vt_view[0:dv, :] = rows.astype(BF16)
    vt_view[dv:dv + 8, :] = _rows8([1.0], seq).astype(BF16)
    vt_view[dv + 8:dv + VAUG, :] = jnp.zeros((VAUG - 8, seq), BF16)


QK_AHEAD = 2


def _flash(blocks, n_chain, dv, bounded):
    acc = [jnp.zeros((dv + VAUG, BLK), F32) for _ in range(n_chain)]
    mx = [jnp.full((1, BLK), NEG, F32) for _ in range(n_chain)]
    scores = {}

    def issue(k):
        if k < len(blocks):
            scores[k] = jnp.dot(blocks[k][0](), blocks[k][2](), preferred_element_type=F32)

    for k in range(2 * QK_AHEAD):
        issue(k)
    for k0 in range(0, len(blocks), 2):
        issue(k0 + 2 * QK_AHEAD)
        issue(k0 + 2 * QK_AHEAD + 1)
        pending = []
        for k in range(k0, min(k0 + 2, len(blocks))):
            _, vt, _, mask, chain = blocks[k]
            s = scores.pop(k)
            if mask is not None:
                s = jnp.where(mask, s, NEG)
            if bounded:
                pending.append((vt, chain, None, jnp.exp2(s).astype(BF16)))
            else:
                m_new = jnp.maximum(mx[chain], jnp.max(s, axis=0, keepdims=True))
                alpha = jnp.exp2(mx[chain] - m_new)
                mx[chain] = m_new
                pending.append((vt, chain, alpha, jnp.exp2(s - m_new).astype(BF16)))
        assert len({c for _, c, _, _ in pending}) == len(pending), "a pair must not share a chain"
        for vt, chain, alpha, p in pending:
            prev = acc[chain] if alpha is None else alpha * acc[chain]
            acc[chain] = prev + jnp.dot(vt(), p, preferred_element_type=F32)
    return acc


def _causal_blocks(nt, kaug_scr, vt_rows, qt_scr):
    key = lax.broadcasted_iota(jnp.int32, (BLK, BLK), 0)
    qry = lax.broadcasted_iota(jnp.int32, (BLK, BLK), 1)
    diag = key <= qry
    blocks = []
    for j in range(nt):
        ks = slice(j * BLK, (j + 1) * BLK)
        for qh in range(j, nt):
            qs = slice(qh * BLK, (qh + 1) * BLK)
            for e in range(2):
                blocks.append((lambda ks=ks: kaug_scr[ks, :],
                               lambda e=e, ks=ks: vt_rows[e](ks),
                               lambda e=e, qs=qs: qt_scr[e, :, qs],
                               diag if qh == j else None, e * nt + qh))
    return blocks


def _proj_kernel(x_ref, g_ref, w_ref, ws_ref, o_ref, os_ref, a_scr):
    j = pl.program_id(1)

    @pl.when(j == 0)
    def _():
        x = x_ref[...]
        inv = lax.rsqrt(jnp.mean(x * x, axis=-1, keepdims=True) + EPS)
        a = (x * inv * g_ref[...]).astype(BF16)
        a_scr[...] = a
        os_ref[...] = jnp.dot(a, ws_ref[...], preferred_element_type=F32)

    o_ref[...] = jnp.dot(a_scr[...], w_ref[...], preferred_element_type=F32).astype(o_ref.dtype)


def _proj(x2d, g, w, ws, *, tm=1024, tn=1536):
    t, d = x2d.shape
    n = w.shape[1]
    return pl.pallas_call(
        _proj_kernel,
        grid=(t // tm, n // tn),
        in_specs=[
            pl.BlockSpec((tm, d), lambda i, j: (i, 0)),
            pl.BlockSpec((1, d), lambda i, j: (0, 0)),
            pl.BlockSpec((d, tn), lambda i, j: (0, j)),
            pl.BlockSpec((d, LANES), lambda i, j: (0, 0)),
        ],
        out_specs=[
            pl.BlockSpec((tm, tn), lambda i, j: (i, j)),
            pl.BlockSpec((tm, LANES), lambda i, j: (i, 0)),
        ],
        out_shape=[jax.ShapeDtypeStruct((t, n), BF16), jax.ShapeDtypeStruct((t, LANES), F32)],
        scratch_shapes=[pltpu.VMEM((tm, d), BF16)],
        compiler_params=_cparams(("parallel", "arbitrary")),
        name="proj",
    )(x2d, g, w, ws)


def _compress_kernel(r_ref, w1_ref, pe_ref, w1f_ref, w2_ref, kg_ref, kc_ref, vc_ref):
    outs = []
    for kv in range(2):
        c1 = jnp.dot(pe_ref[kv], w1f_ref[kv], preferred_element_type=F32)[0:1]
        per_group = []
        for g in range(NSA_GROUPS):
            ab = jnp.dot(r_ref[0, kv * NSA_GROUPS + g], w1_ref[kv], preferred_element_type=F32)
            pre = ab[:, :LANES] + pltpu.roll(ab[:, LANES:], LANES - 1, 0) + c1
            hid = jax.nn.gelu(pre).astype(BF16)
            o = jnp.dot(hid, w2_ref[kv], preferred_element_type=F32)
            if kv == 0:
                o = o * lax.rsqrt(jnp.mean(o * o, axis=-1, keepdims=True) + EPS) * kg_ref[...]
            per_group.append(o)
        outs.append(jnp.concatenate(per_group, axis=-1))
    kc_ref[0] = outs[0].astype(BF16)
    vc_ref[0] = outs[1].astype(BF16)


def _compress(r, w1cat, pe8, w1f, w2, kg):
    b = r.shape[0]
    nrow = r.shape[2]
    full = lambda a: pl.BlockSpec(a.shape, lambda i: (0,) * a.ndim)
    return pl.pallas_call(
        _compress_kernel,
        grid=(b,),
        in_specs=[pl.BlockSpec((1,) + r.shape[1:], lambda i: (i, 0, 0, 0)),
                  full(w1cat), full(pe8), full(w1f), full(w2), full(kg)],
        out_specs=[pl.BlockSpec((1, nrow, LANES), lambda i: (i, 0, 0)),
                   pl.BlockSpec((1, nrow, LANES), lambda i: (i, 0, 0))],
        out_shape=[jax.ShapeDtypeStruct((b, nrow, LANES), BF16),
                   jax.ShapeDtypeStruct((b, nrow, LANES), BF16)],
        compiler_params=_cparams(("parallel",)),
        name="compress",
    )(r, w1cat, pe8, w1f, w2, kg)


def _fcum_kernel(s_ref, fb_ref, u_ref, o_ref, *, chunk):
    z = s_ref[0] + fb_ref[...]
    lf = jax.nn.log_sigmoid(z)
    lft = lf.T[SMALL_FF0:SMALL_FF0 + FOX_HEADS]
    seq = lft.shape[1]
    carry = jnp.zeros((FOX_HEADS, 1), F32)
    for c in range(seq // chunk):
        fc = _dot01_3(lft[:, c * chunk:(c + 1) * chunk], u_ref[...]) + carry
        o_ref[0, :, c * chunk:(c + 1) * chunk] = -fc
        carry = fc[:, chunk - 1:chunk]


def _fcum(small3, fb_row, u):
    b, s, _ = small3.shape
    chunk = u.shape[0]
    return pl.pallas_call(
        functools.partial(_fcum_kernel, chunk=chunk),
        grid=(b,),
        in_specs=[pl.BlockSpec((1, s, LANES), lambda i: (i, 0, 0)),
                  pl.BlockSpec((1, LANES), lambda i: (0, 0)),
                  pl.BlockSpec(u.shape, lambda i: (0, 0))],
        out_specs=pl.BlockSpec((1, FOX_HEADS, s), lambda i: (i, 0, 0)),
        out_shape=jax.ShapeDtypeStruct((b, FOX_HEADS, s), F32),
        compiler_params=_cparams(("parallel",)),
        name="fcum",
    )(small3, fb_row, u)


def _fox_kernel(q_ref, k_ref, v_ref, nf_ref, qg_ref, kg_ref, bd_ref, o_ref,
                kaug_scr, vt_scr, qt_scr, *, bounded):
    p = pl.program_id(1)
    bd = bd_ref[...]
    seq = k_ref.shape[1]
    nt = seq // BLK

    kaug_scr[:, 0:LANES] = _headnorm_pair(k_ref[0].astype(F32), bd, kg_ref[...]).astype(BF16)
    cb = [nf_ref[0, pl.ds(2 * p + e, 1), :] * L2E for e in range(2)]
    rows = []
    for e in range(2):
        rows.extend(_split3(cb[e]))
    feats = jnp.concatenate([_rows8(rows, seq), jnp.zeros((BOUND0 - 8, seq), F32),
                             _rows8([1.0] * 3, seq), jnp.zeros((LANES - BOUND0 - 8, seq), F32)],
                            axis=0)
    kaug_scr[:, LANES:KA] = feats.T.astype(BF16)
    vt = v_ref[0].astype(F32).T
    for e in range(2):
        _store_values_t(vt_scr.at[e], vt[e * HEAD_DIM:(e + 1) * HEAD_DIM], seq)

    qk_bound = _logit_bound(qg_ref, [kg_ref])
    qt = _headnorm_pair(q_ref[0].astype(F32), bd, qg_ref[...]).T
    for e in range(2):
        qt_scr[e] = _query_feats(qt, e, _rows8([0.0] * (3 * e) + [1.0] * 3, seq), None,
                                 qk_bound + cb[e] if bounded else None)

    vt_rows = [lambda ks, e=e: vt_scr[e, :, ks] for e in range(2)]
    acc = _flash(_causal_blocks(nt, kaug_scr, vt_rows, qt_scr), 2 * nt, HEAD_DIM, bounded)
    for qh in range(nt):
        ot = jnp.concatenate([acc[e * nt + qh][0:HEAD_DIM] / acc[e * nt + qh][HEAD_DIM:HEAD_DIM + 1]
                              for e in range(2)], axis=0)
        o_ref[0, qh * BLK:(qh + 1) * BLK, :] = ot.T.astype(o_ref.dtype)


def _fox(proj3, negf, qg, kg, bd, *, bounded):
    b, s, _ = proj3.shape
    npair = FOX_HEADS // 2
    const = lambda a: pl.BlockSpec(a.shape, lambda bi, p: (0,) * a.ndim)
    col = lambda c: pl.BlockSpec((1, s, LANES), lambda bi, p: (bi, 0, c + p))
    return pl.pallas_call(
        functools.partial(_fox_kernel, bounded=bounded),
        grid=(b, npair),
        in_specs=[col(COL_FOX_Q), col(COL_FOX_K), col(COL_FOX_V),
                  pl.BlockSpec((1, FOX_HEADS, s), lambda bi, p: (bi, 0, 0)),
                  const(qg), const(kg), const(bd)],
        out_specs=pl.BlockSpec((1, s, LANES), lambda bi, p: (bi, 0, p)),
        out_shape=jax.ShapeDtypeStruct((b, s, npair * LANES), BF16),
        scratch_shapes=[pltpu.VMEM((s, KA), BF16), pltpu.VMEM((2, HEAD_DIM + VAUG, s), BF16),
                        pltpu.VMEM((2, KA, s), BF16)],
        compiler_params=_cparams(("parallel", "parallel")),
        name="fox",
    )(proj3, proj3, proj3, negf, qg, kg, bd)


def _alibi_at_query(sl_rows, seq):
    pos = lax.broadcasted_iota(jnp.int32, (1, seq), 1).astype(F32)
    return (sl_rows[0:1] + sl_rows[2:3] + sl_rows[4:5]) * pos


def _diff_kernel(q_ref, k_ref, v_ref, kx_ref, sl_ref, lam_ref, qg_ref, kg_ref, sg_ref, bd_ref,
                 o_ref, kaug_scr, vt_scr, qt_scr, *, lam_init, bounded):
    bd = bd_ref[...]
    seq = k_ref.shape[1]
    nt = seq // BLK

    kaug_scr[:, 0:LANES] = _headnorm_pair(k_ref[0].astype(F32), bd, kg_ref[...]).astype(BF16)
    kaug_scr[:, LANES:KA] = kx_ref[...]
    _store_values_t(vt_scr, v_ref[0].astype(F32).T, seq)
    sl = sl_ref[0]
    bound = _logit_bound(qg_ref, [kg_ref]) + _alibi_at_query(sl, seq) if bounded else None
    qt = _headnorm_pair(q_ref[0].astype(F32), bd, qg_ref[...]).T
    for e in range(2):
        qt_scr[e] = _query_feats(qt, e, sl, None, bound)

    vt_rows = [lambda ks: vt_scr[:, ks]] * 2
    acc = _flash(_causal_blocks(nt, kaug_scr, vt_rows, qt_scr), 2 * nt, LANES, bounded)

    lv = lam_ref[...]
    lam = (jnp.exp(jnp.sum(lv[0:1] * lv[1:2], axis=-1, keepdims=True))
           - jnp.exp(jnp.sum(lv[2:3] * lv[3:4], axis=-1, keepdims=True)) + lam_init)
    for qh in range(nt):
        a0, a1 = acc[qh], acc[nt + qh]
        ob = (a0[0:LANES] / a0[LANES:LANES + 1] - lam * (a1[0:LANES] / a1[LANES:LANES + 1])).T
        ob = ob * lax.rsqrt(jnp.mean(ob * ob, axis=-1, keepdims=True) + EPS) * sg_ref[...]
        o_ref[0, qh * BLK:(qh + 1) * BLK, :] = (ob * (1.0 - lam_init)).astype(o_ref.dtype)


def _diff(proj3, kx, slrows, lam_p, qg, kg, sg, bd, lam_init, *, bounded):
    b, s, _ = proj3.shape
    const = lambda a: pl.BlockSpec(a.shape, lambda bi, h: (0,) * a.ndim)
    col = lambda c: pl.BlockSpec((1, s, LANES), lambda bi, h: (bi, 0, c + h))
    return pl.pallas_call(
        functools.partial(_diff_kernel, lam_init=lam_init, bounded=bounded),
        grid=(b, DIFF_HEADS),
        in_specs=[col(COL_DIFF_Q), col(COL_DIFF_K), col(COL_DIFF_V), const(kx),
                  pl.BlockSpec((1, 8, s), lambda bi, h: (h, 0, 0)),
                  const(lam_p), const(qg), const(kg), const(sg), const(bd)],
        out_specs=pl.BlockSpec((1, s, LANES), lambda bi, h: (bi, 0, h)),
        out_shape=jax.ShapeDtypeStruct((b, s, DIFF_HEADS * LANES), BF16),
        scratch_shapes=[pltpu.VMEM((s, KA), BF16), pltpu.VMEM((LANES + VAUG, s), BF16),
                        pltpu.VMEM((2, KA, s), BF16)],
        compiler_params=_cparams(("parallel", "parallel")),
        name="diff",
    )(proj3, proj3, proj3, kx, slrows, lam_p, qg, kg, sg, bd)


def _nsa_select(imp_t, t0):
    ns, tq = imp_t.shape
    blk = lax.broadcasted_iota(jnp.int32, (ns, tq), 0)
    cur = (t0 + lax.broadcasted_iota(jnp.int32, (ns, tq), 1)) // SLC_BLOCK
    forced = (blk == 0) | (blk == cur) | (blk == cur - 1)
    score = jnp.where(blk <= cur, jnp.where(forced, imp_t + FORCE_BONUS, imp_t), NEG)
    rank = jnp.zeros((ns, tq), F32)
    for b in range(ns):
        sb = score[b:b + 1, :]
        beats = (sb > score) | ((sb == score) & (blk > b))
        rank = rank + jnp.where(beats, 1.0, 0.0)
    return jnp.where(rank < float(SLC_TOPN), 1.0, 0.0)


def _slope_rows(slope):
    hi, mid, lo = [float(np.float32(v)) for v in _np_split3(slope * L2E)]
    return [hi, hi, mid, mid, lo, lo]


def _np_split3(x):
    x = np.float32(x)
    hi = np.float32(x.astype(BF16))
    mid = np.float32((x - hi).astype(BF16))
    lo = np.float32((x - hi - mid).astype(BF16))
    return hi, mid, lo


CMP_ROWS = 512


def _nsa_prepare(qall_ref, kc_ref, vc_ref, ks_ref, vs_ref, kw_ref, vw_ref, kx_ref, qg_ref, ksg_ref,
                 kwg_ref, bd_ref, ot_ref, ksaug_scr, kwaug_scr, vst_scr, vwt_scr, notsel_scr,
                 ocmp_scr, *, nc):
    bd = bd_ref[...]
    npair = NSA_HEADS // 2
    seq = ks_ref.shape[1]
    ksaug_scr[:, 0:LANES] = _headnorm_pair(ks_ref[0].astype(F32), bd, ksg_ref[...]).astype(BF16)
    kwaug_scr[:, 0:LANES] = _headnorm_pair(kw_ref[0].astype(F32), bd, kwg_ref[...]).astype(BF16)
    ksaug_scr[:, LANES:KA] = kx_ref[...]
    kwaug_scr[:, LANES:KA] = kx_ref[...]
    for v_ref, vt_scr in ((vs_ref, vst_scr), (vw_ref, vwt_scr)):
        vt = v_ref[0].astype(F32).T
        for e in range(2):
            _store_values_t(vt_scr.at[e], vt[e * HEAD_DIM:(e + 1) * HEAD_DIM], seq)

    lane = lax.broadcasted_iota(jnp.int32, (CMP_ROWS, LANES), 1)
    low = lane < HEAD_DIM
    cend = lane * CMP_STRIDE + (CMP_LEN - 1)
    cend_f = (lax.broadcasted_iota(jnp.int32, (1, LANES), 1) * CMP_STRIDE + (CMP_LEN - 1)).astype(F32)
    kc = kc_ref[0]
    vc = vc_ref[0]

    def chunk(c, carry):
        t0 = pl.multiple_of(c * CMP_ROWS, CMP_ROWS)
        tcol = t0 + lax.broadcasted_iota(jnp.int32, (CMP_ROWS, LANES), 0)
        cmask = (tcol >= cend) & (lane < nc)
        psum = [jnp.zeros((CMP_ROWS, LANES), F32), jnp.zeros((CMP_ROWS, LANES), F32)]
        for r in range(npair):
            q2 = _headnorm_pair(qall_ref[0, pl.ds(t0, CMP_ROWS), r * LANES:(r + 1) * LANES].astype(F32),
                                bd, qg_ref[...])
            halves = []
            for e in range(2):
                slope = 2.0 ** -(r + 1 + 4 * e)
                qm = jnp.where(low if e == 0 else ~low, q2, 0.0).astype(BF16)
                s = jnp.where(cmask, _qk(qm, kc) + (slope * L2E) * cend_f, NEG)
                m = jnp.max(s, axis=-1, keepdims=True)
                pe = jnp.where(cmask, jnp.exp2(s - m), 0.0)
                l = jnp.sum(pe, axis=-1, keepdims=True)
                pn = pe / jnp.where(l > 0.0, l, 1.0)
                psum[e] = psum[e] + pn
                halves.append(jnp.dot(pn.astype(BF16), vc, preferred_element_type=F32))
            ocmp_scr[r, pl.ds(t0, CMP_ROWS), :] = jnp.where(low, halves[0], halves[1])
        for g in range(NSA_GROUPS):
            hi, mid, lo = _split3(psum[g])
            imp_t = (_qk(ot_ref[...], hi.astype(BF16)) + _qk(ot_ref[...], mid.astype(BF16))
                     + _qk(ot_ref[...], lo.astype(BF16)))
            notsel_scr[g, :, pl.ds(t0, CMP_ROWS)] = (_nsa_select(imp_t, t0) - 1.0) * -NEG
        return carry

    lax.fori_loop(0, seq // CMP_ROWS, chunk, 0)


def _nsa_kernel(qall_ref, q_ref, sm_ref, kc_ref, vc_ref, ks_ref, vs_ref, kw_ref, vw_ref, kx_ref,
                slr_ref, qg_ref, ksg_ref, kwg_ref, bd_ref, ot_ref, egt_ref, eg_ref, o_ref,
                ksaug_scr, kwaug_scr, vst_scr, vwt_scr, qts_scr, qtw_scr, notsel_scr, ocmp_scr,
                *, nc, bounded):
    r = pl.program_id(1)
    bd = bd_ref[...]
    seq = ks_ref.shape[1]
    nt = seq // BLK

    @pl.when(r == 0)
    def _():
        _nsa_prepare(qall_ref, kc_ref, vc_ref, ks_ref, vs_ref, kw_ref, vw_ref, kx_ref, qg_ref,
                     ksg_ref, kwg_ref, bd_ref, ot_ref, ksaug_scr, kwaug_scr, vst_scr, vwt_scr,
                     notsel_scr, ocmp_scr, nc=nc)

    qk_bound = _logit_bound(qg_ref, [ksg_ref, kwg_ref])
    qt = _headnorm_pair(q_ref[0].astype(F32), bd, qg_ref[...]).T
    for e in range(2):
        sl = slr_ref[0, e]
        bound = qk_bound + _alibi_at_query(sl, seq) if bounded else None
        qts_scr[e] = _query_feats(qt, e, sl, notsel_scr[e], bound)
        qtw_scr[e] = _query_feats(qt, e, sl, None, bound)

    key = lax.broadcasted_iota(jnp.int32, (BLK, BLK), 0)
    qry = lax.broadcasted_iota(jnp.int32, (BLK, BLK), 1)
    diag = key <= qry
    above = key > qry
    vs_rows = [lambda ks, e=e: vst_scr[e, :, ks] for e in range(2)]
    slc = _causal_blocks(nt, ksaug_scr, vs_rows, qts_scr)

    def win_block(j, qh, e, mask):
        ks = slice(j * BLK, (j + 1) * BLK)
        qs = slice(qh * BLK, (qh + 1) * BLK)
        return (lambda: kwaug_scr[ks, :], lambda: vwt_scr[e, :, ks], lambda: qtw_scr[e, :, qs],
                mask, 2 * nt + e * nt + qh)

    blocks = []
    pos = 0
    for j in range(nt):
        n_j = 2 * (nt - j)
        blocks.extend(slc[pos:pos + n_j])
        pos += n_j
        blocks.extend(win_block(j, j, e, diag) for e in range(2))
        if j > 0:
            blocks.extend(win_block(j - 1, j, e, above) for e in range(2))
    acc = _flash(blocks, 4 * nt, HEAD_DIM, bounded)
    normed = lambda a: a[0:HEAD_DIM] / a[HEAD_DIM:HEAD_DIM + 1]

    gate = jax.nn.sigmoid(sm_ref[0])
    gate_t = gate.T
    g_cmp = _dot01(gate, eg_ref[0])
    g_slc_t = _dot01_left(egt_ref[1], gate_t)
    g_win_t = _dot01_left(egt_ref[2], gate_t)
    for qh in range(nt):
        cols = slice(qh * BLK, (qh + 1) * BLK)
        o_s = jnp.concatenate([normed(acc[e * nt + qh]) for e in range(2)], axis=0)
        o_w = jnp.concatenate([normed(acc[2 * nt + e * nt + qh]) for e in range(2)], axis=0)
        mix = (g_slc_t[:, cols] * o_s + g_win_t[:, cols] * o_w).T
        out = mix + g_cmp[cols, :] * ocmp_scr[r, cols, :]
        o_ref[0, cols, :] = out.astype(o_ref.dtype)


def _nsa(proj3, small3, kc, vc, kx, slr, qg, ksg, kwg, bd, ot, egt, eg, *, bounded):
    b, s, _ = proj3.shape
    npair = NSA_HEADS // 2
    ns = s // SLC_BLOCK
    nc = (s - CMP_LEN) // CMP_STRIDE + 1
    const = lambda a: pl.BlockSpec(a.shape, lambda bi, r: (0,) * a.ndim)
    kvspec = lambda c: pl.BlockSpec((1, s, LANES), lambda bi, r: (bi, 0, c))
    return pl.pallas_call(
        functools.partial(_nsa_kernel, nc=nc, bounded=bounded),
        grid=(b, npair),
        in_specs=[
            pl.BlockSpec((1, s, npair * LANES), lambda bi, r: (bi, 0, COL_NSA_Q // npair)),
            pl.BlockSpec((1, s, LANES), lambda bi, r: (bi, 0, COL_NSA_Q + r)),
            pl.BlockSpec((1, s, LANES), lambda bi, r: (bi, 0, 0)),
            pl.BlockSpec((1,) + kc.shape[1:], lambda bi, r: (bi, 0, 0)),
            pl.BlockSpec((1,) + vc.shape[1:], lambda bi, r: (bi, 0, 0)),
            kvspec(COL_K_SLC), kvspec(COL_V_SLC), kvspec(COL_K_WIN), kvspec(COL_V_WIN),
            const(kx),
            pl.BlockSpec((1, 2, 8, s), lambda bi, r: (r, 0, 0, 0)),
            const(qg), const(ksg), const(kwg), const(bd), const(ot),
            pl.BlockSpec((3, LANES, LANES), lambda bi, r: (0, r, 0)),
            pl.BlockSpec((3, LANES, LANES), lambda bi, r: (0, 0, r)),
        ],
        out_specs=pl.BlockSpec((1, s, LANES), lambda bi, r: (bi, 0, r)),
        out_shape=jax.ShapeDtypeStruct((b, s, npair * LANES), BF16),
        scratch_shapes=[pltpu.VMEM((s, KA), BF16), pltpu.VMEM((s, KA), BF16),
                        pltpu.VMEM((2, HEAD_DIM + VAUG, s), BF16),
                        pltpu.VMEM((2, HEAD_DIM + VAUG, s), BF16),
                        pltpu.VMEM((2, KA, s), BF16), pltpu.VMEM((2, KA, s), BF16),
                        pltpu.VMEM((NSA_GROUPS, ns, s), F32), pltpu.VMEM((npair, s, LANES), F32)],
        compiler_params=_cparams(("parallel", "arbitrary")),
        name="nsa",
    )(proj3, proj3, small3, kc, vc, proj3, proj3, proj3, proj3, kx, slr, qg, ksg, kwg, bd, ot,
      egt, eg)


def _merge_kernel(oa_ref, ob_ref, oc_ref, g0_ref, g1_ref, g2_ref, h_ref, wbr_ref, wo_ref, o_ref):
    merged = None
    for c, (o_r, g_r) in enumerate(((oa_ref, g0_ref), (ob_ref, g1_ref), (oc_ref, g2_ref))):
        y = jnp.dot(o_r[...], wbr_ref[c], preferred_element_type=F32)
        term = jax.nn.sigmoid(g_r[...].astype(F32)) * y
        merged = term if merged is None else merged + term
    o_ref[...] = h_ref[...] + jnp.dot(merged.astype(BF16), wo_ref[...], preferred_element_type=F32)


def _merge(oa, ob, oc, proj2, h2, wbr, wo, *, tm=512):
    t, d = h2.shape
    bw = oa.shape[1]
    row = lambda w: pl.BlockSpec((tm, w), lambda i: (i, 0))
    gate = lambda c: pl.BlockSpec((tm, d), lambda i: (i, COL_MG * LANES // d + c))
    return pl.pallas_call(
        _merge_kernel,
        grid=(t // tm,),
        in_specs=[row(bw), row(bw), row(bw), gate(0), gate(1), gate(2), row(d),
                  pl.BlockSpec(wbr.shape, lambda i: (0, 0, 0)),
                  pl.BlockSpec(wo.shape, lambda i: (0, 0))],
        out_specs=row(d),
        out_shape=jax.ShapeDtypeStruct((t, d), F32),
        compiler_params=_cparams(("parallel",)),
        name="merge",
    )(oa, ob, oc, proj2, proj2, proj2, h2, wbr, wo)


HALO = 16


def _ffn_kernel(x_ref, xh_ref, g_ref, wu_ref, wg_ref, cw_ref, cb_ref, wd_ref, o_ref,
                a_scr, u_scr, acc_scr, *, tm, tiles_per_seq):
    i = pl.program_id(0)
    j = pl.program_id(1)
    nj = pl.num_programs(1)

    @pl.when(j == 0)
    def _():
        def norm(x):
            return (x * lax.rsqrt(jnp.mean(x * x, axis=-1, keepdims=True) + EPS)
                    * g_ref[...]).astype(BF16)
        a_scr[0:HALO, :] = norm(xh_ref[...])
        a_scr[HALO:, :] = norm(x_ref[...])
        acc_scr[...] = jnp.zeros_like(acc_scr)

    u = jnp.dot(a_scr[...], wu_ref[...], preferred_element_type=F32)
    rows = lax.broadcasted_iota(jnp.int32, u.shape, 0)
    seq_start = (i % tiles_per_seq) == 0
    u_scr[...] = jnp.where((rows < HALO) & seq_start, 0.0, u)
    cw = cw_ref[...]
    uc = cb_ref[...]
    for t in range(CONV_W):
        uc = uc + cw[t:t + 1, :] * u_scr[pl.ds(HALO - (CONV_W - 1) + t, tm), :]
    gt = jnp.dot(a_scr[HALO:, :], wg_ref[...], preferred_element_type=F32)
    act = (jax.nn.gelu(uc) * gt).astype(BF16)
    acc_scr[...] += jnp.dot(act, wd_ref[...], preferred_element_type=F32)

    @pl.when(j == nj - 1)
    def _():
        o_ref[...] = x_ref[...] + acc_scr[...]


def _ffn(h2, g, wup, cw, cb, wd, seq, *, tm=512, tf=1408):
    t, d = h2.shape
    dff = wd.shape[0]
    nf = dff // tf
    return pl.pallas_call(
        functools.partial(_ffn_kernel, tm=tm, tiles_per_seq=seq // tm),
        grid=(t // tm, nf),
        in_specs=[
            pl.BlockSpec((tm, d), lambda i, j: (i, 0)),
            pl.BlockSpec((HALO, d), lambda i, j: (jnp.maximum(i * (tm // HALO) - 1, 0), 0)),
            pl.BlockSpec((1, d), lambda i, j: (0, 0)),
            pl.BlockSpec((d, tf), lambda i, j: (0, j)),
            pl.BlockSpec((d, tf), lambda i, j: (0, nf + j)),
            pl.BlockSpec((CONV_W, tf), lambda i, j: (0, j)),
            pl.BlockSpec((1, tf), lambda i, j: (0, j)),
            pl.BlockSpec((tf, d), lambda i, j: (j, 0)),
        ],
        out_specs=pl.BlockSpec((tm, d), lambda i, j: (i, 0)),
        out_shape=jax.ShapeDtypeStruct((t, d), F32),
        scratch_shapes=[pltpu.VMEM((HALO + tm, d), BF16), pltpu.VMEM((HALO + tm, tf), F32),
                        pltpu.VMEM((tm, d), F32)],
        compiler_params=_cparams(("parallel", "arbitrary")),
        name="ffn",
    )(h2, h2, g, wup, wup, cw, cb, wd)


def _nsa_head_order():
    hpg = NSA_HEADS // NSA_GROUPS
    return [h for r in range(hpg) for h in (r, hpg + r)]


def _constants(seq):
    ns = seq // SLC_BLOCK
    nc = (seq - CMP_LEN) // CMP_STRIDE + 1
    assert MASK0 + ns <= LANES and MASK0 >= 8 and N_BIAS <= 8
    bd = np.kron(np.eye(2), np.ones((HEAD_DIM, HEAD_DIM))).astype(np.float32)
    c_start = np.arange(LANES) * CMP_STRIDE
    s_start = np.arange(ns) * SLC_BLOCK
    ot = ((c_start[None, :] < s_start[:, None] + SLC_BLOCK)
          & (c_start[None, :] + CMP_LEN > s_start[:, None])
          & (np.arange(LANES)[None, :] < nc)).astype(np.float32)
    pos = np.arange(seq)
    kx = np.zeros((seq, LANES), np.float32)
    for k in range(N_BIAS):
        kx[:, k] = (pos // 8) * 8 if k % 2 == 0 else pos % 8
    kx[pos, MASK0 + pos // SLC_BLOCK] = 1.0
    kx[:, BOUND0:BOUND0 + 3] = 1.0
    order = _nsa_head_order()
    eg = np.zeros((3, LANES, NSA_HEADS * HEAD_DIM), np.float32)
    for c in range(3):
        for slot, h in enumerate(order):
            eg[c, SMALL_GATE0 + c * NSA_HEADS + h, slot * HEAD_DIM:(slot + 1) * HEAD_DIM] = 1.0
    u = (np.arange(BLK)[:, None] <= np.arange(BLK)[None, :]).astype(np.float32)
    dsl = np.zeros((DIFF_HEADS, 8, seq), np.float32)
    for h in range(DIFF_HEADS):
        rows = _slope_rows(2.0 ** (-8.0 * (h + 1) / DIFF_HEADS))
        dsl[h, :N_BIAS, :] = np.asarray(rows, np.float32)[:, None]
    nsl = np.zeros((NSA_HEADS // 2, 2, 8, seq), np.float32)
    for slot, h in enumerate(order):
        rows = _slope_rows(2.0 ** (-8.0 * (h + 1) / NSA_HEADS))
        nsl[slot // 2, slot % 2, :N_BIAS, :] = np.asarray(rows, np.float32)[:, None]
    as_bf = lambda a: jnp.asarray(a, BF16)
    return dict(bd=as_bf(bd), ot=as_bf(ot), kx=as_bf(kx), eg=as_bf(eg),
                egt=as_bf(np.transpose(eg, (0, 2, 1))), u=as_bf(u), dsl=jnp.asarray(dsl, F32),
                nsl=jnp.asarray(nsl, F32))


def _pack_w_in(w):
    d = w.shape[0]
    hd = HEAD_DIM
    sizes = [512, 768, 24, 512, 512, 512, 512, 512, 512, 8, 3 * d]
    offs = np.concatenate([[0], np.cumsum(sizes)])
    nq, nkv, ngate, dq, dk, dv, fq, fk, fv, ff, mg = [w[:, offs[k]:offs[k + 1]] for k in range(11)]
    nq = jnp.concatenate([nq[:, h * hd:(h + 1) * hd] for h in _nsa_head_order()], axis=1)
    kv = lambda c, k: nkv[:, (c * 2 + k) * LANES:(c * 2 + k + 1) * LANES]
    cols = [mg, nq, dq, dk, dv, fq, fk, fv,
            kv(1, 0), kv(1, 1), kv(2, 0), kv(2, 1), kv(0, 0), kv(0, 1)]
    main = jnp.concatenate(cols, axis=1)
    main = jnp.pad(main, ((0, 0), (0, PROJ_UNITS * LANES - main.shape[1])))
    small = jnp.pad(jnp.concatenate([ngate, ff], axis=1), ((0, 0), (0, LANES - 32)))
    return main.astype(BF16), small.astype(BF16)


def kernel(x, attn_norm_g, w_in, nsa_q_g, nsa_k_g, cmp_pe, cmp_w1, cmp_w2, diff_q_g, diff_k_g,
           diff_lam, diff_subln_g, fox_q_g, fox_k_g, fox_b, w_br, w_o, ffn_norm_g, w_up, conv_w,
           conv_b, w_down):
    b, s, d = x.shape
    depth = w_in.shape[0]
    hd = HEAD_DIM
    qscale = hd ** -0.5 * L2E
    cst = _constants(s)
    rows16 = s // CMP_STRIDE
    assert rows16 == LANES and s % CMP_ROWS == 0, "NSA kernel keeps all compressed blocks in one 128-lane tile"
    order = _nsa_head_order()
    tile2 = lambda g: jnp.tile(g, 2).reshape(1, LANES).astype(F32)

    h = x.reshape(b * s, d)
    for l in range(depth):
        wmain, wsmall = _pack_w_in(w_in[l])
        proj, small = _proj(h, attn_norm_g[l].reshape(1, d), wmain, wsmall)
        proj3 = proj.reshape(b, s, PROJ_UNITS * LANES)
        small3 = small.reshape(b, s, LANES)

        raw = proj3[:, :, COL_K_CMP * LANES:(COL_V_CMP + 1) * LANES]
        raw = raw.reshape(b, rows16, CMP_STRIDE, 2 * NSA_GROUPS, hd)
        raw = jnp.transpose(raw, (0, 3, 1, 2, 4)).reshape(b, 2 * NSA_GROUPS, rows16, CMP_STRIDE * hd)
        half = CMP_STRIDE * hd
        w1cat = jnp.concatenate([cmp_w1[l][:, :half], cmp_w1[l][:, half:]], axis=2).astype(BF16)
        pe8 = jnp.broadcast_to(cmp_pe[l].reshape(2, 1, CMP_LEN * hd), (2, 8, CMP_LEN * hd)).astype(BF16)
        kc, vc = _compress(raw, w1cat, pe8, cmp_w1[l].astype(BF16), cmp_w2[l].astype(BF16),
                           nsa_k_g[l, 0].reshape(1, hd))

        fb_row = jnp.zeros((1, LANES), F32).at[0, SMALL_FF0:SMALL_FF0 + FOX_HEADS].set(fox_b[l])
        negf = _fcum(small3, fb_row, cst["u"])

        def mixer(call, qg, kgs, *args):
            span = 2.0 * HEAD_DIM * BOUND_SLACK * jnp.max(jnp.abs(qg)) * max_abs(kgs)
            return lax.cond(span <= BOUND_LIMIT, functools.partial(call, bounded=True),
                            functools.partial(call, bounded=False), *args)

        max_abs = lambda gs: functools.reduce(jnp.maximum, [jnp.max(jnp.abs(g)) for g in gs])
        nsa_qg, diff_qg, fox_qg = nsa_q_g[l] * qscale, diff_q_g[l] * qscale, fox_q_g[l] * qscale
        o_a = mixer(_nsa, nsa_qg, [nsa_k_g[l, 1], nsa_k_g[l, 2]],
                    proj3, small3, kc, vc, cst["kx"], cst["nsl"], tile2(nsa_qg),
                    tile2(nsa_k_g[l, 1]), tile2(nsa_k_g[l, 2]), cst["bd"], cst["ot"], cst["egt"],
                    cst["eg"])
        lam_init = 0.8 - 0.6 * math.exp(-0.3 * l)
        o_b = mixer(functools.partial(_diff, lam_init=lam_init), diff_qg, [diff_k_g[l]],
                    proj3, cst["kx"], cst["dsl"], diff_lam[l], tile2(diff_qg), tile2(diff_k_g[l]),
                    diff_subln_g[l].reshape(1, LANES), cst["bd"])
        o_c = mixer(_fox, fox_qg, [fox_k_g[l]],
                    proj3, negf, tile2(fox_qg), tile2(fox_k_g[l]), cst["bd"])

        wbr = w_br[l]
        wbr_a = jnp.concatenate([wbr[0, hh * hd:(hh + 1) * hd] for hh in order], axis=0)
        wbr_p = jnp.stack([wbr_a, wbr[1], wbr[2]]).astype(BF16)
        bw = NSA_HEADS * hd
        h = _merge(o_a.reshape(b * s, bw), o_b.reshape(b * s, bw), o_c.reshape(b * s, bw),
                   proj, h, wbr_p, w_o[l].astype(BF16))
        h = _ffn(h, ffn_norm_g[l].reshape(1, d), w_up[l].astype(BF16), conv_w[l],
                 conv_b[l].reshape(1, -1), w_down[l].astype(BF16), s)
    return h.reshape(b, s, d)
```

```python
import functools
import math

import numpy as np
import jax
import jax.numpy as jnp
from jax import lax
from jax.experimental import pallas as pl
from jax.experimental.pallas import tpu as pltpu

F32 = jnp.float32
BF16 = jnp.bfloat16

HEAD_DIM = 64
NSA_HEADS = 8
NSA_GROUPS = 2
CMP_LEN = 32
CMP_STRIDE = 16
SLC_BLOCK = 64
SLC_TOPN = 8
WINDOW = 256
FORCE_BONUS = 1.0e4
DIFF_HEADS = 4
FOX_HEADS = 8
CONV_W = 3
EPS = 1e-6
NEG = -1.0e30
L2E = 1.4426950408889634

LANES = 128
BLK = 256
KA = 2 * LANES
N_BIAS = 6
MASK0 = 8
BOUND0 = 40
VAUG = 16
BOUND_SLACK = 1.02
BOUND_LIMIT = 100.0
VMEM_LIMIT = 56 * 1024 * 1024

COL_MG = 0
COL_NSA_Q = 24
COL_DIFF_Q, COL_DIFF_K, COL_DIFF_V = 28, 32, 36
COL_FOX_Q, COL_FOX_K, COL_FOX_V = 40, 44, 48
COL_K_SLC, COL_V_SLC, COL_K_WIN, COL_V_WIN, COL_K_CMP, COL_V_CMP = 52, 53, 54, 55, 56, 57
PROJ_UNITS = 60
SMALL_GATE0 = 0
SMALL_FF0 = 24


def _cparams(sem):
    return pltpu.CompilerParams(dimension_semantics=sem, vmem_limit_bytes=VMEM_LIMIT)


def _split3(x):
    hi = x.astype(BF16).astype(F32)
    r1 = x - hi
    mid = r1.astype(BF16).astype(F32)
    lo = (r1 - mid).astype(BF16).astype(F32)
    return hi, mid, lo


def _dot01(x, m):
    hi = x.astype(BF16)
    lo = (x - hi.astype(F32)).astype(BF16)
    return (jnp.dot(hi, m, preferred_element_type=F32)
            + jnp.dot(lo, m, preferred_element_type=F32))


def _dot01_left(m, x):
    hi = x.astype(BF16)
    lo = (x - hi.astype(F32)).astype(BF16)
    return (jnp.dot(m, hi, preferred_element_type=F32)
            + jnp.dot(m, lo, preferred_element_type=F32))


def _dot01_3(x, m):
    hi, mid, lo = _split3(x)
    return (jnp.dot(hi.astype(BF16), m, preferred_element_type=F32)
            + jnp.dot(mid.astype(BF16), m, preferred_element_type=F32)
            + jnp.dot(lo.astype(BF16), m, preferred_element_type=F32))


def _qk(q, k):
    return lax.dot_general(q, k, (((1,), (1,)), ((), ())), preferred_element_type=F32)


def _headnorm_pair(x, bd, gain):
    ss = jnp.dot((x * x).astype(BF16), bd, preferred_element_type=F32)
    return x * lax.rsqrt(ss * (1.0 / HEAD_DIM) + EPS) * gain


def _rows8(vals, width):
    row = lax.broadcasted_iota(jnp.int32, (8, width), 0)
    out = jnp.zeros((8, width), F32)
    for k, v in enumerate(vals):
        out = jnp.where(row == k, v, out)
    return out


def _query_feats(qt, half, bias8, mask_rows, bound_row):
    tq = qt.shape[1]
    row = lax.broadcasted_iota(jnp.int32, (LANES, tq), 0)
    keep = (row < HEAD_DIM) if half == 0 else (row >= HEAD_DIM)
    nmask = BOUND0 - MASK0
    parts = [jnp.where(keep, qt, 0.0), bias8,
             mask_rows if mask_rows is not None else jnp.zeros((nmask, tq), F32),
             _rows8(_split3(-bound_row), tq) if bound_row is not None else jnp.zeros((8, tq), F32),
             jnp.zeros((LANES - BOUND0 - 8, tq), F32)]
    return jnp.concatenate(parts, axis=0).astype(BF16)


def _logit_bound(qg_ref, kg_refs):
    kmax = None
    for kg_ref in kg_refs:
        k = jnp.max(jnp.abs(kg_ref[...]), axis=-1, keepdims=True)
        kmax = k if kmax is None else jnp.maximum(kmax, k)
    return jnp.max(jnp.abs(qg_ref[...]), axis=-1, keepdims=True) * kmax * (HEAD_DIM * BOUND_SLACK)


def _store_values_t(vt_view, rows, seq):
    dv = rows.shape[0]
    vt_view[0:dv, :] = rows.astype(BF16)
    vt_view[dv:dv + 8, :] = _rows8([1.0], seq).astype(BF16)
    vt_view[dv + 8:dv + VAUG, :] = jnp.zeros((VAUG - 8, seq), BF16)


QK_AHEAD = 2


def _flash(blocks, n_chain, dv, bounded):
    acc = [jnp.zeros((dv + VAUG, BLK), F32) for _ in range(n_chain)]
    mx = [jnp.full((1, BLK), NEG, F32) for _ in range(n_chain)]
    scores = {}

    def issue(k):
        if k < len(blocks):
            scores[k] = jnp.dot(blocks[k][0](), blocks[k][2](), preferred_element_type=F32)

    for k in range(2 * QK_AHEAD):
        issue(k)
    for k0 in range(0, len(blocks), 2):
        issue(k0 + 2 * QK_AHEAD)
        issue(k0 + 2 * QK_AHEAD + 1)
        pending = []
        for k in range(k0, min(k0 + 2, len(blocks))):
            _, vt, _, mask, chain = blocks[k]
            s = scores.pop(k)
            if mask is not None:
                s = jnp.where(mask, s, NEG)
            if bounded:
                pending.append((vt, chain, None, jnp.exp2(s).astype(BF16)))
            else:
                m_new = jnp.maximum(mx[chain], jnp.max(s, axis=0, keepdims=True))
                alpha = jnp.exp2(mx[chain] - m_new)
                mx[chain] = m_new
                pending.append((vt, chain, alpha, jnp.exp2(s - m_new).astype(BF16)))
        assert len({c for _, c, _, _ in pending}) == len(pending), "a pair must not share a chain"
        for vt, chain, alpha, p in pending:
            prev = acc[chain] if alpha is None else alpha * acc[chain]
            acc[chain] = prev + jnp.dot(vt(), p, preferred_element_type=F32)
    return acc


def _causal_blocks(nt, kaug_scr, vt_rows, qt_scr):
    key = lax.broadcasted_iota(jnp.int32, (BLK, BLK), 0)
    qry = lax.broadcasted_iota(jnp.int32, (BLK, BLK), 1)
    diag = key <= qry
    blocks = []
    for j in range(nt):
        ks = slice(j * BLK, (j + 1) * BLK)
        for qh in range(j, nt):
            qs = slice(qh * BLK, (qh + 1) * BLK)
            for e in range(2):
                blocks.append((lambda ks=ks: kaug_scr[ks, :],
                               lambda e=e, ks=ks: vt_rows[e](ks),
                               lambda e=e, qs=qs: qt_scr[e, :, qs],
                               diag if qh == j else None, e * nt + qh))
    return blocks


def _proj_kernel(x_ref, g_ref, w_ref, ws_ref, o_ref, os_ref, a_scr):
    j = pl.program_id(1)

    @pl.when(j == 0)
    def _():
        x = x_ref[...]
        inv = lax.rsqrt(jnp.mean(x * x, axis=-1, keepdims=True) + EPS)
        a = (x * inv * g_ref[...]).astype(BF16)
        a_scr[...] = a
        os_ref[...] = jnp.dot(a, ws_ref[...], preferred_element_type=F32)

    o_ref[...] = jnp.dot(a_scr[...], w_ref[...], preferred_element_type=F32).astype(o_ref.dtype)


def _proj(x2d, g, w, ws, *, tm=1024, tn=2560):
    t, d = x2d.shape
    n = w.shape[1]
    return pl.pallas_call(
        _proj_kernel,
        grid=(t // tm, n // tn),
        in_specs=[
            pl.BlockSpec((tm, d), lambda i, j: (i, 0)),
            pl.BlockSpec((1, d), lambda i, j: (0, 0)),
            pl.BlockSpec((d, tn), lambda i, j: (0, j)),
            pl.BlockSpec((d, LANES), lambda i, j: (0, 0)),
        ],
        out_specs=[
            pl.BlockSpec((tm, tn), lambda i, j: (i, j)),
            pl.BlockSpec((tm, LANES), lambda i, j: (i, 0)),
        ],
        out_shape=[jax.ShapeDtypeStruct((t, n), BF16), jax.ShapeDtypeStruct((t, LANES), F32)],
        scratch_shapes=[pltpu.VMEM((tm, d), BF16)],
        compiler_params=_cparams(("parallel", "arbitrary")),
        name="proj",
    )(x2d, g, w, ws)


def _compress_kernel(r_ref, w1_ref, pe_ref, w1f_ref, w2_ref, kg_ref, kc_ref, vc_ref):
    outs = []
    for kv in range(2):
        c1 = jnp.dot(pe_ref[kv], w1f_ref[kv], preferred_element_type=F32)[0:1]
        per_group = []
        for g in range(NSA_GROUPS):
            ab = jnp.dot(r_ref[0, kv * NSA_GROUPS + g], w1_ref[kv], preferred_element_type=F32)
            pre = ab[:, :LANES] + pltpu.roll(ab[:, LANES:], LANES - 1, 0) + c1
            hid = jax.nn.gelu(pre).astype(BF16)
            o = jnp.dot(hid, w2_ref[kv], preferred_element_type=F32)
            if kv == 0:
                o = o * lax.rsqrt(jnp.mean(o * o, axis=-1, keepdims=True) + EPS) * kg_ref[...]
            per_group.append(o)
        outs.append(jnp.concatenate(per_group, axis=-1))
    kc_ref[0] = outs[0].astype(BF16)
    vc_ref[0] = outs[1].astype(BF16)


def _compress(r, w1cat, pe8, w1f, w2, kg):
    b = r.shape[0]
    nrow = r.shape[2]
    full = lambda a: pl.BlockSpec(a.shape, lambda i: (0,) * a.ndim)
    return pl.pallas_call(
        _compress_kernel,
        grid=(b,),
        in_specs=[pl.BlockSpec((1,) + r.shape[1:], lambda i: (i, 0, 0, 0)),
                  full(w1cat), full(pe8), full(w1f), full(w2), full(kg)],
        out_specs=[pl.BlockSpec((1, nrow, LANES), lambda i: (i, 0, 0)),
                   pl.BlockSpec((1, nrow, LANES), lambda i: (i, 0, 0))],
        out_shape=[jax.ShapeDtypeStruct((b, nrow, LANES), BF16),
                   jax.ShapeDtypeStruct((b, nrow, LANES), BF16)],
        compiler_params=_cparams(("parallel",)),
        name="compress",
    )(r, w1cat, pe8, w1f, w2, kg)


def _fcum_kernel(s_ref, fb_ref, u_ref, o_ref, *, chunk):
    z = s_ref[0] + fb_ref[...]
    lf = jax.nn.log_sigmoid(z)
    lft = lf.T[SMALL_FF0:SMALL_FF0 + FOX_HEADS]
    seq = lft.shape[1]
    carry = jnp.zeros((FOX_HEADS, 1), F32)
    for c in range(seq // chunk):
        fc = _dot01_3(lft[:, c * chunk:(c + 1) * chunk], u_ref[...]) + carry
        o_ref[0, :, c * chunk:(c + 1) * chunk] = -fc
        carry = fc[:, chunk - 1:chunk]


def _fcum(small3, fb_row, u):
    b, s, _ = small3.shape
    chunk = u.shape[0]
    return pl.pallas_call(
        functools.partial(_fcum_kernel, chunk=chunk),
        grid=(b,),
        in_specs=[pl.BlockSpec((1, s, LANES), lambda i: (i, 0, 0)),
                  pl.BlockSpec((1, LANES), lambda i: (0, 0)),
                  pl.BlockSpec(u.shape, lambda i: (0, 0))],
        out_specs=pl.BlockSpec((1, FOX_HEADS, s), lambda i: (i, 0, 0)),
        out_shape=jax.ShapeDtypeStruct((b, FOX_HEADS, s), F32),
        compiler_params=_cparams(("parallel",)),
        name="fcum",
    )(small3, fb_row, u)


def _fox_kernel(q_ref, k_ref, v_ref, nf_ref, qg_ref, kg_ref, bd_ref, o_ref,
                kaug_scr, vt_scr, qt_scr, *, bounded):
    p = pl.program_id(1)
    bd = bd_ref[...]
    seq = k_ref.shape[1]
    nt = seq // BLK

    kaug_scr[:, 0:LANES] = _headnorm_pair(k_ref[0].astype(F32), bd, kg_ref[...]).astype(BF16)
    cb = [nf_ref[0, pl.ds(2 * p + e, 1), :] * L2E for e in range(2)]
    rows = []
    for e in range(2):
        rows.extend(_split3(cb[e]))
    feats = jnp.concatenate([_rows8(rows, seq), jnp.zeros((BOUND0 - 8, seq), F32),
                             _rows8([1.0] * 3, seq), jnp.zeros((LANES - BOUND0 - 8, seq), F32)],
                            axis=0)
    kaug_scr[:, LANES:KA] = feats.T.astype(BF16)
    vt = v_ref[0].astype(F32).T
    for e in range(2):
        _store_values_t(vt_scr.at[e], vt[e * HEAD_DIM:(e + 1) * HEAD_DIM], seq)

    qk_bound = _logit_bound(qg_ref, [kg_ref])
    qt = _headnorm_pair(q_ref[0].astype(F32), bd, qg_ref[...]).T
    for e in range(2):
        qt_scr[e] = _query_feats(qt, e, _rows8([0.0] * (3 * e) + [1.0] * 3, seq), None,
                                 qk_bound + cb[e] if bounded else None)

    vt_rows = [lambda ks, e=e: vt_scr[e, :, ks] for e in range(2)]
    acc = _flash(_causal_blocks(nt, kaug_scr, vt_rows, qt_scr), 2 * nt, HEAD_DIM, bounded)
    for qh in range(nt):
        ot = jnp.concatenate([acc[e * nt + qh][0:HEAD_DIM] / acc[e * nt + qh][HEAD_DIM:HEAD_DIM + 1]
                              for e in range(2)], axis=0)
        o_ref[0, qh * BLK:(qh + 1) * BLK, :] = ot.T.astype(o_ref.dtype)


def _fox(proj3, negf, qg, kg, bd, *, bounded):
    b, s, _ = proj3.shape
    npair = FOX_HEADS // 2
    const = lambda a: pl.BlockSpec(a.shape, lambda bi, p: (0,) * a.ndim)
    col = lambda c: pl.BlockSpec((1, s, LANES), lambda bi, p: (bi, 0, c + p))
    return pl.pallas_call(
        functools.partial(_fox_kernel, bounded=bounded),
        grid=(b, npair),
        in_specs=[col(COL_FOX_Q), col(COL_FOX_K), col(COL_FOX_V),
                  pl.BlockSpec((1, FOX_HEADS, s), lambda bi, p: (bi, 0, 0)),
                  const(qg), const(kg), const(bd)],
        out_specs=pl.BlockSpec((1, s, LANES), lambda bi, p: (bi, 0, p)),
        out_shape=jax.ShapeDtypeStruct((b, s, npair * LANES), BF16),
        scratch_shapes=[pltpu.VMEM((s, KA), BF16), pltpu.VMEM((2, HEAD_DIM + VAUG, s), BF16),
                        pltpu.VMEM((2, KA, s), BF16)],
        compiler_params=_cparams(("parallel", "parallel")),
        name="fox",
    )(proj3, proj3, proj3, negf, qg, kg, bd)


def _alibi_at_query(sl_rows, seq):
    pos = lax.broadcasted_iota(jnp.int32, (1, seq), 1).astype(F32)
    return (sl_rows[0:1] + sl_rows[2:3] + sl_rows[4:5]) * pos


def _diff_kernel(q_ref, k_ref, v_ref, kx_ref, sl_ref, lam_ref, qg_ref, kg_ref, sg_ref, bd_ref,
                 o_ref, kaug_scr, vt_scr, qt_scr, *, lam_init, bounded):
    bd = bd_ref[...]
    seq = k_ref.shape[1]
    nt = seq // BLK

    kaug_scr[:, 0:LANES] = _headnorm_pair(k_ref[0].astype(F32), bd, kg_ref[...]).astype(BF16)
    kaug_scr[:, LANES:KA] = kx_ref[...]
    _store_values_t(vt_scr, v_ref[0].astype(F32).T, seq)
    sl = sl_ref[0]
    bound = _logit_bound(qg_ref, [kg_ref]) + _alibi_at_query(sl, seq) if bounded else None
    qt = _headnorm_pair(q_ref[0].astype(F32), bd, qg_ref[...]).T
    for e in range(2):
        qt_scr[e] = _query_feats(qt, e, sl, None, bound)

    vt_rows = [lambda ks: vt_scr[:, ks]] * 2
    acc = _flash(_causal_blocks(nt, kaug_scr, vt_rows, qt_scr), 2 * nt, LANES, bounded)

    lv = lam_ref[...]
    lam = (jnp.exp(jnp.sum(lv[0:1] * lv[1:2], axis=-1, keepdims=True))
           - jnp.exp(jnp.sum(lv[2:3] * lv[3:4], axis=-1, keepdims=True)) + lam_init)
    for qh in range(nt):
        a0, a1 = acc[qh], acc[nt + qh]
        ob = (a0[0:LANES] / a0[LANES:LANES + 1] - lam * (a1[0:LANES] / a1[LANES:LANES + 1])).T
        ob = ob * lax.rsqrt(jnp.mean(ob * ob, axis=-1, keepdims=True) + EPS) * sg_ref[...]
        o_ref[0, qh * BLK:(qh + 1) * BLK, :] = (ob * (1.0 - lam_init)).astype(o_ref.dtype)


def _diff(proj3, kx, slrows, lam_p, qg, kg, sg, bd, lam_init, *, bounded):
    b, s, _ = proj3.shape
    const = lambda a: pl.BlockSpec(a.shape, lambda bi, h: (0,) * a.ndim)
    col = lambda c: pl.BlockSpec((1, s, LANES), lambda bi, h: (bi, 0, c + h))
    return pl.pallas_call(
        functools.partial(_diff_kernel, lam_init=lam_init, bounded=bounded),
        grid=(b, DIFF_HEADS),
        in_specs=[col(COL_DIFF_Q), col(COL_DIFF_K), col(COL_DIFF_V), const(kx),
                  pl.BlockSpec((1, 8, s), lambda bi, h: (h, 0, 0)),
                  const(lam_p), const(qg), const(kg), const(sg), const(bd)],
        out_specs=pl.BlockSpec((1, s, LANES), lambda bi, h: (bi, 0, h)),
        out_shape=jax.ShapeDtypeStruct((b, s, DIFF_HEADS * LANES), BF16),
        scratch_shapes=[pltpu.VMEM((s, KA), BF16), pltpu.VMEM((LANES + VAUG, s), BF16),
                        pltpu.VMEM((2, KA, s), BF16)],
        compiler_params=_cparams(("parallel", "parallel")),
        name="diff",
    )(proj3, proj3, proj3, kx, slrows, lam_p, qg, kg, sg, bd)


def _nsa_select(imp_t, t0):
    ns, tq = imp_t.shape
    blk = lax.broadcasted_iota(jnp.int32, (ns, tq), 0)
    cur = (t0 + lax.broadcasted_iota(jnp.int32, (ns, tq), 1)) // SLC_BLOCK
    forced = (blk == 0) | (blk == cur) | (blk == cur - 1)
    score = jnp.where(blk <= cur, jnp.where(forced, imp_t + FORCE_BONUS, imp_t), NEG)
    sel = jnp.zeros((ns, tq), F32)
    blk_f = blk.astype(F32)
    for _ in range(min(SLC_TOPN, ns)):
        best = jnp.max(score, axis=0, keepdims=True)
        first = jnp.min(jnp.where(score == best, blk_f, float(ns)), axis=0, keepdims=True)
        take = blk_f == first
        sel = jnp.where(take, 1.0, sel)
        score = jnp.where(take, 2.0 * NEG, score)
    return sel


def _slope_rows(slope):
    hi, mid, lo = [float(np.float32(v)) for v in _np_split3(slope * L2E)]
    return [hi, hi, mid, mid, lo, lo]


def _np_split3(x):
    x = np.float32(x)
    hi = np.float32(x.astype(BF16))
    mid = np.float32((x - hi).astype(BF16))
    lo = np.float32((x - hi - mid).astype(BF16))
    return hi, mid, lo


CMP_ROWS = 512


def _nsa_prepare(qall_ref, kc_ref, vc_ref, ks_ref, vs_ref, kw_ref, vw_ref, kx_ref, qg_ref, ksg_ref,
                 kwg_ref, bd_ref, ot_ref, ksaug_scr, kwaug_scr, vst_scr, vwt_scr, notsel_scr,
                 ocmp_scr, *, nc):
    bd = bd_ref[...]
    npair = NSA_HEADS // 2
    seq = ks_ref.shape[1]
    ksaug_scr[:, 0:LANES] = _headnorm_pair(ks_ref[0].astype(F32), bd, ksg_ref[...]).astype(BF16)
    kwaug_scr[:, 0:LANES] = _headnorm_pair(kw_ref[0].astype(F32), bd, kwg_ref[...]).astype(BF16)
    ksaug_scr[:, LANES:KA] = kx_ref[...]
    kwaug_scr[:, LANES:KA] = kx_ref[...]
    for v_ref, vt_scr in ((vs_ref, vst_scr), (vw_ref, vwt_scr)):
        vt = v_ref[0].astype(F32).T
        for e in range(2):
            _store_values_t(vt_scr.at[e], vt[e * HEAD_DIM:(e + 1) * HEAD_DIM], seq)

    lane = lax.broadcasted_iota(jnp.int32, (CMP_ROWS, LANES), 1)
    low = lane < HEAD_DIM
    cend = lane * CMP_STRIDE + (CMP_LEN - 1)
    cend_f = (lax.broadcasted_iota(jnp.int32, (1, LANES), 1) * CMP_STRIDE + (CMP_LEN - 1)).astype(F32)
    kc = kc_ref[0]
    vc = vc_ref[0]

    def chunk(c, carry):
        t0 = pl.multiple_of(c * CMP_ROWS, CMP_ROWS)
        tcol = t0 + lax.broadcasted_iota(jnp.int32, (CMP_ROWS, LANES), 0)
        cmask = (tcol >= cend) & (lane < nc)
        psum = [jnp.zeros((CMP_ROWS, LANES), F32), jnp.zeros((CMP_ROWS, LANES), F32)]
        for r in range(npair):
            q2 = _headnorm_pair(qall_ref[0, pl.ds(t0, CMP_ROWS), r * LANES:(r + 1) * LANES].astype(F32),
                                bd, qg_ref[...])
            halves = []
            for e in range(2):
                slope = 2.0 ** -(r + 1 + 4 * e)
                qm = jnp.where(low if e == 0 else ~low, q2, 0.0).astype(BF16)
                s = jnp.where(cmask, _qk(qm, kc) + (slope * L2E) * cend_f, NEG)
                m = jnp.max(s, axis=-1, keepdims=True)
                pe = jnp.where(cmask, jnp.exp2(s - m), 0.0)
                l = jnp.sum(pe, axis=-1, keepdims=True)
                pn = pe / jnp.where(l > 0.0, l, 1.0)
                psum[e] = psum[e] + pn
                halves.append(jnp.dot(pn.astype(BF16), vc, preferred_element_type=F32))
            ocmp_scr[r, pl.ds(t0, CMP_ROWS), :] = jnp.where(low, halves[0], halves[1])
        for g in range(NSA_GROUPS):
            hi, mid, lo = _split3(psum[g])
            imp_t = (_qk(ot_ref[...], hi.astype(BF16)) + _qk(ot_ref[...], mid.astype(BF16))
                     + _qk(ot_ref[...], lo.astype(BF16)))
            notsel_scr[g, :, pl.ds(t0, CMP_ROWS)] = (_nsa_select(imp_t, t0) - 1.0) * -NEG
        return carry

    lax.fori_loop(0, seq // CMP_ROWS, chunk, 0)


def _nsa_kernel(qall_ref, q_ref, sm_ref, kc_ref, vc_ref, ks_ref, vs_ref, kw_ref, vw_ref, kx_ref,
                slr_ref, qg_ref, ksg_ref, kwg_ref, bd_ref, ot_ref, egt_ref, eg_ref, o_ref,
                ksaug_scr, kwaug_scr, vst_scr, vwt_scr, qts_scr, qtw_scr, notsel_scr, ocmp_scr,
                *, nc, bounded):
    r = pl.program_id(1)
    bd = bd_ref[...]
    seq = ks_ref.shape[1]
    nt = seq // BLK

    @pl.when(r == 0)
    def _():
        _nsa_prepare(qall_ref, kc_ref, vc_ref, ks_ref, vs_ref, kw_ref, vw_ref, kx_ref, qg_ref,
                     ksg_ref, kwg_ref, bd_ref, ot_ref, ksaug_scr, kwaug_scr, vst_scr, vwt_scr,
                     notsel_scr, ocmp_scr, nc=nc)

    qk_bound = _logit_bound(qg_ref, [ksg_ref, kwg_ref])
    qt = _headnorm_pair(q_ref[0].astype(F32), bd, qg_ref[...]).T
    for e in range(2):
        sl = slr_ref[0, e]
        bound = qk_bound + _alibi_at_query(sl, seq) if bounded else None
        qts_scr[e] = _query_feats(qt, e, sl, notsel_scr[e], bound)
        qtw_scr[e] = _query_feats(qt, e, sl, None, bound)

    key = lax.broadcasted_iota(jnp.int32, (BLK, BLK), 0)
    qry = lax.broadcasted_iota(jnp.int32, (BLK, BLK), 1)
    diag = key <= qry
    above = key > qry
    vs_rows = [lambda ks, e=e: vst_scr[e, :, ks] for e in range(2)]
    slc = _causal_blocks(nt, ksaug_scr, vs_rows, qts_scr)

    def win_block(j, qh, e, mask):
        ks = slice(j * BLK, (j + 1) * BLK)
        qs = slice(qh * BLK, (qh + 1) * BLK)
        return (lambda: kwaug_scr[ks, :], lambda: vwt_scr[e, :, ks], lambda: qtw_scr[e, :, qs],
                mask, 2 * nt + e * nt + qh)

    blocks = []
    pos = 0
    for j in range(nt):
        n_j = 2 * (nt - j)
        blocks.extend(slc[pos:pos + n_j])
        pos += n_j
        blocks.extend(win_block(j, j, e, diag) for e in range(2))
        if j > 0:
            blocks.extend(win_block(j - 1, j, e, above) for e in range(2))
    acc = _flash(blocks, 4 * nt, HEAD_DIM, bounded)
    normed = lambda a: a[0:HEAD_DIM] / a[HEAD_DIM:HEAD_DIM + 1]

    gate = jax.nn.sigmoid(sm_ref[0])
    gate_t = gate.T
    g_cmp = _dot01(gate, eg_ref[0])
    g_slc_t = _dot01_left(egt_ref[1], gate_t)
    g_win_t = _dot01_left(egt_ref[2], gate_t)
    for qh in range(nt):
        cols = slice(qh * BLK, (qh + 1) * BLK)
        o_s = jnp.concatenate([normed(acc[e * nt + qh]) for e in range(2)], axis=0)
        o_w = jnp.concatenate([normed(acc[2 * nt + e * nt + qh]) for e in range(2)], axis=0)
        mix = (g_slc_t[:, cols] * o_s + g_win_t[:, cols] * o_w).T
        out = mix + g_cmp[cols, :] * ocmp_scr[r, cols, :]
        o_ref[0, cols, :] = out.astype(o_ref.dtype)


def _nsa(proj3, small3, kc, vc, kx, slr, qg, ksg, kwg, bd, ot, egt, eg, *, bounded):
    b, s, _ = proj3.shape
    npair = NSA_HEADS // 2
    ns = s // SLC_BLOCK
    nc = (s - CMP_LEN) // CMP_STRIDE + 1
    const = lambda a: pl.BlockSpec(a.shape, lambda bi, r: (0,) * a.ndim)
    kvspec = lambda c: pl.BlockSpec((1, s, LANES), lambda bi, r: (bi, 0, c))
    return pl.pallas_call(
        functools.partial(_nsa_kernel, nc=nc, bounded=bounded),
        grid=(b, npair),
        in_specs=[
            pl.BlockSpec((1, s, npair * LANES), lambda bi, r: (bi, 0, COL_NSA_Q // npair)),
            pl.BlockSpec((1, s, LANES), lambda bi, r: (bi, 0, COL_NSA_Q + r)),
            pl.BlockSpec((1, s, LANES), lambda bi, r: (bi, 0, 0)),
            pl.BlockSpec((1,) + kc.shape[1:], lambda bi, r: (bi, 0, 0)),
            pl.BlockSpec((1,) + vc.shape[1:], lambda bi, r: (bi, 0, 0)),
            kvspec(COL_K_SLC), kvspec(COL_V_SLC), kvspec(COL_K_WIN), kvspec(COL_V_WIN),
            const(kx),
            pl.BlockSpec((1, 2, 8, s), lambda bi, r: (r, 0, 0, 0)),
            const(qg), const(ksg), const(kwg), const(bd), const(ot),
            pl.BlockSpec((3, LANES, LANES), lambda bi, r: (0, r, 0)),
            pl.BlockSpec((3, LANES, LANES), lambda bi, r: (0, 0, r)),
        ],
        out_specs=pl.BlockSpec((1, s, LANES), lambda bi, r: (bi, 0, r)),
        out_shape=jax.ShapeDtypeStruct((b, s, npair * LANES), BF16),
        scratch_shapes=[pltpu.VMEM((s, KA), BF16), pltpu.VMEM((s, KA), BF16),
                        pltpu.VMEM((2, HEAD_DIM + VAUG, s), BF16),
                        pltpu.VMEM((2, HEAD_DIM + VAUG, s), BF16),
                        pltpu.VMEM((2, KA, s), BF16), pltpu.VMEM((2, KA, s), BF16),
                        pltpu.VMEM((NSA_GROUPS, ns, s), F32), pltpu.VMEM((npair, s, LANES), F32)],
        compiler_params=_cparams(("parallel", "arbitrary")),
        name="nsa",
    )(proj3, proj3, small3, kc, vc, proj3, proj3, proj3, proj3, kx, slr, qg, ksg, kwg, bd, ot,
      egt, eg)


def _merge_kernel(oa_ref, ob_ref, oc_ref, g0_ref, g1_ref, g2_ref, h_ref, wbr_ref, wo_ref, o_ref):
    merged = None
    for c, (o_r, g_r) in enumerate(((oa_ref, g0_ref), (ob_ref, g1_ref), (oc_ref, g2_ref))):
        y = jnp.dot(o_r[...], wbr_ref[c], preferred_element_type=F32)
        term = jax.nn.sigmoid(g_r[...].astype(F32)) * y
        merged = term if merged is None else merged + term
    o_ref[...] = h_ref[...] + jnp.dot(merged.astype(BF16), wo_ref[...], preferred_element_type=F32)


def _merge(oa, ob, oc, proj2, h2, wbr, wo, *, tm=512):
    t, d = h2.shape
    bw = oa.shape[1]
    row = lambda w: pl.BlockSpec((tm, w), lambda i: (i, 0))
    gate = lambda c: pl.BlockSpec((tm, d), lambda i: (i, COL_MG * LANES // d + c))
    return pl.pallas_call(
        _merge_kernel,
        grid=(t // tm,),
        in_specs=[row(bw), row(bw), row(bw), gate(0), gate(1), gate(2), row(d),
                  pl.BlockSpec(wbr.shape, lambda i: (0, 0, 0)),
                  pl.BlockSpec(wo.shape, lambda i: (0, 0))],
        out_specs=row(d),
        out_shape=jax.ShapeDtypeStruct((t, d), F32),
        compiler_params=_cparams(("parallel",)),
        name="merge",
    )(oa, ob, oc, proj2, proj2, proj2, h2, wbr, wo)


HALO = 16


def _ffn_kernel(x_ref, xh_ref, g_ref, wup_ref, cw_ref, cb_ref, wd_ref, o_ref, a_scr, u_scr,
                *, tm, tf, tiles_per_seq):
    i = pl.program_id(0)
    dff = wd_ref.shape[0]

    def norm(x):
        return (x * lax.rsqrt(jnp.mean(x * x, axis=-1, keepdims=True) + EPS)
                * g_ref[...]).astype(BF16)

    a_scr[0:HALO, :] = norm(xh_ref[...])
    a_scr[HALO:, :] = norm(x_ref[...])
    seq_start = (i % tiles_per_seq) == 0
    acts = []
    for c in range(dff // tf):
        cols = slice(c * tf, (c + 1) * tf)
        u = jnp.dot(a_scr[...], wup_ref[:, cols], preferred_element_type=F32)
        rows = lax.broadcasted_iota(jnp.int32, u.shape, 0)
        u_scr[...] = jnp.where((rows < HALO) & seq_start, 0.0, u)
        uc = cb_ref[:, cols]
        for t in range(CONV_W):
            uc = uc + cw_ref[t:t + 1, cols] * u_scr[pl.ds(HALO - (CONV_W - 1) + t, tm), :]
        gt = jnp.dot(a_scr[HALO:, :], wup_ref[:, dff + c * tf:dff + (c + 1) * tf],
                     preferred_element_type=F32)
        acts.append((jax.nn.gelu(uc) * gt).astype(BF16))
    act = jnp.concatenate(acts, axis=1)
    o_ref[...] = x_ref[...] + jnp.dot(act, wd_ref[...], preferred_element_type=F32)


def _ffn(h2, g, wup, cw, cb, wd, seq, *, tm=512, tf=1408):
    t, d = h2.shape
    resident = lambda a: pl.BlockSpec(a.shape, lambda i: (0,) * a.ndim, pipeline_mode=pl.Buffered(1))
    return pl.pallas_call(
        functools.partial(_ffn_kernel, tm=tm, tf=tf, tiles_per_seq=seq // tm),
        grid=(t // tm,),
        in_specs=[
            pl.BlockSpec((tm, d), lambda i: (i, 0)),
            pl.BlockSpec((HALO, d), lambda i: (jnp.maximum(i * (tm // HALO) - 1, 0), 0)),
            resident(g), resident(wup), resident(cw), resident(cb), resident(wd),
        ],
        out_specs=pl.BlockSpec((tm, d), lambda i: (i, 0)),
        out_shape=jax.ShapeDtypeStruct((t, d), F32),
        scratch_shapes=[pltpu.VMEM((HALO + tm, d), BF16), pltpu.VMEM((HALO + tm, tf), F32)],
        compiler_params=_cparams(("parallel",)),
        name="ffn",
    )(h2, h2, g, wup, cw, cb, wd)


def _nsa_head_order():
    hpg = NSA_HEADS // NSA_GROUPS
    return [h for r in range(hpg) for h in (r, hpg + r)]


def _constants(seq):
    ns = seq // SLC_BLOCK
    nc = (seq - CMP_LEN) // CMP_STRIDE + 1
    assert MASK0 + ns <= LANES and MASK0 >= 8 and N_BIAS <= 8
    bd = np.kron(np.eye(2), np.ones((HEAD_DIM, HEAD_DIM))).astype(np.float32)
    c_start = np.arange(LANES) * CMP_STRIDE
    s_start = np.arange(ns) * SLC_BLOCK
    ot = ((c_start[None, :] < s_start[:, None] + SLC_BLOCK)
          & (c_start[None, :] + CMP_LEN > s_start[:, None])
          & (np.arange(LANES)[None, :] < nc)).astype(np.float32)
    pos = np.arange(seq)
    kx = np.zeros((seq, LANES), np.float32)
    for k in range(N_BIAS):
        kx[:, k] = (pos // 8) * 8 if k % 2 == 0 else pos % 8
    kx[pos, MASK0 + pos // SLC_BLOCK] = 1.0
    kx[:, BOUND0:BOUND0 + 3] = 1.0
    order = _nsa_head_order()
    eg = np.zeros((3, LANES, NSA_HEADS * HEAD_DIM), np.float32)
    for c in range(3):
        for slot, h in enumerate(order):
            eg[c, SMALL_GATE0 + c * NSA_HEADS + h, slot * HEAD_DIM:(slot + 1) * HEAD_DIM] = 1.0
    u = (np.arange(BLK)[:, None] <= np.arange(BLK)[None, :]).astype(np.float32)
    dsl = np.zeros((DIFF_HEADS, 8, seq), np.float32)
    for h in range(DIFF_HEADS):
        rows = _slope_rows(2.0 ** (-8.0 * (h + 1) / DIFF_HEADS))
        dsl[h, :N_BIAS, :] = np.asarray(rows, np.float32)[:, None]
    nsl = np.zeros((NSA_HEADS // 2, 2, 8, seq), np.float32)
    for slot, h in enumerate(order):
        rows = _slope_rows(2.0 ** (-8.0 * (h + 1) / NSA_HEADS))
        nsl[slot // 2, slot % 2, :N_BIAS, :] = np.asarray(rows, np.float32)[:, None]
    as_bf = lambda a: jnp.asarray(a, BF16)
    return dict(bd=as_bf(bd), ot=as_bf(ot), kx=as_bf(kx), eg=as_bf(eg),
                egt=as_bf(np.transpose(eg, (0, 2, 1))), u=as_bf(u), dsl=jnp.asarray(dsl, F32),
                nsl=jnp.asarray(nsl, F32))


def _pack_w_in(w):
    d = w.shape[0]
    hd = HEAD_DIM
    sizes = [512, 768, 24, 512, 512, 512, 512, 512, 512, 8, 3 * d]
    offs = np.concatenate([[0], np.cumsum(sizes)])
    nq, nkv, ngate, dq, dk, dv, fq, fk, fv, ff, mg = [w[:, offs[k]:offs[k + 1]] for k in range(11)]
    nq = jnp.concatenate([nq[:, h * hd:(h + 1) * hd] for h in _nsa_head_order()], axis=1)
    kv = lambda c, k: nkv[:, (c * 2 + k) * LANES:(c * 2 + k + 1) * LANES]
    cols = [mg, nq, dq, dk, dv, fq, fk, fv,
            kv(1, 0), kv(1, 1), kv(2, 0), kv(2, 1), kv(0, 0), kv(0, 1)]
    main = jnp.concatenate(cols, axis=1)
    main = jnp.pad(main, ((0, 0), (0, PROJ_UNITS * LANES - main.shape[1])))
    small = jnp.pad(jnp.concatenate([ngate, ff], axis=1), ((0, 0), (0, LANES - 32)))
    return main.astype(BF16), small.astype(BF16)


def kernel(x, attn_norm_g, w_in, nsa_q_g, nsa_k_g, cmp_pe, cmp_w1, cmp_w2, diff_q_g, diff_k_g,
           diff_lam, diff_subln_g, fox_q_g, fox_k_g, fox_b, w_br, w_o, ffn_norm_g, w_up, conv_w,
           conv_b, w_down):
    b, s, d = x.shape
    depth = w_in.shape[0]
    hd = HEAD_DIM
    qscale = hd ** -0.5 * L2E
    cst = _constants(s)
    rows16 = s // CMP_STRIDE
    assert rows16 == LANES and s % CMP_ROWS == 0, "NSA kernel keeps all compressed blocks in one 128-lane tile"
    order = _nsa_head_order()
    tile2 = lambda g: jnp.tile(g, 2).reshape(1, LANES).astype(F32)

    h = x.reshape(b * s, d)
    for l in range(depth):
        wmain, wsmall = _pack_w_in(w_in[l])
        proj, small = _proj(h, attn_norm_g[l].reshape(1, d), wmain, wsmall)
        proj3 = proj.reshape(b, s, PROJ_UNITS * LANES)
        small3 = small.reshape(b, s, LANES)

        raw = proj3[:, :, COL_K_CMP * LANES:(COL_V_CMP + 1) * LANES]
        raw = raw.reshape(b, rows16, CMP_STRIDE, 2 * NSA_GROUPS, hd)
        raw = jnp.transpose(raw, (0, 3, 1, 2, 4)).reshape(b, 2 * NSA_GROUPS, rows16, CMP_STRIDE * hd)
        half = CMP_STRIDE * hd
        w1cat = jnp.concatenate([cmp_w1[l][:, :half], cmp_w1[l][:, half:]], axis=2).astype(BF16)
        pe8 = jnp.broadcast_to(cmp_pe[l].reshape(2, 1, CMP_LEN * hd), (2, 8, CMP_LEN * hd)).astype(BF16)
        kc, vc = _compress(raw, w1cat, pe8, cmp_w1[l].astype(BF16), cmp_w2[l].astype(BF16),
                           nsa_k_g[l, 0].reshape(1, hd))

        fb_row = jnp.zeros((1, LANES), F32).at[0, SMALL_FF0:SMALL_FF0 + FOX_HEADS].set(fox_b[l])
        negf = _fcum(small3, fb_row, cst["u"])

        def mixer(call, qg, kgs, *args):
            span = 2.0 * HEAD_DIM * BOUND_SLACK * jnp.max(jnp.abs(qg)) * max_abs(kgs)
            return lax.cond(span <= BOUND_LIMIT, functools.partial(call, bounded=True),
                            functools.partial(call, bounded=False), *args)

        max_abs = lambda gs: functools.reduce(jnp.maximum, [jnp.max(jnp.abs(g)) for g in gs])
        nsa_qg, diff_qg, fox_qg = nsa_q_g[l] * qscale, diff_q_g[l] * qscale, fox_q_g[l] * qscale
        o_a = mixer(_nsa, nsa_qg, [nsa_k_g[l, 1], nsa_k_g[l, 2]],
                    proj3, small3, kc, vc, cst["kx"], cst["nsl"], tile2(nsa_qg),
                    tile2(nsa_k_g[l, 1]), tile2(nsa_k_g[l, 2]), cst["bd"], cst["ot"], cst["egt"],
                    cst["eg"])
        lam_init = 0.8 - 0.6 * math.exp(-0.3 * l)
        o_b = mixer(functools.partial(_diff, lam_init=lam_init), diff_qg, [diff_k_g[l]],
                    proj3, cst["kx"], cst["dsl"], diff_lam[l], tile2(diff_qg), tile2(diff_k_g[l]),
                    diff_subln_g[l].reshape(1, LANES), cst["bd"])
        o_c = mixer(_fox, fox_qg, [fox_k_g[l]],
                    proj3, negf, tile2(fox_qg), tile2(fox_k_g[l]), cst["bd"])

        wbr = w_br[l]
        wbr_a = jnp.concatenate([wbr[0, hh * hd:(hh + 1) * hd] for hh in order], axis=0)
        wbr_p = jnp.stack([wbr_a, wbr[1], wbr[2]]).astype(BF16)
        bw = NSA_HEADS * hd
        h = _merge(o_a.reshape(b * s, bw), o_b.reshape(b * s, bw), o_c.reshape(b * s, bw),
                   proj, h, wbr_p, w_o[l].astype(BF16))
        h = _ffn(h, ffn_norm_g[l].reshape(1, d), w_up[l].astype(BF16), conv_w[l],
                 conv_b[l].reshape(1, -1), w_down[l].astype(BF16), s)
    return h.reshape(b, s, d)
```

```python
import functools
import math

import numpy as np
import jax
import jax.numpy as jnp
from jax import lax
from jax.experimental import pallas as pl
from jax.experimental.pallas import tpu as pltpu

F32 = jnp.float32
BF16 = jnp.bfloat16

HEAD_DIM = 64
NSA_HEADS = 8
NSA_GROUPS = 2
CMP_LEN = 32
CMP_STRIDE = 16
SLC_BLOCK = 64
SLC_TOPN = 8
WINDOW = 256
FORCE_BONUS = 1.0e4
DIFF_HEADS = 4
FOX_HEADS = 8
CONV_W = 3
EPS = 1e-6
NEG = -1.0e30
L2E = 1.4426950408889634

LANES = 128
BLK = 256
KA = 2 * LANES
N_BIAS = 6
MASK0 = 8
BOUND0 = 40
VAUG = 16
BOUND_SLACK = 1.02
BOUND_LIMIT = 100.0
VMEM_LIMIT = 56 * 1024 * 1024

COL_MG = 0
COL_NSA_Q = 24
COL_DIFF_Q, COL_DIFF_K, COL_DIFF_V = 28, 32, 36
COL_FOX_Q, COL_FOX_K, COL_FOX_V = 40, 44, 48
COL_K_SLC, COL_V_SLC, COL_K_WIN, COL_V_WIN, COL_K_CMP, COL_V_CMP = 52, 53, 54, 55, 56, 57
PROJ_UNITS = 60
SMALL_GATE0 = 0
SMALL_FF0 = 24


def _cparams(sem):
    return pltpu.CompilerParams(dimension_semantics=sem, vmem_limit_bytes=VMEM_LIMIT)


def _split3(x):
    hi = x.astype(BF16).astype(F32)
    r1 = x - hi
    mid = r1.astype(BF16).astype(F32)
    lo = (r1 - mid).astype(BF16).astype(F32)
    return hi, mid, lo


def _dot01(x, m):
    hi = x.astype(BF16)
    lo = (x - hi.astype(F32)).astype(BF16)
    return (jnp.dot(hi, m, preferred_element_type=F32)
            + jnp.dot(lo, m, preferred_element_type=F32))


def _dot01_left(m, x):
    hi = x.astype(BF16)
    lo = (x - hi.astype(F32)).astype(BF16)
    return (jnp.dot(m, hi, preferred_element_type=F32)
            + jnp.dot(m, lo, preferred_element_type=F32))


def _dot01_3(x, m):
    hi, mid, lo = _split3(x)
    return (jnp.dot(hi.astype(BF16), m, preferred_element_type=F32)
            + jnp.dot(mid.astype(BF16), m, preferred_element_type=F32)
            + jnp.dot(lo.astype(BF16), m, preferred_element_type=F32))


def _qk(q, k):
    return lax.dot_general(q, k, (((1,), (1,)), ((), ())), preferred_element_type=F32)


def _headnorm_pair(x, bd, gain):
    ss = jnp.dot((x * x).astype(BF16), bd, preferred_element_type=F32)
    return x * lax.rsqrt(ss * (1.0 / HEAD_DIM) + EPS) * gain


def _rows8(vals, width):
    row = lax.broadcasted_iota(jnp.int32, (8, width), 0)
    out = jnp.zeros((8, width), F32)
    for k, v in enumerate(vals):
        out = jnp.where(row == k, v, out)
    return out


def _query_feats(qt, half, bias8, mask_rows, bound_row):
    tq = qt.shape[1]
    row = lax.broadcasted_iota(jnp.int32, (LANES, tq), 0)
    keep = (row < HEAD_DIM) if half == 0 else (row >= HEAD_DIM)
    nmask = BOUND0 - MASK0
    parts = [jnp.where(keep, qt, 0.0), bias8,
             mask_rows if mask_rows is not None else jnp.zeros((nmask, tq), F32),
             _rows8(_split3(-bound_row), tq) if bound_row is not None else jnp.zeros((8, tq), F32),
             jnp.zeros((LANES - BOUND0 - 8, tq), F32)]
    return jnp.concatenate(parts, axis=0).astype(BF16)


def _logit_bound(qg_ref, kg_refs):
    kmax = None
    for kg_ref in kg_refs:
        k = jnp.max(jnp.abs(kg_ref[...]), axis=-1, keepdims=True)
        kmax = k if kmax is None else jnp.maximum(kmax, k)
    return jnp.max(jnp.abs(qg_ref[...]), axis=-1, keepdims=True) * kmax * (HEAD_DIM * BOUND_SLACK)


def _store_values_t(vt_view, rows, seq):
    dv = rows.shape[0]
    vt_view[0:dv, :] = rows.astype(BF16)
    vt_view[dv:dv + 8, :] = _rows8([1.0], seq).astype(BF16)
    vt_view[dv + 8:dv + VAUG, :] = jnp.zeros((VAUG - 8, seq), BF16)


QK_AHEAD = 2


def _flash(blocks, n_chain, dv, bounded):
    acc = [jnp.zeros((dv + VAUG, BLK), F32) for _ in range(n_chain)]
    mx = [jnp.full((1, BLK), NEG, F32) for _ in range(n_chain)]
    scores = {}

    def issue(k):
        if k < len(blocks):
            scores[k] = jnp.dot(blocks[k][0](), blocks[k][2](), preferred_element_type=F32)

    for k in range(2 * QK_AHEAD):
        issue(k)
    for k0 in range(0, len(blocks), 2):
        issue(k0 + 2 * QK_AHEAD)
        issue(k0 + 2 * QK_AHEAD + 1)
        pending = []
        for k in range(k0, min(k0 + 2, len(blocks))):
            _, vt, _, mask, chain = blocks[k]
            s = scores.pop(k)
            if mask is not None:
                s = jnp.where(mask, s, NEG)
            if bounded:
                pending.append((vt, chain, None, jnp.exp2(s).astype(BF16)))
            else:
                m_new = jnp.maximum(mx[chain], jnp.max(s, axis=0, keepdims=True))
                alpha = jnp.exp2(mx[chain] - m_new)
                mx[chain] = m_new
                pending.append((vt, chain, alpha, jnp.exp2(s - m_new).astype(BF16)))
        assert len({c for _, c, _, _ in pending}) == len(pending), "a pair must not share a chain"
        for vt, chain, alpha, p in pending:
            prev = acc[chain] if alpha is None else alpha * acc[chain]
            acc[chain] = prev + jnp.dot(vt(), p, preferred_element_type=F32)
    return acc


def _causal_blocks(nt, kaug_scr, vt_rows, qt_scr):
    key = lax.broadcasted_iota(jnp.int32, (BLK, BLK), 0)
    qry = lax.broadcasted_iota(jnp.int32, (BLK, BLK), 1)
    diag = key <= qry
    blocks = []
    for j in range(nt):
        ks = slice(j * BLK, (j + 1) * BLK)
        for qh in range(j, nt):
            qs = slice(qh * BLK, (qh + 1) * BLK)
            for e in range(2):
                blocks.append((lambda ks=ks: kaug_scr[ks, :],
                               lambda e=e, ks=ks: vt_rows[e](ks),
                               lambda e=e, qs=qs: qt_scr[e, :, qs],
                               diag if qh == j else None, e * nt + qh))
    return blocks


def _proj_kernel(x_ref, g_ref, w_ref, ws_ref, o_ref, os_ref, a_scr):
    j = pl.program_id(1)

    @pl.when(j == 0)
    def _():
        x = x_ref[...]
        inv = lax.rsqrt(jnp.mean(x * x, axis=-1, keepdims=True) + EPS)
        a = (x * inv * g_ref[...]).astype(BF16)
        a_scr[...] = a
        os_ref[...] = jnp.dot(a, ws_ref[...], preferred_element_type=F32)

    o_ref[...] = jnp.dot(a_scr[...], w_ref[...], preferred_element_type=F32).astype(o_ref.dtype)


def _proj(x2d, g, w, ws, *, tm=1024, tn=2560):
    t, d = x2d.shape
    n = w.shape[1]
    return pl.pallas_call(
        _proj_kernel,
        grid=(t // tm, n // tn),
        in_specs=[
            pl.BlockSpec((tm, d), lambda i, j: (i, 0)),
            pl.BlockSpec((1, d), lambda i, j: (0, 0)),
            pl.BlockSpec((d, tn), lambda i, j: (0, j)),
            pl.BlockSpec((d, LANES), lambda i, j: (0, 0)),
        ],
        out_specs=[
            pl.BlockSpec((tm, tn), lambda i, j: (i, j)),
            pl.BlockSpec((tm, LANES), lambda i, j: (i, 0)),
        ],
        out_shape=[jax.ShapeDtypeStruct((t, n), BF16), jax.ShapeDtypeStruct((t, LANES), F32)],
        scratch_shapes=[pltpu.VMEM((tm, d), BF16)],
        compiler_params=_cparams(("parallel", "arbitrary")),
        name="proj",
    )(x2d, g, w, ws)


def _compress_kernel(k_ref, v_ref, w1p_ref, pe_ref, w1f_ref, w2p_ref, kg_ref, bd_ref,
                     kc_ref, vc_ref, raw_scr):
    seq = k_ref.shape[1]
    nrow = kc_ref.shape[1]
    raw_scr[seq:, :] = jnp.zeros((raw_scr.shape[0] - seq, LANES), F32)
    for kv, (src, dst) in enumerate(((k_ref, kc_ref), (v_ref, vc_ref))):
        raw_scr[0:seq, :] = src[0].astype(F32)
        c1 = jnp.dot(pe_ref[kv], w1f_ref[kv], preferred_element_type=F32)[0:1]
        pre = jnp.concatenate([c1, c1], axis=1)
        for l in range(CMP_LEN):
            rows = raw_scr[pl.ds(l, nrow, stride=CMP_STRIDE), :].astype(BF16)
            pre = pre + jnp.dot(rows, w1p_ref[kv, l], preferred_element_type=F32)
        hid = jax.nn.gelu(pre).astype(BF16)
        o = jnp.dot(hid, w2p_ref[kv], preferred_element_type=F32)
        if kv == 0:
            o = _headnorm_pair(o, bd_ref[...], kg_ref[...])
        dst[0] = o.astype(BF16)


def _compress(proj3, w1p, pe8, w1f, w2p, kg2, bd):
    b, s, _ = proj3.shape
    nrow = s // CMP_STRIDE
    full = lambda a: pl.BlockSpec(a.shape, lambda i: (0,) * a.ndim)
    return pl.pallas_call(
        _compress_kernel,
        grid=(b,),
        in_specs=[pl.BlockSpec((1, s, LANES), lambda i: (i, 0, COL_K_CMP)),
                  pl.BlockSpec((1, s, LANES), lambda i: (i, 0, COL_V_CMP)),
                  full(w1p), full(pe8), full(w1f), full(w2p), full(kg2), full(bd)],
        out_specs=[pl.BlockSpec((1, nrow, LANES), lambda i: (i, 0, 0)),
                   pl.BlockSpec((1, nrow, LANES), lambda i: (i, 0, 0))],
        out_shape=[jax.ShapeDtypeStruct((b, nrow, LANES), BF16),
                   jax.ShapeDtypeStruct((b, nrow, LANES), BF16)],
        scratch_shapes=[pltpu.VMEM((s + CMP_LEN, LANES), F32)],
        compiler_params=_cparams(("parallel",)),
        name="compress",
    )(proj3, proj3, w1p, pe8, w1f, w2p, kg2, bd)


def _fcum_kernel(s_ref, fb_ref, u_ref, o_ref, *, chunk):
    z = s_ref[0] + fb_ref[...]
    lf = jax.nn.log_sigmoid(z)
    lft = lf.T[SMALL_FF0:SMALL_FF0 + FOX_HEADS]
    seq = lft.shape[1]
    carry = jnp.zeros((FOX_HEADS, 1), F32)
    for c in range(seq // chunk):
        fc = _dot01_3(lft[:, c * chunk:(c + 1) * chunk], u_ref[...]) + carry
        o_ref[0, :, c * chunk:(c + 1) * chunk] = -fc
        carry = fc[:, chunk - 1:chunk]


def _fcum(small3, fb_row, u):
    b, s, _ = small3.shape
    chunk = u.shape[0]
    return pl.pallas_call(
        functools.partial(_fcum_kernel, chunk=chunk),
        grid=(b,),
        in_specs=[pl.BlockSpec((1, s, LANES), lambda i: (i, 0, 0)),
                  pl.BlockSpec((1, LANES), lambda i: (0, 0)),
                  pl.BlockSpec(u.shape, lambda i: (0, 0))],
        out_specs=pl.BlockSpec((1, FOX_HEADS, s), lambda i: (i, 0, 0)),
        out_shape=jax.ShapeDtypeStruct((b, FOX_HEADS, s), F32),
        compiler_params=_cparams(("parallel",)),
        name="fcum",
    )(small3, fb_row, u)


def _fox_kernel(q_ref, k_ref, v_ref, nf_ref, qg_ref, kg_ref, bd_ref, o_ref,
                kaug_scr, vt_scr, qt_scr, *, bounded):
    p = pl.program_id(1)
    bd = bd_ref[...]
    seq = k_ref.shape[1]
    nt = seq // BLK

    kaug_scr[:, 0:LANES] = _headnorm_pair(k_ref[0].astype(F32), bd, kg_ref[...]).astype(BF16)
    cb = [nf_ref[0, pl.ds(2 * p + e, 1), :] * L2E for e in range(2)]
    rows = []
    for e in range(2):
        rows.extend(_split3(cb[e]))
    feats = jnp.concatenate([_rows8(rows, seq), jnp.zeros((BOUND0 - 8, seq), F32),
                             _rows8([1.0] * 3, seq), jnp.zeros((LANES - BOUND0 - 8, seq), F32)],
                            axis=0)
    kaug_scr[:, LANES:KA] = feats.T.astype(BF16)
    vt = v_ref[0].astype(F32).T
    for e in range(2):
        _store_values_t(vt_scr.at[e], vt[e * HEAD_DIM:(e + 1) * HEAD_DIM], seq)

    qk_bound = _logit_bound(qg_ref, [kg_ref])
    qt = _headnorm_pair(q_ref[0].astype(F32), bd, qg_ref[...]).T
    for e in range(2):
        qt_scr[e] = _query_feats(qt, e, _rows8([0.0] * (3 * e) + [1.0] * 3, seq), None,
                                 qk_bound + cb[e] if bounded else None)

    vt_rows = [lambda ks, e=e: vt_scr[e, :, ks] for e in range(2)]
    acc = _flash(_causal_blocks(nt, kaug_scr, vt_rows, qt_scr), 2 * nt, HEAD_DIM, bounded)
    for qh in range(nt):
        ot = jnp.concatenate([acc[e * nt + qh][0:HEAD_DIM] / acc[e * nt + qh][HEAD_DIM:HEAD_DIM + 1]
                              for e in range(2)], axis=0)
        o_ref[0, qh * BLK:(qh + 1) * BLK, :] = ot.T.astype(o_ref.dtype)


def _fox(proj3, negf, qg, kg, bd, *, bounded):
    b, s, _ = proj3.shape
    npair = FOX_HEADS // 2
    const = lambda a: pl.BlockSpec(a.shape, lambda bi, p: (0,) * a.ndim)
    col = lambda c: pl.BlockSpec((1, s, LANES), lambda bi, p: (bi, 0, c + p))
    return pl.pallas_call(
        functools.partial(_fox_kernel, bounded=bounded),
        grid=(b, npair),
        in_specs=[col(COL_FOX_Q), col(COL_FOX_K), col(COL_FOX_V),
                  pl.BlockSpec((1, FOX_HEADS, s), lambda bi, p: (bi, 0, 0)),
                  const(qg), const(kg), const(bd)],
        out_specs=pl.BlockSpec((1, s, LANES), lambda bi, p: (bi, 0, p)),
        out_shape=jax.ShapeDtypeStruct((b, s, npair * LANES), BF16),
        scratch_shapes=[pltpu.VMEM((s, KA), BF16), pltpu.VMEM((2, HEAD_DIM + VAUG, s), BF16),
                        pltpu.VMEM((2, KA, s), BF16)],
        compiler_params=_cparams(("parallel", "parallel")),
        name="fox",
    )(proj3, proj3, proj3, negf, qg, kg, bd)


def _alibi_at_query(sl_rows, seq):
    pos = lax.broadcasted_iota(jnp.int32, (1, seq), 1).astype(F32)
    return (sl_rows[0:1] + sl_rows[2:3] + sl_rows[4:5]) * pos


def _diff_kernel(q_ref, k_ref, v_ref, kx_ref, sl_ref, lam_ref, qg_ref, kg_ref, sg_ref, bd_ref,
                 o_ref, kaug_scr, vt_scr, qt_scr, *, lam_init, bounded):
    bd = bd_ref[...]
    seq = k_ref.shape[1]
    nt = seq // BLK

    kaug_scr[:, 0:LANES] = _headnorm_pair(k_ref[0].astype(F32), bd, kg_ref[...]).astype(BF16)
    kaug_scr[:, LANES:KA] = kx_ref[...]
    _store_values_t(vt_scr, v_ref[0].astype(F32).T, seq)
    sl = sl_ref[0]
    bound = _logit_bound(qg_ref, [kg_ref]) + _alibi_at_query(sl, seq) if bounded else None
    qt = _headnorm_pair(q_ref[0].astype(F32), bd, qg_ref[...]).T
    for e in range(2):
        qt_scr[e] = _query_feats(qt, e, sl, None, bound)

    vt_rows = [lambda ks: vt_scr[:, ks]] * 2
    acc = _flash(_causal_blocks(nt, kaug_scr, vt_rows, qt_scr), 2 * nt, LANES, bounded)

    lv = lam_ref[...]
    lam = (jnp.exp(jnp.sum(lv[0:1] * lv[1:2], axis=-1, keepdims=True))
           - jnp.exp(jnp.sum(lv[2:3] * lv[3:4], axis=-1, keepdims=True)) + lam_init)
    for qh in range(nt):
        a0, a1 = acc[qh], acc[nt + qh]
        ob = (a0[0:LANES] / a0[LANES:LANES + 1] - lam * (a1[0:LANES] / a1[LANES:LANES + 1])).T
        ob = ob * lax.rsqrt(jnp.mean(ob * ob, axis=-1, keepdims=True) + EPS) * sg_ref[...]
        o_ref[0, qh * BLK:(qh + 1) * BLK, :] = (ob * (1.0 - lam_init)).astype(o_ref.dtype)


def _diff(proj3, kx, slrows, lam_p, qg, kg, sg, bd, lam_init, *, bounded):
    b, s, _ = proj3.shape
    const = lambda a: pl.BlockSpec(a.shape, lambda bi, h: (0,) * a.ndim)
    col = lambda c: pl.BlockSpec((1, s, LANES), lambda bi, h: (bi, 0, c + h))
    return pl.pallas_call(
        functools.partial(_diff_kernel, lam_init=lam_init, bounded=bounded),
        grid=(b, DIFF_HEADS),
        in_specs=[col(COL_DIFF_Q), col(COL_DIFF_K), col(COL_DIFF_V), const(kx),
                  pl.BlockSpec((1, 8, s), lambda bi, h: (h, 0, 0)),
                  const(lam_p), const(qg), const(kg), const(sg), const(bd)],
        out_specs=pl.BlockSpec((1, s, LANES), lambda bi, h: (bi, 0, h)),
        out_shape=jax.ShapeDtypeStruct((b, s, DIFF_HEADS * LANES), BF16),
        scratch_shapes=[pltpu.VMEM((s, KA), BF16), pltpu.VMEM((LANES + VAUG, s), BF16),
                        pltpu.VMEM((2, KA, s), BF16)],
        compiler_params=_cparams(("parallel", "parallel")),
        name="diff",
    )(proj3, proj3, proj3, kx, slrows, lam_p, qg, kg, sg, bd)


def _nsa_select(imp_t, t0):
    ns, tq = imp_t.shape
    blk = lax.broadcasted_iota(jnp.int32, (ns, tq), 0)
    cur = (t0 + lax.broadcasted_iota(jnp.int32, (ns, tq), 1)) // SLC_BLOCK
    forced = (blk == 0) | (blk == cur) | (blk == cur - 1)
    score = jnp.where(blk <= cur, jnp.where(forced, imp_t + FORCE_BONUS, imp_t), NEG)
    sel = jnp.zeros((ns, tq), F32)
    blk_f = blk.astype(F32)
    for _ in range(min(SLC_TOPN, ns)):
        best = jnp.max(score, axis=0, keepdims=True)
        first = jnp.min(jnp.where(score == best, blk_f, float(ns)), axis=0, keepdims=True)
        take = blk_f == first
        sel = jnp.where(take, 1.0, sel)
        score = jnp.where(take, 2.0 * NEG, score)
    return sel


def _slope_rows(slope):
    hi, mid, lo = [float(np.float32(v)) for v in _np_split3(slope * L2E)]
    return [hi, hi, mid, mid, lo, lo]


def _np_split3(x):
    x = np.float32(x)
    hi = np.float32(x.astype(BF16))
    mid = np.float32((x - hi).astype(BF16))
    lo = np.float32((x - hi - mid).astype(BF16))
    return hi, mid, lo


CMP_ROWS = 512


def _nsa_prepare(qall_ref, kc_ref, vc_ref, ks_ref, vs_ref, kw_ref, vw_ref, kx_ref, cx_ref, nsl_ref,
                 qg_ref, ksg_ref, kwg_ref, bd_ref, ot_ref, ksaug_scr, kwaug_scr, vst_scr, vwt_scr,
                 notsel_scr, ocmp_scr, kcaug_scr, *, nc):
    bd = bd_ref[...]
    npair = NSA_HEADS // 2
    seq = ks_ref.shape[1]
    ksaug_scr[:, 0:LANES] = _headnorm_pair(ks_ref[0].astype(F32), bd, ksg_ref[...]).astype(BF16)
    kwaug_scr[:, 0:LANES] = _headnorm_pair(kw_ref[0].astype(F32), bd, kwg_ref[...]).astype(BF16)
    ksaug_scr[:, LANES:KA] = kx_ref[...]
    kwaug_scr[:, LANES:KA] = kx_ref[...]
    for v_ref, vt_scr in ((vs_ref, vst_scr), (vw_ref, vwt_scr)):
        vt = v_ref[0].astype(F32).T
        for e in range(2):
            _store_values_t(vt_scr.at[e], vt[e * HEAD_DIM:(e + 1) * HEAD_DIM], seq)
    kcaug_scr[:, 0:LANES] = kc_ref[0]
    kcaug_scr[:, LANES:KA] = cx_ref[...]
    vct = vc_ref[0].astype(F32).T.astype(BF16)

    nidx = lax.broadcasted_iota(jnp.int32, (LANES, CMP_ROWS), 0)
    cend = nidx * CMP_STRIDE + (CMP_LEN - 1)

    def chunk(c, carry):
        t0 = pl.multiple_of(c * CMP_ROWS, CMP_ROWS)
        tq = t0 + lax.broadcasted_iota(jnp.int32, (LANES, CMP_ROWS), 1)
        cmask = (tq >= cend) & (nidx < nc)
        psum = [jnp.zeros((LANES, CMP_ROWS), F32), jnp.zeros((LANES, CMP_ROWS), F32)]
        scores = []
        for r in range(npair):
            qt = _headnorm_pair(qall_ref[0, pl.ds(t0, CMP_ROWS), r * LANES:(r + 1) * LANES].astype(F32),
                                bd, qg_ref[...]).T
            for e in range(2):
                qf = _query_feats(qt, e, nsl_ref[r, e, :, 0:CMP_ROWS], None, None)
                scores.append(jnp.dot(kcaug_scr[...], qf, preferred_element_type=F32))
        probs = []
        for k, s in enumerate(scores):
            s = jnp.where(cmask, s, NEG)
            m = jnp.max(s, axis=0, keepdims=True)
            pe = jnp.where(cmask, jnp.exp2(s - m), 0.0)
            l = jnp.sum(pe, axis=0, keepdims=True)
            pn = pe / jnp.where(l > 0.0, l, 1.0)
            psum[k % 2] = psum[k % 2] + pn
            probs.append(pn.astype(BF16))
        for k, p in enumerate(probs):
            hs = slice((k % 2) * HEAD_DIM, (k % 2 + 1) * HEAD_DIM)
            ocmp_scr[k // 2, hs, pl.ds(t0, CMP_ROWS)] = jnp.dot(vct[hs], p,
                                                                preferred_element_type=F32)
        for g in range(NSA_GROUPS):
            hi, mid, lo = _split3(psum[g])
            imp_t = (jnp.dot(ot_ref[...], hi.astype(BF16), preferred_element_type=F32)
                     + jnp.dot(ot_ref[...], mid.astype(BF16), preferred_element_type=F32)
                     + jnp.dot(ot_ref[...], lo.astype(BF16), preferred_element_type=F32))
            notsel_scr[g, :, pl.ds(t0, CMP_ROWS)] = (_nsa_select(imp_t, t0) - 1.0) * -NEG
        return carry

    lax.fori_loop(0, seq // CMP_ROWS, chunk, 0)


def _nsa_kernel(qall_ref, q_ref, sm_ref, kc_ref, vc_ref, ks_ref, vs_ref, kw_ref, vw_ref, kx_ref,
                cx_ref, nsl_ref, slr_ref, qg_ref, ksg_ref, kwg_ref, bd_ref, ot_ref, egt_ref, o_ref,
                ksaug_scr, kwaug_scr, vst_scr, vwt_scr, qts_scr, qtw_scr, notsel_scr, ocmp_scr,
                kcaug_scr, *, nc, bounded):
    r = pl.program_id(1)
    bd = bd_ref[...]
    seq = ks_ref.shape[1]
    nt = seq // BLK

    @pl.when(r == 0)
    def _():
        _nsa_prepare(qall_ref, kc_ref, vc_ref, ks_ref, vs_ref, kw_ref, vw_ref, kx_ref, cx_ref,
                     nsl_ref, qg_ref, ksg_ref, kwg_ref, bd_ref, ot_ref, ksaug_scr, kwaug_scr,
                     vst_scr, vwt_scr, notsel_scr, ocmp_scr, kcaug_scr, nc=nc)

    qk_bound = _logit_bound(qg_ref, [ksg_ref, kwg_ref])
    qt = _headnorm_pair(q_ref[0].astype(F32), bd, qg_ref[...]).T
    for e in range(2):
        sl = slr_ref[0, e]
        bound = qk_bound + _alibi_at_query(sl, seq) if bounded else None
        qts_scr[e] = _query_feats(qt, e, sl, notsel_scr[e], bound)
        qtw_scr[e] = _query_feats(qt, e, sl, None, bound)

    key = lax.broadcasted_iota(jnp.int32, (BLK, BLK), 0)
    qry = lax.broadcasted_iota(jnp.int32, (BLK, BLK), 1)
    diag = key <= qry
    above = key > qry
    vs_rows = [lambda ks, e=e: vst_scr[e, :, ks] for e in range(2)]
    slc = _causal_blocks(nt, ksaug_scr, vs_rows, qts_scr)

    def win_block(j, qh, e, mask):
        ks = slice(j * BLK, (j + 1) * BLK)
        qs = slice(qh * BLK, (qh + 1) * BLK)
        return (lambda: kwaug_scr[ks, :], lambda: vwt_scr[e, :, ks], lambda: qtw_scr[e, :, qs],
                mask, 2 * nt + e * nt + qh)

    blocks = []
    pos = 0
    for j in range(nt):
        n_j = 2 * (nt - j)
        blocks.extend(slc[pos:pos + n_j])
        pos += n_j
        blocks.extend(win_block(j, j, e, diag) for e in range(2))
        if j > 0:
            blocks.extend(win_block(j - 1, j, e, above) for e in range(2))
    acc = _flash(blocks, 4 * nt, HEAD_DIM, bounded)
    normed = lambda a: a[0:HEAD_DIM] / a[HEAD_DIM:HEAD_DIM + 1]

    gate_t = jax.nn.sigmoid(sm_ref[0]).T
    g_cmp_t, g_slc_t, g_win_t = [_dot01_left(egt_ref[c], gate_t) for c in range(3)]
    for qh in range(nt):
        cols = slice(qh * BLK, (qh + 1) * BLK)
        o_s = jnp.concatenate([normed(acc[e * nt + qh]) for e in range(2)], axis=0)
        o_w = jnp.concatenate([normed(acc[2 * nt + e * nt + qh]) for e in range(2)], axis=0)
        mix = (g_cmp_t[:, cols] * ocmp_scr[r, :, cols] + g_slc_t[:, cols] * o_s
               + g_win_t[:, cols] * o_w)
        o_ref[0, cols, :] = mix.T.astype(o_ref.dtype)


def _nsa(proj3, small3, kc, vc, kx, cx, nsl, qg, ksg, kwg, bd, ot, egt, *, bounded):
    b, s, _ = proj3.shape
    npair = NSA_HEADS // 2
    ns = s // SLC_BLOCK
    nc = (s - CMP_LEN) // CMP_STRIDE + 1
    const = lambda a: pl.BlockSpec(a.shape, lambda bi, r: (0,) * a.ndim)
    kvspec = lambda c: pl.BlockSpec((1, s, LANES), lambda bi, r: (bi, 0, c))
    return pl.pallas_call(
        functools.partial(_nsa_kernel, nc=nc, bounded=bounded),
        grid=(b, npair),
        in_specs=[
            pl.BlockSpec((1, s, npair * LANES), lambda bi, r: (bi, 0, COL_NSA_Q // npair)),
            pl.BlockSpec((1, s, LANES), lambda bi, r: (bi, 0, COL_NSA_Q + r)),
            pl.BlockSpec((1, s, LANES), lambda bi, r: (bi, 0, 0)),
            pl.BlockSpec((1,) + kc.shape[1:], lambda bi, r: (bi, 0, 0)),
            pl.BlockSpec((1,) + vc.shape[1:], lambda bi, r: (bi, 0, 0)),
            kvspec(COL_K_SLC), kvspec(COL_V_SLC), kvspec(COL_K_WIN), kvspec(COL_V_WIN),
            const(kx), const(cx), const(nsl),
            pl.BlockSpec((1, 2, 8, s), lambda bi, r: (r, 0, 0, 0)),
            const(qg), const(ksg), const(kwg), const(bd), const(ot),
            pl.BlockSpec((3, LANES, LANES), lambda bi, r: (0, r, 0)),
        ],
        out_specs=pl.BlockSpec((1, s, LANES), lambda bi, r: (bi, 0, r)),
        out_shape=jax.ShapeDtypeStruct((b, s, npair * LANES), BF16),
        scratch_shapes=[pltpu.VMEM((s, KA), BF16), pltpu.VMEM((s, KA), BF16),
                        pltpu.VMEM((2, HEAD_DIM + VAUG, s), BF16),
                        pltpu.VMEM((2, HEAD_DIM + VAUG, s), BF16),
                        pltpu.VMEM((2, KA, s), BF16), pltpu.VMEM((2, KA, s), BF16),
                        pltpu.VMEM((NSA_GROUPS, ns, s), F32), pltpu.VMEM((npair, LANES, s), F32),
                        pltpu.VMEM((kc.shape[1], KA), BF16)],
        compiler_params=_cparams(("parallel", "arbitrary")),
        name="nsa",
    )(proj3, proj3, small3, kc, vc, proj3, proj3, proj3, proj3, kx, cx, nsl, nsl, qg, ksg, kwg, bd,
      ot, egt)


def _merge_kernel(oa_ref, ob_ref, oc_ref, g0_ref, g1_ref, g2_ref, h_ref, wbr_ref, wo_ref, o_ref):
    merged = None
    for c, (o_r, g_r) in enumerate(((oa_ref, g0_ref), (ob_ref, g1_ref), (oc_ref, g2_ref))):
        y = jnp.dot(o_r[...], wbr_ref[c], preferred_element_type=F32)
        term = jax.nn.sigmoid(g_r[...].astype(F32)) * y
        merged = term if merged is None else merged + term
    o_ref[...] = h_ref[...] + jnp.dot(merged.astype(BF16), wo_ref[...], preferred_element_type=F32)


def _merge(oa, ob, oc, proj2, h2, wbr, wo, *, tm=512):
    t, d = h2.shape
    bw = oa.shape[1]
    row = lambda w: pl.BlockSpec((tm, w), lambda i: (i, 0))
    gate = lambda c: pl.BlockSpec((tm, d), lambda i: (i, COL_MG * LANES // d + c))
    return pl.pallas_call(
        _merge_kernel,
        grid=(t // tm,),
        in_specs=[row(bw), row(bw), row(bw), gate(0), gate(1), gate(2), row(d),
                  pl.BlockSpec(wbr.shape, lambda i: (0, 0, 0)),
                  pl.BlockSpec(wo.shape, lambda i: (0, 0))],
        out_specs=row(d),
        out_shape=jax.ShapeDtypeStruct((t, d), F32),
        compiler_params=_cparams(("parallel",)),
        name="merge",
    )(oa, ob, oc, proj2, proj2, proj2, h2, wbr, wo)


HALO = 16


def _ffn_kernel(x_ref, xh_ref, g_ref, wup_ref, cw_ref, cb_ref, wd_ref, o_ref, a_scr, u_scr,
                *, tm, tf, tiles_per_seq):
    i = pl.program_id(0)
    dff = wd_ref.shape[0]

    def norm(x):
        return (x * lax.rsqrt(jnp.mean(x * x, axis=-1, keepdims=True) + EPS)
                * g_ref[...]).astype(BF16)

    a_scr[0:HALO, :] = norm(xh_ref[...])
    a_scr[HALO:, :] = norm(x_ref[...])
    seq_start = (i % tiles_per_seq) == 0
    acts = []
    for c in range(dff // tf):
        cols = slice(c * tf, (c + 1) * tf)
        u = jnp.dot(a_scr[...], wup_ref[:, cols], preferred_element_type=F32)
        rows = lax.broadcasted_iota(jnp.int32, u.shape, 0)
        u_scr[...] = jnp.where((rows < HALO) & seq_start, 0.0, u)
        uc = cb_ref[:, cols]
        for t in range(CONV_W):
            uc = uc + cw_ref[t:t + 1, cols] * u_scr[pl.ds(HALO - (CONV_W - 1) + t, tm), :]
        gt = jnp.dot(a_scr[HALO:, :], wup_ref[:, dff + c * tf:dff + (c + 1) * tf],
                     preferred_element_type=F32)
        acts.append((jax.nn.gelu(uc) * gt).astype(BF16))
    act = jnp.concatenate(acts, axis=1)
    o_ref[...] = x_ref[...] + jnp.dot(act, wd_ref[...], preferred_element_type=F32)


def _ffn(h2, g, wup, cw, cb, wd, seq, *, tm=512, tf=1408):
    t, d = h2.shape
    resident = lambda a: pl.BlockSpec(a.shape, lambda i: (0,) * a.ndim, pipeline_mode=pl.Buffered(1))
    return pl.pallas_call(
        functools.partial(_ffn_kernel, tm=tm, tf=tf, tiles_per_seq=seq // tm),
        grid=(t // tm,),
        in_specs=[
            pl.BlockSpec((tm, d), lambda i: (i, 0)),
            pl.BlockSpec((HALO, d), lambda i: (jnp.maximum(i * (tm // HALO) - 1, 0), 0)),
            resident(g), resident(wup), resident(cw), resident(cb), resident(wd),
        ],
        out_specs=pl.BlockSpec((tm, d), lambda i: (i, 0)),
        out_shape=jax.ShapeDtypeStruct((t, d), F32),
        scratch_shapes=[pltpu.VMEM((HALO + tm, d), BF16), pltpu.VMEM((HALO + tm, tf), F32)],
        compiler_params=_cparams(("parallel",)),
        name="ffn",
    )(h2, h2, g, wup, cw, cb, wd)


def _nsa_head_order():
    hpg = NSA_HEADS // NSA_GROUPS
    return [h for r in range(hpg) for h in (r, hpg + r)]


def _constants(seq):
    ns = seq // SLC_BLOCK
    nc = (seq - CMP_LEN) // CMP_STRIDE + 1
    assert MASK0 + ns <= LANES and MASK0 >= 8 and N_BIAS <= 8
    bd = np.kron(np.eye(2), np.ones((HEAD_DIM, HEAD_DIM))).astype(np.float32)
    c_start = np.arange(LANES) * CMP_STRIDE
    s_start = np.arange(ns) * SLC_BLOCK
    ot = ((c_start[None, :] < s_start[:, None] + SLC_BLOCK)
          & (c_start[None, :] + CMP_LEN > s_start[:, None])
          & (np.arange(LANES)[None, :] < nc)).astype(np.float32)
    pos = np.arange(seq)
    kx = np.zeros((seq, LANES), np.float32)
    for k in range(N_BIAS):
        kx[:, k] = (pos // 8) * 8 if k % 2 == 0 else pos % 8
    kx[pos, MASK0 + pos // SLC_BLOCK] = 1.0
    kx[:, BOUND0:BOUND0 + 3] = 1.0
    cx = kx[np.minimum(np.arange(LANES) * CMP_STRIDE + (CMP_LEN - 1), seq - 1)]
    order = _nsa_head_order()
    eg = np.zeros((3, LANES, NSA_HEADS * HEAD_DIM), np.float32)
    for c in range(3):
        for slot, h in enumerate(order):
            eg[c, SMALL_GATE0 + c * NSA_HEADS + h, slot * HEAD_DIM:(slot + 1) * HEAD_DIM] = 1.0
    u = (np.arange(BLK)[:, None] <= np.arange(BLK)[None, :]).astype(np.float32)
    dsl = np.zeros((DIFF_HEADS, 8, seq), np.float32)
    for h in range(DIFF_HEADS):
        rows = _slope_rows(2.0 ** (-8.0 * (h + 1) / DIFF_HEADS))
        dsl[h, :N_BIAS, :] = np.asarray(rows, np.float32)[:, None]
    nsl = np.zeros((NSA_HEADS // 2, 2, 8, seq), np.float32)
    for slot, h in enumerate(order):
        rows = _slope_rows(2.0 ** (-8.0 * (h + 1) / NSA_HEADS))
        nsl[slot // 2, slot % 2, :N_BIAS, :] = np.asarray(rows, np.float32)[:, None]
    as_bf = lambda a: jnp.asarray(a, BF16)
    return dict(bd=as_bf(bd), ot=as_bf(ot), kx=as_bf(kx), cx=as_bf(cx),
                egt=as_bf(np.transpose(eg, (0, 2, 1))), u=as_bf(u), dsl=jnp.asarray(dsl, F32),
                nsl=jnp.asarray(nsl, F32))


def _pack_w_in(w):
    d = w.shape[0]
    hd = HEAD_DIM
    sizes = [512, 768, 24, 512, 512, 512, 512, 512, 512, 8, 3 * d]
    offs = np.concatenate([[0], np.cumsum(sizes)])
    nq, nkv, ngate, dq, dk, dv, fq, fk, fv, ff, mg = [w[:, offs[k]:offs[k + 1]] for k in range(11)]
    nq = jnp.concatenate([nq[:, h * hd:(h + 1) * hd] for h in _nsa_head_order()], axis=1)
    kv = lambda c, k: nkv[:, (c * 2 + k) * LANES:(c * 2 + k + 1) * LANES]
    cols = [mg, nq, dq, dk, dv, fq, fk, fv,
            kv(1, 0), kv(1, 1), kv(2, 0), kv(2, 1), kv(0, 0), kv(0, 1)]
    main = jnp.concatenate(cols, axis=1)
    main = jnp.pad(main, ((0, 0), (0, PROJ_UNITS * LANES - main.shape[1])))
    small = jnp.pad(jnp.concatenate([ngate, ff], axis=1), ((0, 0), (0, LANES - 32)))
    return main.astype(BF16), small.astype(BF16)


def kernel(x, attn_norm_g, w_in, nsa_q_g, nsa_k_g, cmp_pe, cmp_w1, cmp_w2, diff_q_g, diff_k_g,
           diff_lam, diff_subln_g, fox_q_g, fox_k_g, fox_b, w_br, w_o, ffn_norm_g, w_up, conv_w,
           conv_b, w_down):
    b, s, d = x.shape
    depth = w_in.shape[0]
    hd = HEAD_DIM
    qscale = hd ** -0.5 * L2E
    cst = _constants(s)
    rows16 = s // CMP_STRIDE
    assert rows16 == LANES and s % CMP_ROWS == 0, "NSA kernel keeps all compressed blocks in one 128-lane tile"
    order = _nsa_head_order()
    tile2 = lambda g: jnp.tile(g, 2).reshape(1, LANES).astype(F32)

    h = x.reshape(b * s, d)
    for l in range(depth):
        wmain, wsmall = _pack_w_in(w_in[l])
        proj, small = _proj(h, attn_norm_g[l].reshape(1, d), wmain, wsmall)
        proj3 = proj.reshape(b, s, PROJ_UNITS * LANES)
        small3 = small.reshape(b, s, LANES)

        w1 = cmp_w1[l].astype(BF16).reshape(2, CMP_LEN, hd, 2 * hd)
        z1 = jnp.zeros_like(w1)
        w1p = jnp.concatenate([jnp.concatenate([w1, z1], axis=3),
                               jnp.concatenate([z1, w1], axis=3)], axis=2)
        w2 = cmp_w2[l].astype(BF16)
        z2 = jnp.zeros_like(w2)
        w2p = jnp.concatenate([jnp.concatenate([w2, z2], axis=2),
                               jnp.concatenate([z2, w2], axis=2)], axis=1)
        pe8 = jnp.broadcast_to(cmp_pe[l].reshape(2, 1, CMP_LEN * hd), (2, 8, CMP_LEN * hd)).astype(BF16)
        kc, vc = _compress(proj3, w1p, pe8, cmp_w1[l].astype(BF16), w2p, tile2(nsa_k_g[l, 0]),
                           cst["bd"])

        fb_row = jnp.zeros((1, LANES), F32).at[0, SMALL_FF0:SMALL_FF0 + FOX_HEADS].set(fox_b[l])
        negf = _fcum(small3, fb_row, cst["u"])

        def mixer(call, qg, kgs, *args):
            span = 2.0 * HEAD_DIM * BOUND_SLACK * jnp.max(jnp.abs(qg)) * max_abs(kgs)
            return lax.cond(span <= BOUND_LIMIT, functools.partial(call, bounded=True),
                            functools.partial(call, bounded=False), *args)

        max_abs = lambda gs: functools.reduce(jnp.maximum, [jnp.max(jnp.abs(g)) for g in gs])
        nsa_qg, diff_qg, fox_qg = nsa_q_g[l] * qscale, diff_q_g[l] * qscale, fox_q_g[l] * qscale
        o_a = mixer(_nsa, nsa_qg, [nsa_k_g[l, 1], nsa_k_g[l, 2]],
                    proj3, small3, kc, vc, cst["kx"], cst["cx"], cst["nsl"], tile2(nsa_qg),
                    tile2(nsa_k_g[l, 1]), tile2(nsa_k_g[l, 2]), cst["bd"], cst["ot"], cst["egt"])
        lam_init = 0.8 - 0.6 * math.exp(-0.3 * l)
        o_b = mixer(functools.partial(_diff, lam_init=lam_init), diff_qg, [diff_k_g[l]],
                    proj3, cst["kx"], cst["dsl"], diff_lam[l], tile2(diff_qg), tile2(diff_k_g[l]),
                    diff_subln_g[l].reshape(1, LANES), cst["bd"])
        o_c = mixer(_fox, fox_qg, [fox_k_g[l]],
                    proj3, negf, tile2(fox_qg), tile2(fox_k_g[l]), cst["bd"])

        wbr = w_br[l]
        wbr_a = jnp.concatenate([wbr[0, hh * hd:(hh + 1) * hd] for hh in order], axis=0)
        wbr_p = jnp.stack([wbr_a, wbr[1], wbr[2]]).astype(BF16)
        bw = NSA_HEADS * hd
        h = _merge(o_a.reshape(b * s, bw), o_b.reshape(b * s, bw), o_c.reshape(b * s, bw),
                   proj, h, wbr_p, w_o[l].astype(BF16))
        h = _ffn(h, ffn_norm_g[l].reshape(1, d), w_up[l].astype(BF16), conv_w[l],
                 conv_b[l].reshape(1, -1), w_down[l].astype(BF16), s)
    return h.reshape(b, s, d)
```

```python
import functools
import math

import numpy as np
import jax
import jax.numpy as jnp
from jax import lax
from jax.experimental import pallas as pl
from jax.experimental.pallas import tpu as pltpu

F32 = jnp.float32
BF16 = jnp.bfloat16

HEAD_DIM = 64
NSA_HEADS = 8
NSA_GROUPS = 2
CMP_LEN = 32
CMP_STRIDE = 16
SLC_BLOCK = 64
SLC_TOPN = 8
WINDOW = 256
FORCE_BONUS = 1.0e4
DIFF_HEADS = 4
FOX_HEADS = 8
CONV_W = 3
EPS = 1e-6
NEG = -1.0e30
L2E = 1.4426950408889634

LANES = 128
BLK = 256
KA = 2 * LANES
N_BIAS = 6
MASK0 = 8
BOUND0 = 40
VAUG = 16
BOUND_SLACK = 1.02
BOUND_LIMIT = 100.0
VMEM_LIMIT = 56 * 1024 * 1024

COL_MG = 0
COL_NSA_Q = 24
COL_DIFF_Q, COL_DIFF_K, COL_DIFF_V = 28, 32, 36
COL_FOX_Q, COL_FOX_K, COL_FOX_V = 40, 44, 48
COL_K_SLC, COL_V_SLC, COL_K_WIN, COL_V_WIN, COL_K_CMP, COL_V_CMP = 52, 53, 54, 55, 56, 57
PROJ_UNITS = 60
SMALL_GATE0 = 0
SMALL_FF0 = 24


def _cparams(sem):
    return pltpu.CompilerParams(dimension_semantics=sem, vmem_limit_bytes=VMEM_LIMIT)


def _split3(x):
    hi = x.astype(BF16).astype(F32)
    r1 = x - hi
    mid = r1.astype(BF16).astype(F32)
    lo = (r1 - mid).astype(BF16).astype(F32)
    return hi, mid, lo


def _dot01(x, m):
    hi = x.astype(BF16)
    lo = (x - hi.astype(F32)).astype(BF16)
    return (jnp.dot(hi, m, preferred_element_type=F32)
            + jnp.dot(lo, m, preferred_element_type=F32))


def _dot01_left(m, x):
    hi = x.astype(BF16)
    lo = (x - hi.astype(F32)).astype(BF16)
    return (jnp.dot(m, hi, preferred_element_type=F32)
            + jnp.dot(m, lo, preferred_element_type=F32))


def _dot01_3(x, m):
    hi, mid, lo = _split3(x)
    return (jnp.dot(hi.astype(BF16), m, preferred_element_type=F32)
            + jnp.dot(mid.astype(BF16), m, preferred_element_type=F32)
            + jnp.dot(lo.astype(BF16), m, preferred_element_type=F32))


def _qk(q, k):
    return lax.dot_general(q, k, (((1,), (1,)), ((), ())), preferred_element_type=F32)


def _headnorm_pair(x, bd, gain):
    ss = jnp.dot((x * x).astype(BF16), bd, preferred_element_type=F32)
    return x * lax.rsqrt(ss * (1.0 / HEAD_DIM) + EPS) * gain


def _rows8(vals, width):
    row = lax.broadcasted_iota(jnp.int32, (8, width), 0)
    out = jnp.zeros((8, width), F32)
    for k, v in enumerate(vals):
        out = jnp.where(row == k, v, out)
    return out


def _query_feats(qt, half, bias8, mask_rows, bound_row):
    tq = qt.shape[1]
    row = lax.broadcasted_iota(jnp.int32, (LANES, tq), 0)
    keep = (row < HEAD_DIM) if half == 0 else (row >= HEAD_DIM)
    nmask = BOUND0 - MASK0
    parts = [jnp.where(keep, qt, 0.0), bias8,
             mask_rows if mask_rows is not None else jnp.zeros((nmask, tq), F32),
             _rows8(_split3(-bound_row), tq) if bound_row is not None else jnp.zeros((8, tq), F32),
             jnp.zeros((LANES - BOUND0 - 8, tq), F32)]
    return jnp.concatenate(parts, axis=0).astype(BF16)


def _logit_bound(qg_ref, kg_refs):
    kmax = None
    for kg_ref in kg_refs:
        k = jnp.max(jnp.abs(kg_ref[...]), axis=-1, keepdims=True)
        kmax = k if kmax is None else jnp.maximum(kmax, k)
    return jnp.max(jnp.abs(qg_ref[...]), axis=-1, keepdims=True) * kmax * (HEAD_DIM * BOUND_SLACK)


def _store_values_t(vt_view, rows, seq):
    dv = rows.shape[0]
    vt_view[0:dv, :] = rows.astype(BF16)
    vt_view[dv:dv + 8, :] = _rows8([1.0], seq).astype(BF16)
    vt_view[dv + 8:dv + VAUG, :] = jnp.zeros((VAUG - 8, seq), BF16)


QK_AHEAD = 2


def _flash(blocks, n_chain, dv, bounded):
    acc = [jnp.zeros((dv + VAUG, BLK), F32) for _ in range(n_chain)]
    mx = [jnp.full((1, BLK), NEG, F32) for _ in range(n_chain)]
    scores = {}

    def issue(k):
        if k < len(blocks):
            scores[k] = jnp.dot(blocks[k][0](), blocks[k][2](), preferred_element_type=F32)

    for k in range(2 * QK_AHEAD):
        issue(k)
    for k0 in range(0, len(blocks), 2):
        issue(k0 + 2 * QK_AHEAD)
        issue(k0 + 2 * QK_AHEAD + 1)
        pending = []
        for k in range(k0, min(k0 + 2, len(blocks))):
            _, vt, _, mask, chain = blocks[k]
            s = scores.pop(k)
            if mask is not None:
                s = jnp.where(mask, s, NEG)
            if bounded:
                pending.append((vt, chain, None, jnp.exp2(s).astype(BF16)))
            else:
                m_new = jnp.maximum(mx[chain], jnp.max(s, axis=0, keepdims=True))
                alpha = jnp.exp2(mx[chain] - m_new)
                mx[chain] = m_new
                pending.append((vt, chain, alpha, jnp.exp2(s - m_new).astype(BF16)))
        assert len({c for _, c, _, _ in pending}) == len(pending), "a pair must not share a chain"
        for vt, chain, alpha, p in pending:
            prev = acc[chain] if alpha is None else alpha * acc[chain]
            acc[chain] = prev + jnp.dot(vt(), p, preferred_element_type=F32)
    return acc


def _causal_blocks(nt, kaug_scr, vt_rows, qt_scr):
    key = lax.broadcasted_iota(jnp.int32, (BLK, BLK), 0)
    qry = lax.broadcasted_iota(jnp.int32, (BLK, BLK), 1)
    diag = key <= qry
    blocks = []
    for j in range(nt):
        ks = slice(j * BLK, (j + 1) * BLK)
        for qh in range(j, nt):
            qs = slice(qh * BLK, (qh + 1) * BLK)
            for e in range(2):
                blocks.append((lambda ks=ks: kaug_scr[ks, :],
                               lambda e=e, ks=ks: vt_rows[e](ks),
                               lambda e=e, qs=qs: qt_scr[e, :, qs],
                               diag if qh == j else None, e * nt + qh))
    return blocks


def _proj_kernel(x_ref, g_ref, w_ref, ws_ref, o_ref, os_ref, a_scr):
    j = pl.program_id(1)

    @pl.when(j == 0)
    def _():
        x = x_ref[...]
        inv = lax.rsqrt(jnp.mean(x * x, axis=-1, keepdims=True) + EPS)
        a = (x * inv * g_ref[...]).astype(BF16)
        a_scr[...] = a
        os_ref[...] = jnp.dot(a, ws_ref[...], preferred_element_type=F32)

    o_ref[...] = jnp.dot(a_scr[...], w_ref[...], preferred_element_type=F32).astype(o_ref.dtype)


def _proj(x2d, g, w, ws, *, tm=1024, tn=2560):
    t, d = x2d.shape
    n = w.shape[1]
    return pl.pallas_call(
        _proj_kernel,
        grid=(t // tm, n // tn),
        in_specs=[
            pl.BlockSpec((tm, d), lambda i, j: (i, 0)),
            pl.BlockSpec((1, d), lambda i, j: (0, 0)),
            pl.BlockSpec((d, tn), lambda i, j: (0, j)),
            pl.BlockSpec((d, LANES), lambda i, j: (0, 0)),
        ],
        out_specs=[
            pl.BlockSpec((tm, tn), lambda i, j: (i, j)),
            pl.BlockSpec((tm, LANES), lambda i, j: (i, 0)),
        ],
        out_shape=[jax.ShapeDtypeStruct((t, n), BF16), jax.ShapeDtypeStruct((t, LANES), F32)],
        scratch_shapes=[pltpu.VMEM((tm, d), BF16)],
        compiler_params=_cparams(("parallel", "arbitrary")),
        name="proj",
    )(x2d, g, w, ws)


def _compress_kernel(k_ref, v_ref, w1p_ref, pe_ref, w1f_ref, w2p_ref, kg_ref, bd_ref,
                     kc_ref, vc_ref, raw_scr):
    seq = k_ref.shape[1]
    nrow = kc_ref.shape[1]
    raw_scr[seq:, :] = jnp.zeros((raw_scr.shape[0] - seq, LANES), F32)
    for kv, (src, dst) in enumerate(((k_ref, kc_ref), (v_ref, vc_ref))):
        raw_scr[0:seq, :] = src[0].astype(F32)
        c1 = jnp.dot(pe_ref[kv], w1f_ref[kv], preferred_element_type=F32)[0:1]
        pre = jnp.concatenate([c1, c1], axis=1)
        for l in range(CMP_LEN):
            rows = raw_scr[pl.ds(l, nrow, stride=CMP_STRIDE), :].astype(BF16)
            pre = pre + jnp.dot(rows, w1p_ref[kv, l], preferred_element_type=F32)
        hid = jax.nn.gelu(pre).astype(BF16)
        o = jnp.dot(hid, w2p_ref[kv], preferred_element_type=F32)
        if kv == 0:
            o = _headnorm_pair(o, bd_ref[...], kg_ref[...])
        dst[0] = o.astype(BF16)


def _compress(proj3, w1p, pe8, w1f, w2p, kg2, bd):
    b, s, _ = proj3.shape
    nrow = s // CMP_STRIDE
    full = lambda a: pl.BlockSpec(a.shape, lambda i: (0,) * a.ndim)
    return pl.pallas_call(
        _compress_kernel,
        grid=(b,),
        in_specs=[pl.BlockSpec((1, s, LANES), lambda i: (i, 0, COL_K_CMP)),
                  pl.BlockSpec((1, s, LANES), lambda i: (i, 0, COL_V_CMP)),
                  full(w1p), full(pe8), full(w1f), full(w2p), full(kg2), full(bd)],
        out_specs=[pl.BlockSpec((1, nrow, LANES), lambda i: (i, 0, 0)),
                   pl.BlockSpec((1, nrow, LANES), lambda i: (i, 0, 0))],
        out_shape=[jax.ShapeDtypeStruct((b, nrow, LANES), BF16),
                   jax.ShapeDtypeStruct((b, nrow, LANES), BF16)],
        scratch_shapes=[pltpu.VMEM((s + CMP_LEN, LANES), F32)],
        compiler_params=_cparams(("parallel",)),
        name="compress",
    )(proj3, proj3, w1p, pe8, w1f, w2p, kg2, bd)


def _fcum_kernel(s_ref, fb_ref, u_ref, o_ref, *, chunk):
    z = s_ref[0] + fb_ref[...]
    lf = jax.nn.log_sigmoid(z)
    lft = lf.T[SMALL_FF0:SMALL_FF0 + FOX_HEADS]
    seq = lft.shape[1]
    carry = jnp.zeros((FOX_HEADS, 1), F32)
    for c in range(seq // chunk):
        fc = _dot01_3(lft[:, c * chunk:(c + 1) * chunk], u_ref[...]) + carry
        o_ref[0, :, c * chunk:(c + 1) * chunk] = -fc
        carry = fc[:, chunk - 1:chunk]


def _fcum(small3, fb_row, u):
    b, s, _ = small3.shape
    chunk = u.shape[0]
    return pl.pallas_call(
        functools.partial(_fcum_kernel, chunk=chunk),
        grid=(b,),
        in_specs=[pl.BlockSpec((1, s, LANES), lambda i: (i, 0, 0)),
                  pl.BlockSpec((1, LANES), lambda i: (0, 0)),
                  pl.BlockSpec(u.shape, lambda i: (0, 0))],
        out_specs=pl.BlockSpec((1, FOX_HEADS, s), lambda i: (i, 0, 0)),
        out_shape=jax.ShapeDtypeStruct((b, FOX_HEADS, s), F32),
        compiler_params=_cparams(("parallel",)),
        name="fcum",
    )(small3, fb_row, u)


def _fox_kernel(q_ref, k_ref, v_ref, nf_ref, qg_ref, kg_ref, bd_ref, o_ref,
                kaug_scr, vt_scr, qt_scr, *, bounded):
    p = pl.program_id(1)
    bd = bd_ref[...]
    seq = k_ref.shape[1]
    nt = seq // BLK

    kaug_scr[:, 0:LANES] = _headnorm_pair(k_ref[0].astype(F32), bd, kg_ref[...]).astype(BF16)
    cb = [nf_ref[0, pl.ds(2 * p + e, 1), :] * L2E for e in range(2)]
    rows = []
    for e in range(2):
        rows.extend(_split3(cb[e]))
    feats = jnp.concatenate([_rows8(rows, seq), jnp.zeros((BOUND0 - 8, seq), F32),
                             _rows8([1.0] * 3, seq), jnp.zeros((LANES - BOUND0 - 8, seq), F32)],
                            axis=0)
    kaug_scr[:, LANES:KA] = feats.T.astype(BF16)
    vt = v_ref[0].astype(F32).T
    for e in range(2):
        _store_values_t(vt_scr.at[e], vt[e * HEAD_DIM:(e + 1) * HEAD_DIM], seq)

    qk_bound = _logit_bound(qg_ref, [kg_ref])
    qt = _headnorm_pair(q_ref[0].astype(F32), bd, qg_ref[...]).T
    for e in range(2):
        qt_scr[e] = _query_feats(qt, e, _rows8([0.0] * (3 * e) + [1.0] * 3, seq), None,
                                 qk_bound + cb[e] if bounded else None)

    vt_rows = [lambda ks, e=e: vt_scr[e, :, ks] for e in range(2)]
    acc = _flash(_causal_blocks(nt, kaug_scr, vt_rows, qt_scr), 2 * nt, HEAD_DIM, bounded)
    for qh in range(nt):
        ot = jnp.concatenate([acc[e * nt + qh][0:HEAD_DIM] / acc[e * nt + qh][HEAD_DIM:HEAD_DIM + 1]
                              for e in range(2)], axis=0)
        o_ref[0, qh * BLK:(qh + 1) * BLK, :] = ot.T.astype(o_ref.dtype)


def _fox(proj3, negf, qg, kg, bd, *, bounded):
    b, s, _ = proj3.shape
    npair = FOX_HEADS // 2
    const = lambda a: pl.BlockSpec(a.shape, lambda bi, p: (0,) * a.ndim)
    col = lambda c: pl.BlockSpec((1, s, LANES), lambda bi, p: (bi, 0, c + p))
    return pl.pallas_call(
        functools.partial(_fox_kernel, bounded=bounded),
        grid=(b, npair),
        in_specs=[col(COL_FOX_Q), col(COL_FOX_K), col(COL_FOX_V),
                  pl.BlockSpec((1, FOX_HEADS, s), lambda bi, p: (bi, 0, 0)),
                  const(qg), const(kg), const(bd)],
        out_specs=pl.BlockSpec((1, s, LANES), lambda bi, p: (bi, 0, p)),
        out_shape=jax.ShapeDtypeStruct((b, s, npair * LANES), BF16),
        scratch_shapes=[pltpu.VMEM((s, KA), BF16), pltpu.VMEM((2, HEAD_DIM + VAUG, s), BF16),
                        pltpu.VMEM((2, KA, s), BF16)],
        compiler_params=_cparams(("parallel", "parallel")),
        name="fox",
    )(proj3, proj3, proj3, negf, qg, kg, bd)


def _alibi_at_query(sl_rows, seq):
    pos = lax.broadcasted_iota(jnp.int32, (1, seq), 1).astype(F32)
    return (sl_rows[0:1] + sl_rows[2:3] + sl_rows[4:5]) * pos


def _diff_kernel(q_ref, k_ref, v_ref, kx_ref, sl_ref, lam_ref, qg_ref, kg_ref, sg_ref, bd_ref,
                 o_ref, kaug_scr, vt_scr, qt_scr, *, lam_init, bounded):
    bd = bd_ref[...]
    seq = k_ref.shape[1]
    nt = seq // BLK

    kaug_scr[:, 0:LANES] = _headnorm_pair(k_ref[0].astype(F32), bd, kg_ref[...]).astype(BF16)
    kaug_scr[:, LANES:KA] = kx_ref[...]
    _store_values_t(vt_scr, v_ref[0].astype(F32).T, seq)
    sl = sl_ref[0]
    bound = _logit_bound(qg_ref, [kg_ref]) + _alibi_at_query(sl, seq) if bounded else None
    qt = _headnorm_pair(q_ref[0].astype(F32), bd, qg_ref[...]).T
    for e in range(2):
        qt_scr[e] = _query_feats(qt, e, sl, None, bound)

    vt_rows = [lambda ks: vt_scr[:, ks]] * 2
    acc = _flash(_causal_blocks(nt, kaug_scr, vt_rows, qt_scr), 2 * nt, LANES, bounded)

    lv = lam_ref[...]
    lam = (jnp.exp(jnp.sum(lv[0:1] * lv[1:2], axis=-1, keepdims=True))
           - jnp.exp(jnp.sum(lv[2:3] * lv[3:4], axis=-1, keepdims=True)) + lam_init)
    for qh in range(nt):
        a0, a1 = acc[qh], acc[nt + qh]
        ob = (a0[0:LANES] / a0[LANES:LANES + 1] - lam * (a1[0:LANES] / a1[LANES:LANES + 1])).T
        ob = ob * lax.rsqrt(jnp.mean(ob * ob, axis=-1, keepdims=True) + EPS) * sg_ref[...]
        o_ref[0, qh * BLK:(qh + 1) * BLK, :] = (ob * (1.0 - lam_init)).astype(o_ref.dtype)


def _diff(proj3, kx, slrows, lam_p, qg, kg, sg, bd, lam_init, *, bounded):
    b, s, _ = proj3.shape
    const = lambda a: pl.BlockSpec(a.shape, lambda bi, h: (0,) * a.ndim)
    col = lambda c: pl.BlockSpec((1, s, LANES), lambda bi, h: (bi, 0, c + h))
    return pl.pallas_call(
        functools.partial(_diff_kernel, lam_init=lam_init, bounded=bounded),
        grid=(b, DIFF_HEADS),
        in_specs=[col(COL_DIFF_Q), col(COL_DIFF_K), col(COL_DIFF_V), const(kx),
                  pl.BlockSpec((1, 8, s), lambda bi, h: (h, 0, 0)),
                  const(lam_p), const(qg), const(kg), const(sg), const(bd)],
        out_specs=pl.BlockSpec((1, s, LANES), lambda bi, h: (bi, 0, h)),
        out_shape=jax.ShapeDtypeStruct((b, s, DIFF_HEADS * LANES), BF16),
        scratch_shapes=[pltpu.VMEM((s, KA), BF16), pltpu.VMEM((LANES + VAUG, s), BF16),
                        pltpu.VMEM((2, KA, s), BF16)],
        compiler_params=_cparams(("parallel", "parallel")),
        name="diff",
    )(proj3, proj3, proj3, kx, slrows, lam_p, qg, kg, sg, bd)


def _nsa_select(imp_t, t0):
    ns, tq = imp_t.shape
    blk = lax.broadcasted_iota(jnp.int32, (ns, tq), 0)
    cur = (t0 + lax.broadcasted_iota(jnp.int32, (ns, tq), 1)) // SLC_BLOCK
    forced = (blk == 0) | (blk == cur) | (blk == cur - 1)
    score = jnp.where(blk <= cur, jnp.where(forced, imp_t + FORCE_BONUS, imp_t), NEG)
    sel = jnp.zeros((ns, tq), F32)
    blk_f = blk.astype(F32)
    for _ in range(min(SLC_TOPN, ns)):
        best = jnp.max(score, axis=0, keepdims=True)
        first = jnp.min(jnp.where(score == best, blk_f, float(ns)), axis=0, keepdims=True)
        take = blk_f == first
        sel = jnp.where(take, 1.0, sel)
        score = jnp.where(take, 2.0 * NEG, score)
    return sel


def _slope_rows(slope):
    hi, mid, lo = [float(np.float32(v)) for v in _np_split3(slope * L2E)]
    return [hi, hi, mid, mid, lo, lo]


def _np_split3(x):
    x = np.float32(x)
    hi = np.float32(x.astype(BF16))
    mid = np.float32((x - hi).astype(BF16))
    lo = np.float32((x - hi - mid).astype(BF16))
    return hi, mid, lo


CMP_ROWS = 512


def _nsa_prepare(qall_ref, kc_ref, vc_ref, ks_ref, vs_ref, kw_ref, vw_ref, kx_ref, cx_ref, nsl_ref,
                 qg_ref, ksg_ref, kwg_ref, bd_ref, ot_ref, ksaug_scr, kwaug_scr, vst_scr, vwt_scr,
                 notsel_scr, ocmp_scr, kcaug_scr, *, nc):
    bd = bd_ref[...]
    npair = NSA_HEADS // 2
    seq = ks_ref.shape[1]
    ksaug_scr[:, 0:LANES] = _headnorm_pair(ks_ref[0].astype(F32), bd, ksg_ref[...]).astype(BF16)
    kwaug_scr[:, 0:LANES] = _headnorm_pair(kw_ref[0].astype(F32), bd, kwg_ref[...]).astype(BF16)
    ksaug_scr[:, LANES:KA] = kx_ref[...]
    kwaug_scr[:, LANES:KA] = kx_ref[...]
    for v_ref, vt_scr in ((vs_ref, vst_scr), (vw_ref, vwt_scr)):
        vt = v_ref[0].astype(F32).T
        for e in range(2):
            _store_values_t(vt_scr.at[e], vt[e * HEAD_DIM:(e + 1) * HEAD_DIM], seq)
    kcaug_scr[:, 0:LANES] = kc_ref[0]
    kcaug_scr[:, LANES:KA] = cx_ref[...]
    vct = vc_ref[0].astype(F32).T.astype(BF16)

    nidx = lax.broadcasted_iota(jnp.int32, (LANES, CMP_ROWS), 0)
    cend = nidx * CMP_STRIDE + (CMP_LEN - 1)

    def chunk(c, carry):
        t0 = pl.multiple_of(c * CMP_ROWS, CMP_ROWS)
        tq = t0 + lax.broadcasted_iota(jnp.int32, (LANES, CMP_ROWS), 1)
        cmask = (tq >= cend) & (nidx < nc)
        psum = [jnp.zeros((LANES, CMP_ROWS), F32), jnp.zeros((LANES, CMP_ROWS), F32)]
        scores = []
        for r in range(npair):
            qt = _headnorm_pair(qall_ref[0, pl.ds(t0, CMP_ROWS), r * LANES:(r + 1) * LANES].astype(F32),
                                bd, qg_ref[...]).T
            for e in range(2):
                qf = _query_feats(qt, e, nsl_ref[r, e, :, 0:CMP_ROWS], None, None)
                scores.append(jnp.dot(kcaug_scr[...], qf, preferred_element_type=F32))
        probs = []
        for k, s in enumerate(scores):
            s = jnp.where(cmask, s, NEG)
            m = jnp.max(s, axis=0, keepdims=True)
            pe = jnp.where(cmask, jnp.exp2(s - m), 0.0)
            l = jnp.sum(pe, axis=0, keepdims=True)
            pn = pe / jnp.where(l > 0.0, l, 1.0)
            psum[k % 2] = psum[k % 2] + pn
            probs.append(pn.astype(BF16))
        for k, p in enumerate(probs):
            hs = slice((k % 2) * HEAD_DIM, (k % 2 + 1) * HEAD_DIM)
            ocmp_scr[k // 2, hs, pl.ds(t0, CMP_ROWS)] = jnp.dot(vct[hs], p,
                                                                preferred_element_type=F32)
        for g in range(NSA_GROUPS):
            hi, mid, lo = _split3(psum[g])
            imp_t = (jnp.dot(ot_ref[...], hi.astype(BF16), preferred_element_type=F32)
                     + jnp.dot(ot_ref[...], mid.astype(BF16), preferred_element_type=F32)
                     + jnp.dot(ot_ref[...], lo.astype(BF16), preferred_element_type=F32))
            notsel_scr[g, :, pl.ds(t0, CMP_ROWS)] = (_nsa_select(imp_t, t0) - 1.0) * -NEG
        return carry

    lax.fori_loop(0, seq // CMP_ROWS, chunk, 0)


def _nsa_kernel(qall_ref, q_ref, sm_ref, kc_ref, vc_ref, ks_ref, vs_ref, kw_ref, vw_ref, kx_ref,
                cx_ref, nsl_ref, slr_ref, qg_ref, ksg_ref, kwg_ref, bd_ref, ot_ref, egt_ref, o_ref,
                ksaug_scr, kwaug_scr, vst_scr, vwt_scr, qts_scr, qtw_scr, notsel_scr, ocmp_scr,
                kcaug_scr, *, nc, bounded):
    r = pl.program_id(1)
    bd = bd_ref[...]
    seq = ks_ref.shape[1]
    nt = seq // BLK

    @pl.when(r == 0)
    def _():
        _nsa_prepare(qall_ref, kc_ref, vc_ref, ks_ref, vs_ref, kw_ref, vw_ref, kx_ref, cx_ref,
                     nsl_ref, qg_ref, ksg_ref, kwg_ref, bd_ref, ot_ref, ksaug_scr, kwaug_scr,
                     vst_scr, vwt_scr, notsel_scr, ocmp_scr, kcaug_scr, nc=nc)

    qk_bound = _logit_bound(qg_ref, [ksg_ref, kwg_ref])
    qt = _headnorm_pair(q_ref[0].astype(F32), bd, qg_ref[...]).T
    for e in range(2):
        sl = slr_ref[0, e]
        bound = qk_bound + _alibi_at_query(sl, seq) if bounded else None
        qts_scr[e] = _query_feats(qt, e, sl, notsel_scr[e], bound)
        qtw_scr[e] = _query_feats(qt, e, sl, None, bound)

    key = lax.broadcasted_iota(jnp.int32, (BLK, BLK), 0)
    qry = lax.broadcasted_iota(jnp.int32, (BLK, BLK), 1)
    diag = key <= qry
    above = key > qry
    vs_rows = [lambda ks, e=e: vst_scr[e, :, ks] for e in range(2)]
    slc = _causal_blocks(nt, ksaug_scr, vs_rows, qts_scr)

    def win_block(j, qh, e, mask):
        ks = slice(j * BLK, (j + 1) * BLK)
        qs = slice(qh * BLK, (qh + 1) * BLK)
        return (lambda: kwaug_scr[ks, :], lambda: vwt_scr[e, :, ks], lambda: qtw_scr[e, :, qs],
                mask, 2 * nt + e * nt + qh)

    blocks = []
    pos = 0
    for j in range(nt):
        n_j = 2 * (nt - j)
        blocks.extend(slc[pos:pos + n_j])
        pos += n_j
        blocks.extend(win_block(j, j, e, diag) for e in range(2))
        if j > 0:
            blocks.extend(win_block(j - 1, j, e, above) for e in range(2))
    acc = _flash(blocks, 4 * nt, HEAD_DIM, bounded)
    normed = lambda a: a[0:HEAD_DIM] / a[HEAD_DIM:HEAD_DIM + 1]

    gate_t = jax.nn.sigmoid(sm_ref[0]).T
    g_cmp_t, g_slc_t, g_win_t = [_dot01_left(egt_ref[c], gate_t) for c in range(3)]
    for qh in range(nt):
        cols = slice(qh * BLK, (qh + 1) * BLK)
        o_s = jnp.concatenate([normed(acc[e * nt + qh]) for e in range(2)], axis=0)
        o_w = jnp.concatenate([normed(acc[2 * nt + e * nt + qh]) for e in range(2)], axis=0)
        mix = (g_cmp_t[:, cols] * ocmp_scr[r, :, cols] + g_slc_t[:, cols] * o_s
               + g_win_t[:, cols] * o_w)
        o_ref[0, cols, :] = mix.T.astype(o_ref.dtype)


def _nsa(proj3, small3, kc, vc, kx, cx, nsl, qg, ksg, kwg, bd, ot, egt, *, bounded):
    b, s, _ = proj3.shape
    npair = NSA_HEADS // 2
    ns = s // SLC_BLOCK
    nc = (s - CMP_LEN) // CMP_STRIDE + 1
    const = lambda a: pl.BlockSpec(a.shape, lambda bi, r: (0,) * a.ndim)
    kvspec = lambda c: pl.BlockSpec((1, s, LANES), lambda bi, r: (bi, 0, c))
    return pl.pallas_call(
        functools.partial(_nsa_kernel, nc=nc, bounded=bounded),
        grid=(b, npair),
        in_specs=[
            pl.BlockSpec((1, s, npair * LANES), lambda bi, r: (bi, 0, COL_NSA_Q // npair)),
            pl.BlockSpec((1, s, LANES), lambda bi, r: (bi, 0, COL_NSA_Q + r)),
            pl.BlockSpec((1, s, LANES), lambda bi, r: (bi, 0, 0)),
            pl.BlockSpec((1,) + kc.shape[1:], lambda bi, r: (bi, 0, 0)),
            pl.BlockSpec((1,) + vc.shape[1:], lambda bi, r: (bi, 0, 0)),
            kvspec(COL_K_SLC), kvspec(COL_V_SLC), kvspec(COL_K_WIN), kvspec(COL_V_WIN),
            const(kx), const(cx), const(nsl),
            pl.BlockSpec((1, 2, 8, s), lambda bi, r: (r, 0, 0, 0)),
            const(qg), const(ksg), const(kwg), const(bd), const(ot),
            pl.BlockSpec((3, LANES, LANES), lambda bi, r: (0, r, 0)),
        ],
        out_specs=pl.BlockSpec((1, s, LANES), lambda bi, r: (bi, 0, r)),
        out_shape=jax.ShapeDtypeStruct((b, s, npair * LANES), BF16),
        scratch_shapes=[pltpu.VMEM((s, KA), BF16), pltpu.VMEM((s, KA), BF16),
                        pltpu.VMEM((2, HEAD_DIM + VAUG, s), BF16),
                        pltpu.VMEM((2, HEAD_DIM + VAUG, s), BF16),
                        pltpu.VMEM((2, KA, s), BF16), pltpu.VMEM((2, KA, s), BF16),
                        pltpu.VMEM((NSA_GROUPS, ns, s), F32), pltpu.VMEM((npair, LANES, s), F32),
                        pltpu.VMEM((kc.shape[1], KA), BF16)],
        compiler_params=_cparams(("parallel", "arbitrary")),
        name="nsa",
    )(proj3, proj3, small3, kc, vc, proj3, proj3, proj3, proj3, kx, cx, nsl, nsl, qg, ksg, kwg, bd,
      ot, egt)


def _merge_kernel(oa_ref, ob_ref, oc_ref, g0_ref, g1_ref, g2_ref, h_ref, wbr_ref, wo_ref, o_ref):
    merged = None
    for c, (o_r, g_r) in enumerate(((oa_ref, g0_ref), (ob_ref, g1_ref), (oc_ref, g2_ref))):
        y = jnp.dot(o_r[...], wbr_ref[c], preferred_element_type=F32)
        term = jax.nn.sigmoid(g_r[...].astype(F32)) * y
        merged = term if merged is None else merged + term
    o_ref[...] = h_ref[...] + jnp.dot(merged.astype(BF16), wo_ref[...], preferred_element_type=F32)


def _merge(oa, ob, oc, proj2, h2, wbr, wo, *, tm=1024):
    t, d = h2.shape
    bw = oa.shape[1]
    row = lambda w: pl.BlockSpec((tm, w), lambda i: (i, 0))
    gate = lambda c: pl.BlockSpec((tm, d), lambda i: (i, COL_MG * LANES // d + c))
    resident = lambda a: pl.BlockSpec(a.shape, lambda i: (0,) * a.ndim, pipeline_mode=pl.Buffered(1))
    return pl.pallas_call(
        _merge_kernel,
        grid=(t // tm,),
        in_specs=[row(bw), row(bw), row(bw), gate(0), gate(1), gate(2), row(d),
                  resident(wbr), resident(wo)],
        out_specs=row(d),
        out_shape=jax.ShapeDtypeStruct((t, d), F32),
        compiler_params=_cparams(("parallel",)),
        name="merge",
    )(oa, ob, oc, proj2, proj2, proj2, h2, wbr, wo)


HALO = 16


def _ffn_kernel(x_ref, xh_ref, g_ref, wup_ref, cw_ref, cb_ref, wd_ref, o_ref, u_scr,
                *, tm, tf, tiles_per_seq):
    i = pl.program_id(0)
    dff = wd_ref.shape[0]

    def norm(x):
        return (x * lax.rsqrt(jnp.mean(x * x, axis=-1, keepdims=True) + EPS)
                * g_ref[...]).astype(BF16)

    a_main = norm(x_ref[...])
    a_all = jnp.concatenate([norm(xh_ref[...]), a_main], axis=0)
    seq_start = (i % tiles_per_seq) == 0
    acts = []
    for c in range(dff // tf):
        cols = slice(c * tf, (c + 1) * tf)
        u = jnp.dot(a_all, wup_ref[:, cols], preferred_element_type=F32)
        rows = lax.broadcasted_iota(jnp.int32, u.shape, 0)
        u_scr[...] = jnp.where((rows < HALO) & seq_start, 0.0, u)
        uc = cb_ref[:, cols]
        for t in range(CONV_W):
            uc = uc + cw_ref[t:t + 1, cols] * u_scr[pl.ds(HALO - (CONV_W - 1) + t, tm), :]
        gt = jnp.dot(a_main, wup_ref[:, dff + c * tf:dff + (c + 1) * tf],
                     preferred_element_type=F32)
        acts.append((jax.nn.gelu(uc) * gt).astype(BF16))
    act = jnp.concatenate(acts, axis=1)
    o_ref[...] = x_ref[...] + jnp.dot(act, wd_ref[...], preferred_element_type=F32)


def _ffn(h2, g, wup, cw, cb, wd, seq, *, tm=512, tf=1408):
    t, d = h2.shape
    resident = lambda a: pl.BlockSpec(a.shape, lambda i: (0,) * a.ndim, pipeline_mode=pl.Buffered(1))
    return pl.pallas_call(
        functools.partial(_ffn_kernel, tm=tm, tf=tf, tiles_per_seq=seq // tm),
        grid=(t // tm,),
        in_specs=[
            pl.BlockSpec((tm, d), lambda i: (i, 0)),
            pl.BlockSpec((HALO, d), lambda i: (jnp.maximum(i * (tm // HALO) - 1, 0), 0)),
            resident(g), resident(wup), resident(cw), resident(cb), resident(wd),
        ],
        out_specs=pl.BlockSpec((tm, d), lambda i: (i, 0)),
        out_shape=jax.ShapeDtypeStruct((t, d), F32),
        scratch_shapes=[pltpu.VMEM((HALO + tm, tf), F32)],
        compiler_params=_cparams(("parallel",)),
        name="ffn",
    )(h2, h2, g, wup, cw, cb, wd)


def _nsa_head_order():
    hpg = NSA_HEADS // NSA_GROUPS
    return [h for r in range(hpg) for h in (r, hpg + r)]


def _constants(seq):
    ns = seq // SLC_BLOCK
    nc = (seq - CMP_LEN) // CMP_STRIDE + 1
    assert MASK0 + ns <= LANES and MASK0 >= 8 and N_BIAS <= 8
    bd = np.kron(np.eye(2), np.ones((HEAD_DIM, HEAD_DIM))).astype(np.float32)
    c_start = np.arange(LANES) * CMP_STRIDE
    s_start = np.arange(ns) * SLC_BLOCK
    ot = ((c_start[None, :] < s_start[:, None] + SLC_BLOCK)
          & (c_start[None, :] + CMP_LEN > s_start[:, None])
          & (np.arange(LANES)[None, :] < nc)).astype(np.float32)
    pos = np.arange(seq)
    kx = np.zeros((seq, LANES), np.float32)
    for k in range(N_BIAS):
        kx[:, k] = (pos // 8) * 8 if k % 2 == 0 else pos % 8
    kx[pos, MASK0 + pos // SLC_BLOCK] = 1.0
    kx[:, BOUND0:BOUND0 + 3] = 1.0
    cx = kx[np.minimum(np.arange(LANES) * CMP_STRIDE + (CMP_LEN - 1), seq - 1)]
    order = _nsa_head_order()
    eg = np.zeros((3, LANES, NSA_HEADS * HEAD_DIM), np.float32)
    for c in range(3):
        for slot, h in enumerate(order):
            eg[c, SMALL_GATE0 + c * NSA_HEADS + h, slot * HEAD_DIM:(slot + 1) * HEAD_DIM] = 1.0
    u = (np.arange(BLK)[:, None] <= np.arange(BLK)[None, :]).astype(np.float32)
    dsl = np.zeros((DIFF_HEADS, 8, seq), np.float32)
    for h in range(DIFF_HEADS):
        rows = _slope_rows(2.0 ** (-8.0 * (h + 1) / DIFF_HEADS))
        dsl[h, :N_BIAS, :] = np.asarray(rows, np.float32)[:, None]
    nsl = np.zeros((NSA_HEADS // 2, 2, 8, seq), np.float32)
    for slot, h in enumerate(order):
        rows = _slope_rows(2.0 ** (-8.0 * (h + 1) / NSA_HEADS))
        nsl[slot // 2, slot % 2, :N_BIAS, :] = np.asarray(rows, np.float32)[:, None]
    as_bf = lambda a: jnp.asarray(a, BF16)
    return dict(bd=as_bf(bd), ot=as_bf(ot), kx=as_bf(kx), cx=as_bf(cx),
                egt=as_bf(np.transpose(eg, (0, 2, 1))), u=as_bf(u), dsl=jnp.asarray(dsl, F32),
                nsl=jnp.asarray(nsl, F32))


def _pack_w_in(w):
    d = w.shape[0]
    hd = HEAD_DIM
    sizes = [512, 768, 24, 512, 512, 512, 512, 512, 512, 8, 3 * d]
    offs = np.concatenate([[0], np.cumsum(sizes)])
    nq, nkv, ngate, dq, dk, dv, fq, fk, fv, ff, mg = [w[:, offs[k]:offs[k + 1]] for k in range(11)]
    nq = jnp.concatenate([nq[:, h * hd:(h + 1) * hd] for h in _nsa_head_order()], axis=1)
    kv = lambda c, k: nkv[:, (c * 2 + k) * LANES:(c * 2 + k + 1) * LANES]
    cols = [mg, nq, dq, dk, dv, fq, fk, fv,
            kv(1, 0), kv(1, 1), kv(2, 0), kv(2, 1), kv(0, 0), kv(0, 1)]
    main = jnp.concatenate(cols, axis=1)
    main = jnp.pad(main, ((0, 0), (0, PROJ_UNITS * LANES - main.shape[1])))
    small = jnp.pad(jnp.concatenate([ngate, ff], axis=1), ((0, 0), (0, LANES - 32)))
    return main.astype(BF16), small.astype(BF16)


def kernel(x, attn_norm_g, w_in, nsa_q_g, nsa_k_g, cmp_pe, cmp_w1, cmp_w2, diff_q_g, diff_k_g,
           diff_lam, diff_subln_g, fox_q_g, fox_k_g, fox_b, w_br, w_o, ffn_norm_g, w_up, conv_w,
           conv_b, w_down):
    b, s, d = x.shape
    depth = w_in.shape[0]
    hd = HEAD_DIM
    qscale = hd ** -0.5 * L2E
    cst = _constants(s)
    rows16 = s // CMP_STRIDE
    assert rows16 == LANES and s % CMP_ROWS == 0, "NSA kernel keeps all compressed blocks in one 128-lane tile"
    order = _nsa_head_order()
    tile2 = lambda g: jnp.tile(g, 2).reshape(1, LANES).astype(F32)

    h = x.reshape(b * s, d)
    for l in range(depth):
        wmain, wsmall = _pack_w_in(w_in[l])
        proj, small = _proj(h, attn_norm_g[l].reshape(1, d), wmain, wsmall)
        proj3 = proj.reshape(b, s, PROJ_UNITS * LANES)
        small3 = small.reshape(b, s, LANES)

        w1 = cmp_w1[l].astype(BF16).reshape(2, CMP_LEN, hd, 2 * hd)
        z1 = jnp.zeros_like(w1)
        w1p = jnp.concatenate([jnp.concatenate([w1, z1], axis=3),
                               jnp.concatenate([z1, w1], axis=3)], axis=2)
        w2 = cmp_w2[l].astype(BF16)
        z2 = jnp.zeros_like(w2)
        w2p = jnp.concatenate([jnp.concatenate([w2, z2], axis=2),
                               jnp.concatenate([z2, w2], axis=2)], axis=1)
        pe8 = jnp.broadcast_to(cmp_pe[l].reshape(2, 1, CMP_LEN * hd), (2, 8, CMP_LEN * hd)).astype(BF16)
        kc, vc = _compress(proj3, w1p, pe8, cmp_w1[l].astype(BF16), w2p, tile2(nsa_k_g[l, 0]),
                           cst["bd"])

        fb_row = jnp.zeros((1, LANES), F32).at[0, SMALL_FF0:SMALL_FF0 + FOX_HEADS].set(fox_b[l])
        negf = _fcum(small3, fb_row, cst["u"])

        def mixer(call, qg, kgs, *args):
            span = 2.0 * HEAD_DIM * BOUND_SLACK * jnp.max(jnp.abs(qg)) * max_abs(kgs)
            return lax.cond(span <= BOUND_LIMIT, functools.partial(call, bounded=True),
                            functools.partial(call, bounded=False), *args)

        max_abs = lambda gs: functools.reduce(jnp.maximum, [jnp.max(jnp.abs(g)) for g in gs])
        nsa_qg, diff_qg, fox_qg = nsa_q_g[l] * qscale, diff_q_g[l] * qscale, fox_q_g[l] * qscale
        o_a = mixer(_nsa, nsa_qg, [nsa_k_g[l, 1], nsa_k_g[l, 2]],
                    proj3, small3, kc, vc, cst["kx"], cst["cx"], cst["nsl"], tile2(nsa_qg),
                    tile2(nsa_k_g[l, 1]), tile2(nsa_k_g[l, 2]), cst["bd"], cst["ot"], cst["egt"])
        lam_init = 0.8 - 0.6 * math.exp(-0.3 * l)
        o_b = mixer(functools.partial(_diff, lam_init=lam_init), diff_qg, [diff_k_g[l]],
                    proj3, cst["kx"], cst["dsl"], diff_lam[l], tile2(diff_qg), tile2(diff_k_g[l]),
                    diff_subln_g[l].reshape(1, LANES), cst["bd"])
        o_c = mixer(_fox, fox_qg, [fox_k_g[l]],
                    proj3, negf, tile2(fox_qg), tile2(fox_k_g[l]), cst["bd"])

        wbr = w_br[l]
        wbr_a = jnp.concatenate([wbr[0, hh * hd:(hh + 1) * hd] for hh in order], axis=0)
        wbr_p = jnp.stack([wbr_a, wbr[1], wbr[2]]).astype(BF16)
        bw = NSA_HEADS * hd
        h = _merge(o_a.reshape(b * s, bw), o_b.reshape(b * s, bw), o_c.reshape(b * s, bw),
                   proj, h, wbr_p, w_o[l].astype(BF16))
        h = _ffn(h, ffn_norm_g[l].reshape(1, d), w_up[l].astype(BF16), conv_w[l],
                 conv_b[l].reshape(1, -1), w_down[l].astype(BF16), s)
    return h.reshape(b, s, d)
```

```python
import functools
import math

import numpy as np
import jax
import jax.numpy as jnp
from jax import lax
from jax.experimental import pallas as pl
from jax.experimental.pallas import tpu as pltpu

F32 = jnp.float32
BF16 = jnp.bfloat16

HEAD_DIM = 64
NSA_HEADS = 8
NSA_GROUPS = 2
CMP_LEN = 32
CMP_STRIDE = 16
SLC_BLOCK = 64
SLC_TOPN = 8
WINDOW = 256
FORCE_BONUS = 1.0e4
DIFF_HEADS = 4
FOX_HEADS = 8
CONV_W = 3
EPS = 1e-6
NEG = -1.0e30
L2E = 1.4426950408889634

LANES = 128
BLK = 256
KA = 2 * LANES
N_BIAS = 6
MASK0 = 8
BOUND0 = 40
VAUG = 16
BOUND_SLACK = 1.02
BOUND_LIMIT = 100.0
VMEM_LIMIT = 56 * 1024 * 1024

COL_MG = 0
COL_NSA_Q = 24
COL_DIFF_Q, COL_DIFF_K, COL_DIFF_V = 28, 32, 36
COL_FOX_Q, COL_FOX_K, COL_FOX_V = 40, 44, 48
COL_K_SLC, COL_V_SLC, COL_K_WIN, COL_V_WIN, COL_K_CMP, COL_V_CMP = 52, 53, 54, 55, 56, 57
PROJ_UNITS = 60
SMALL_GATE0 = 0
SMALL_FF0 = 24


def _cparams(sem):
    return pltpu.CompilerParams(dimension_semantics=sem, vmem_limit_bytes=VMEM_LIMIT)


def _split3(x):
    hi = x.astype(BF16).astype(F32)
    r1 = x - hi
    mid = r1.astype(BF16).astype(F32)
    lo = (r1 - mid).astype(BF16).astype(F32)
    return hi, mid, lo


def _dot01(x, m):
    hi = x.astype(BF16)
    lo = (x - hi.astype(F32)).astype(BF16)
    return (jnp.dot(hi, m, preferred_element_type=F32)
            + jnp.dot(lo, m, preferred_element_type=F32))


def _dot01_left(m, x):
    hi = x.astype(BF16)
    lo = (x - hi.astype(F32)).astype(BF16)
    return (jnp.dot(m, hi, preferred_element_type=F32)
            + jnp.dot(m, lo, preferred_element_type=F32))


def _dot01_3(x, m):
    hi, mid, lo = _split3(x)
    return (jnp.dot(hi.astype(BF16), m, preferred_element_type=F32)
            + jnp.dot(mid.astype(BF16), m, preferred_element_type=F32)
            + jnp.dot(lo.astype(BF16), m, preferred_element_type=F32))


def _qk(q, k):
    return lax.dot_general(q, k, (((1,), (1,)), ((), ())), preferred_element_type=F32)


def _headnorm_pair(x, bd, gain):
    ss = jnp.dot((x * x).astype(BF16), bd, preferred_element_type=F32)
    return x * lax.rsqrt(ss * (1.0 / HEAD_DIM) + EPS) * gain


def _rows8(vals, width):
    row = lax.broadcasted_iota(jnp.int32, (8, width), 0)
    out = jnp.zeros((8, width), F32)
    for k, v in enumerate(vals):
        out = jnp.where(row == k, v, out)
    return out


def _query_feats(qt, half, bias8, mask_rows, bound_row):
    tq = qt.shape[1]
    row = lax.broadcasted_iota(jnp.int32, (LANES, tq), 0)
    keep = (row < HEAD_DIM) if half == 0 else (row >= HEAD_DIM)
    nmask = BOUND0 - MASK0
    parts = [jnp.where(keep, qt, 0.0), bias8,
             mask_rows if mask_rows is not None else jnp.zeros((nmask, tq), F32),
             _rows8(_split3(-bound_row), tq) if bound_row is not None else jnp.zeros((8, tq), F32),
             jnp.zeros((LANES - BOUND0 - 8, tq), F32)]
    return jnp.concatenate(parts, axis=0).astype(BF16)


def _logit_bound(qg_ref, kg_refs):
    kmax = None
    for kg_ref in kg_refs:
        k = jnp.max(jnp.abs(kg_ref[...]), axis=-1, keepdims=True)
        kmax = k if kmax is None else jnp.maximum(kmax, k)
    return jnp.max(jnp.abs(qg_ref[...]), axis=-1, keepdims=True) * kmax * (HEAD_DIM * BOUND_SLACK)


def _store_values_t(vt_view, rows, seq):
    dv = rows.shape[0]
    vt_view[0:dv, :] = rows.astype(BF16)
    vt_view[dv:dv + 8, :] = _rows8([1.0], seq).astype(BF16)
    vt_view[dv + 8:dv + VAUG, :] = jnp.zeros((VAUG - 8, seq), BF16)


QK_AHEAD = 2


def _flash(blocks, n_chain, dv, bounded):
    acc = [jnp.zeros((dv + VAUG, BLK), F32) for _ in range(n_chain)]
    mx = [jnp.full((1, BLK), NEG, F32) for _ in range(n_chain)]
    scores = {}

    def issue(k):
        if k < len(blocks):
            scores[k] = jnp.dot(blocks[k][0](), blocks[k][2](), preferred_element_type=F32)

    for k in range(2 * QK_AHEAD):
        issue(k)
    for k0 in range(0, len(blocks), 2):
        issue(k0 + 2 * QK_AHEAD)
        issue(k0 + 2 * QK_AHEAD + 1)
        pending = []
        for k in range(k0, min(k0 + 2, len(blocks))):
            _, vt, _, mask, chain = blocks[k]
            s = scores.pop(k)
            if mask is not None:
                s = jnp.where(mask, s, NEG)
            if bounded:
                pending.append((vt, chain, None, jnp.exp2(s).astype(BF16)))
            else:
                m_new = jnp.maximum(mx[chain], jnp.max(s, axis=0, keepdims=True))
                alpha = jnp.exp2(mx[chain] - m_new)
                mx[chain] = m_new
                pending.append((vt, chain, alpha, jnp.exp2(s - m_new).astype(BF16)))
        assert len({c for _, c, _, _ in pending}) == len(pending), "a pair must not share a chain"
        for vt, chain, alpha, p in pending:
            prev = acc[chain] if alpha is None else alpha * acc[chain]
            acc[chain] = prev + jnp.dot(vt(), p, preferred_element_type=F32)
    return acc


def _causal_blocks(nt, kaug_scr, vt_rows, qt_scr):
    key = lax.broadcasted_iota(jnp.int32, (BLK, BLK), 0)
    qry = lax.broadcasted_iota(jnp.int32, (BLK, BLK), 1)
    diag = key <= qry
    blocks = []
    for j in range(nt):
        ks = slice(j * BLK, (j + 1) * BLK)
        for qh in range(j, nt):
            qs = slice(qh * BLK, (qh + 1) * BLK)
            for e in range(2):
                blocks.append((lambda ks=ks: kaug_scr[ks, :],
                               lambda e=e, ks=ks: vt_rows[e](ks),
                               lambda e=e, qs=qs: qt_scr[e, :, qs],
                               diag if qh == j else None, e * nt + qh))
    return blocks


def _proj_kernel(x_ref, g_ref, w_ref, ws_ref, o_ref, os_ref, a_scr):
    j = pl.program_id(1)

    @pl.when(j == 0)
    def _():
        x = x_ref[...]
        inv = lax.rsqrt(jnp.mean(x * x, axis=-1, keepdims=True) + EPS)
        a = (x * inv * g_ref[...]).astype(BF16)
        a_scr[...] = a
        os_ref[...] = jnp.dot(a, ws_ref[...], preferred_element_type=F32)

    o_ref[...] = jnp.dot(a_scr[...], w_ref[...], preferred_element_type=F32).astype(o_ref.dtype)


def _proj(x2d, g, w, ws, *, tm=1024, tn=2560):
    t, d = x2d.shape
    n = w.shape[1]
    return pl.pallas_call(
        _proj_kernel,
        grid=(t // tm, n // tn),
        in_specs=[
            pl.BlockSpec((tm, d), lambda i, j: (i, 0)),
            pl.BlockSpec((1, d), lambda i, j: (0, 0)),
            pl.BlockSpec((d, tn), lambda i, j: (0, j)),
            pl.BlockSpec((d, LANES), lambda i, j: (0, 0)),
        ],
        out_specs=[
            pl.BlockSpec((tm, tn), lambda i, j: (i, j)),
            pl.BlockSpec((tm, LANES), lambda i, j: (i, 0)),
        ],
        out_shape=[jax.ShapeDtypeStruct((t, n), BF16), jax.ShapeDtypeStruct((t, LANES), F32)],
        scratch_shapes=[pltpu.VMEM((tm, d), BF16)],
        compiler_params=_cparams(("parallel", "arbitrary")),
        name="proj",
    )(x2d, g, w, ws)


def _compress_kernel(k_ref, v_ref, w1p_ref, pe_ref, w1f_ref, w2p_ref, kg_ref, bd_ref,
                     kc_ref, vc_ref, raw_scr):
    seq = k_ref.shape[1]
    nrow = kc_ref.shape[1]
    raw_scr[seq:, :] = jnp.zeros((raw_scr.shape[0] - seq, LANES), F32)
    for kv, (src, dst) in enumerate(((k_ref, kc_ref), (v_ref, vc_ref))):
        raw_scr[0:seq, :] = src[0].astype(F32)
        c1 = jnp.dot(pe_ref[kv], w1f_ref[kv], preferred_element_type=F32)[0:1]
        pre = jnp.concatenate([c1, c1], axis=1)
        for l in range(CMP_LEN):
            rows = raw_scr[pl.ds(l, nrow, stride=CMP_STRIDE), :].astype(BF16)
            pre = pre + jnp.dot(rows, w1p_ref[kv, l], preferred_element_type=F32)
        hid = jax.nn.gelu(pre).astype(BF16)
        o = jnp.dot(hid, w2p_ref[kv], preferred_element_type=F32)
        if kv == 0:
            o = _headnorm_pair(o, bd_ref[...], kg_ref[...])
        dst[0] = o.astype(BF16)


def _compress(proj3, w1p, pe8, w1f, w2p, kg2, bd):
    b, s, _ = proj3.shape
    nrow = s // CMP_STRIDE
    full = lambda a: pl.BlockSpec(a.shape, lambda i: (0,) * a.ndim)
    return pl.pallas_call(
        _compress_kernel,
        grid=(b,),
        in_specs=[pl.BlockSpec((1, s, LANES), lambda i: (i, 0, COL_K_CMP)),
                  pl.BlockSpec((1, s, LANES), lambda i: (i, 0, COL_V_CMP)),
                  full(w1p), full(pe8), full(w1f), full(w2p), full(kg2), full(bd)],
        out_specs=[pl.BlockSpec((1, nrow, LANES), lambda i: (i, 0, 0)),
                   pl.BlockSpec((1, nrow, LANES), lambda i: (i, 0, 0))],
        out_shape=[jax.ShapeDtypeStruct((b, nrow, LANES), BF16),
                   jax.ShapeDtypeStruct((b, nrow, LANES), BF16)],
        scratch_shapes=[pltpu.VMEM((s + CMP_LEN, LANES), F32)],
        compiler_params=_cparams(("parallel",)),
        name="compress",
    )(proj3, proj3, w1p, pe8, w1f, w2p, kg2, bd)


def _fcum_kernel(s_ref, fb_ref, u_ref, o_ref, *, chunk):
    z = s_ref[0] + fb_ref[...]
    lf = jax.nn.log_sigmoid(z)
    lft = lf.T[SMALL_FF0:SMALL_FF0 + FOX_HEADS]
    seq = lft.shape[1]
    carry = jnp.zeros((FOX_HEADS, 1), F32)
    for c in range(seq // chunk):
        fc = _dot01_3(lft[:, c * chunk:(c + 1) * chunk], u_ref[...]) + carry
        o_ref[0, :, c * chunk:(c + 1) * chunk] = -fc
        carry = fc[:, chunk - 1:chunk]


def _fcum(small3, fb_row, u):
    b, s, _ = small3.shape
    chunk = u.shape[0]
    return pl.pallas_call(
        functools.partial(_fcum_kernel, chunk=chunk),
        grid=(b,),
        in_specs=[pl.BlockSpec((1, s, LANES), lambda i: (i, 0, 0)),
                  pl.BlockSpec((1, LANES), lambda i: (0, 0)),
                  pl.BlockSpec(u.shape, lambda i: (0, 0))],
        out_specs=pl.BlockSpec((1, FOX_HEADS, s), lambda i: (i, 0, 0)),
        out_shape=jax.ShapeDtypeStruct((b, FOX_HEADS, s), F32),
        compiler_params=_cparams(("parallel",)),
        name="fcum",
    )(small3, fb_row, u)


def _fox_kernel(q_ref, k_ref, v_ref, nf_ref, qg_ref, kg_ref, bd_ref, o_ref,
                kaug_scr, vt_scr, qt_scr, *, bounded):
    p = pl.program_id(1)
    bd = bd_ref[...]
    seq = k_ref.shape[1]
    nt = seq // BLK

    kaug_scr[:, 0:LANES] = _headnorm_pair(k_ref[0].astype(F32), bd, kg_ref[...]).astype(BF16)
    cb = [nf_ref[0, pl.ds(2 * p + e, 1), :] * L2E for e in range(2)]
    rows = []
    for e in range(2):
        rows.extend(_split3(cb[e]))
    feats = jnp.concatenate([_rows8(rows, seq), jnp.zeros((BOUND0 - 8, seq), F32),
                             _rows8([1.0] * 3, seq), jnp.zeros((LANES - BOUND0 - 8, seq), F32)],
                            axis=0)
    kaug_scr[:, LANES:KA] = feats.T.astype(BF16)
    vt = v_ref[0].astype(F32).T
    for e in range(2):
        _store_values_t(vt_scr.at[e], vt[e * HEAD_DIM:(e + 1) * HEAD_DIM], seq)

    qk_bound = _logit_bound(qg_ref, [kg_ref])
    qt = _headnorm_pair(q_ref[0].astype(F32), bd, qg_ref[...]).T
    for e in range(2):
        qt_scr[e] = _query_feats(qt, e, _rows8([0.0] * (3 * e) + [1.0] * 3, seq), None,
                                 qk_bound + cb[e] if bounded else None)

    vt_rows = [lambda ks, e=e: vt_scr[e, :, ks] for e in range(2)]
    acc = _flash(_causal_blocks(nt, kaug_scr, vt_rows, qt_scr), 2 * nt, HEAD_DIM, bounded)
    for qh in range(nt):
        ot = jnp.concatenate([acc[e * nt + qh][0:HEAD_DIM] / acc[e * nt + qh][HEAD_DIM:HEAD_DIM + 1]
                              for e in range(2)], axis=0)
        o_ref[0, qh * BLK:(qh + 1) * BLK, :] = ot.T.astype(o_ref.dtype)


def _const_spec(a):
    return pl.BlockSpec(a.shape, lambda bi, p: (0,) * a.ndim)


def _col_spec(s, c):
    return pl.BlockSpec((1, s, LANES), lambda bi, p: (bi, 0, c + p))


def _fox_part(proj3, negf, qg, kg, bd):
    s = proj3.shape[1]
    specs = [_col_spec(s, COL_FOX_Q), _col_spec(s, COL_FOX_K), _col_spec(s, COL_FOX_V),
             pl.BlockSpec((1, FOX_HEADS, s), lambda bi, p: (bi, 0, 0)),
             _const_spec(qg), _const_spec(kg), _const_spec(bd)]
    scratch = [pltpu.VMEM((s, KA), BF16), pltpu.VMEM((2, HEAD_DIM + VAUG, s), BF16),
               pltpu.VMEM((2, KA, s), BF16)]
    return _fox_kernel, [proj3, proj3, proj3, negf, qg, kg, bd], specs, scratch


def _alibi_at_query(sl_rows, seq):
    pos = lax.broadcasted_iota(jnp.int32, (1, seq), 1).astype(F32)
    return (sl_rows[0:1] + sl_rows[2:3] + sl_rows[4:5]) * pos


def _diff_kernel(q_ref, k_ref, v_ref, kx_ref, sl_ref, lam_ref, qg_ref, kg_ref, sg_ref, bd_ref,
                 o_ref, kaug_scr, vt_scr, qt_scr, *, lam_init, bounded):
    bd = bd_ref[...]
    seq = k_ref.shape[1]
    nt = seq // BLK

    kaug_scr[:, 0:LANES] = _headnorm_pair(k_ref[0].astype(F32), bd, kg_ref[...]).astype(BF16)
    kaug_scr[:, LANES:KA] = kx_ref[...]
    _store_values_t(vt_scr, v_ref[0].astype(F32).T, seq)
    sl = sl_ref[0]
    bound = _logit_bound(qg_ref, [kg_ref]) + _alibi_at_query(sl, seq) if bounded else None
    qt = _headnorm_pair(q_ref[0].astype(F32), bd, qg_ref[...]).T
    for e in range(2):
        qt_scr[e] = _query_feats(qt, e, sl, None, bound)

    vt_rows = [lambda ks: vt_scr[:, ks]] * 2
    acc = _flash(_causal_blocks(nt, kaug_scr, vt_rows, qt_scr), 2 * nt, LANES, bounded)

    lv = lam_ref[...]
    lam = (jnp.exp(jnp.sum(lv[0:1] * lv[1:2], axis=-1, keepdims=True))
           - jnp.exp(jnp.sum(lv[2:3] * lv[3:4], axis=-1, keepdims=True)) + lam_init)
    for qh in range(nt):
        a0, a1 = acc[qh], acc[nt + qh]
        ob = (a0[0:LANES] / a0[LANES:LANES + 1] - lam * (a1[0:LANES] / a1[LANES:LANES + 1])).T
        ob = ob * lax.rsqrt(jnp.mean(ob * ob, axis=-1, keepdims=True) + EPS) * sg_ref[...]
        o_ref[0, qh * BLK:(qh + 1) * BLK, :] = (ob * (1.0 - lam_init)).astype(o_ref.dtype)


def _diff_part(proj3, kx, slrows, lam_p, qg, kg, sg, bd, lam_init):
    s = proj3.shape[1]
    specs = [_col_spec(s, COL_DIFF_Q), _col_spec(s, COL_DIFF_K), _col_spec(s, COL_DIFF_V),
             _const_spec(kx), pl.BlockSpec((1, 8, s), lambda bi, p: (p, 0, 0)),
             _const_spec(lam_p), _const_spec(qg), _const_spec(kg), _const_spec(sg), _const_spec(bd)]
    scratch = [pltpu.VMEM((s, KA), BF16), pltpu.VMEM((LANES + VAUG, s), BF16),
               pltpu.VMEM((2, KA, s), BF16)]
    return (functools.partial(_diff_kernel, lam_init=lam_init),
            [proj3, proj3, proj3, kx, slrows, lam_p, qg, kg, sg, bd], specs, scratch)


def _nsa_select(imp_t, t0):
    ns, tq = imp_t.shape
    blk = lax.broadcasted_iota(jnp.int32, (ns, tq), 0)
    cur = (t0 + lax.broadcasted_iota(jnp.int32, (ns, tq), 1)) // SLC_BLOCK
    forced = (blk == 0) | (blk == cur) | (blk == cur - 1)
    score = jnp.where(blk <= cur, jnp.where(forced, imp_t + FORCE_BONUS, imp_t), NEG)
    sel = jnp.zeros((ns, tq), F32)
    blk_f = blk.astype(F32)
    for _ in range(min(SLC_TOPN, ns)):
        best = jnp.max(score, axis=0, keepdims=True)
        first = jnp.min(jnp.where(score == best, blk_f, float(ns)), axis=0, keepdims=True)
        take = blk_f == first
        sel = jnp.where(take, 1.0, sel)
        score = jnp.where(take, 2.0 * NEG, score)
    return sel


def _slope_rows(slope):
    hi, mid, lo = [float(np.float32(v)) for v in _np_split3(slope * L2E)]
    return [hi, hi, mid, mid, lo, lo]


def _np_split3(x):
    x = np.float32(x)
    hi = np.float32(x.astype(BF16))
    mid = np.float32((x - hi).astype(BF16))
    lo = np.float32((x - hi - mid).astype(BF16))
    return hi, mid, lo


CMP_ROWS = 512


def _nsa_prepare(qall_ref, kc_ref, vc_ref, ks_ref, vs_ref, kw_ref, vw_ref, kx_ref, cx_ref, nsl_ref,
                 qg_ref, ksg_ref, kwg_ref, bd_ref, ot_ref, ksaug_scr, kwaug_scr, vst_scr, vwt_scr,
                 notsel_scr, ocmp_scr, kcaug_scr, *, nc):
    bd = bd_ref[...]
    npair = NSA_HEADS // 2
    seq = ks_ref.shape[1]
    ksaug_scr[:, 0:LANES] = _headnorm_pair(ks_ref[0].astype(F32), bd, ksg_ref[...]).astype(BF16)
    kwaug_scr[:, 0:LANES] = _headnorm_pair(kw_ref[0].astype(F32), bd, kwg_ref[...]).astype(BF16)
    ksaug_scr[:, LANES:KA] = kx_ref[...]
    kwaug_scr[:, LANES:KA] = kx_ref[...]
    for v_ref, vt_scr in ((vs_ref, vst_scr), (vw_ref, vwt_scr)):
        vt = v_ref[0].astype(F32).T
        for e in range(2):
            _store_values_t(vt_scr.at[e], vt[e * HEAD_DIM:(e + 1) * HEAD_DIM], seq)
    kcaug_scr[:, 0:LANES] = kc_ref[0]
    kcaug_scr[:, LANES:KA] = cx_ref[...]
    vct = vc_ref[0].astype(F32).T.astype(BF16)

    nidx = lax.broadcasted_iota(jnp.int32, (LANES, CMP_ROWS), 0)
    cend = nidx * CMP_STRIDE + (CMP_LEN - 1)

    def chunk(c, carry):
        t0 = pl.multiple_of(c * CMP_ROWS, CMP_ROWS)
        tq = t0 + lax.broadcasted_iota(jnp.int32, (LANES, CMP_ROWS), 1)
        cmask = (tq >= cend) & (nidx < nc)
        psum = [jnp.zeros((LANES, CMP_ROWS), F32), jnp.zeros((LANES, CMP_ROWS), F32)]
        scores = []
        for r in range(npair):
            qt = _headnorm_pair(qall_ref[0, pl.ds(t0, CMP_ROWS), r * LANES:(r + 1) * LANES].astype(F32),
                                bd, qg_ref[...]).T
            for e in range(2):
                qf = _query_feats(qt, e, nsl_ref[r, e, :, 0:CMP_ROWS], None, None)
                scores.append(jnp.dot(kcaug_scr[...], qf, preferred_element_type=F32))
        probs = []
        for k, s in enumerate(scores):
            s = jnp.where(cmask, s, NEG)
            m = jnp.max(s, axis=0, keepdims=True)
            pe = jnp.where(cmask, jnp.exp2(s - m), 0.0)
            l = jnp.sum(pe, axis=0, keepdims=True)
            pn = pe / jnp.where(l > 0.0, l, 1.0)
            psum[k % 2] = psum[k % 2] + pn
            probs.append(pn.astype(BF16))
        for k, p in enumerate(probs):
            hs = slice((k % 2) * HEAD_DIM, (k % 2 + 1) * HEAD_DIM)
            ocmp_scr[k // 2, hs, pl.ds(t0, CMP_ROWS)] = jnp.dot(vct[hs], p,
                                                                preferred_element_type=F32)
        for g in range(NSA_GROUPS):
            hi, mid, lo = _split3(psum[g])
            imp_t = (jnp.dot(ot_ref[...], hi.astype(BF16), preferred_element_type=F32)
                     + jnp.dot(ot_ref[...], mid.astype(BF16), preferred_element_type=F32)
                     + jnp.dot(ot_ref[...], lo.astype(BF16), preferred_element_type=F32))
            notsel_scr[g, :, pl.ds(t0, CMP_ROWS)] = (_nsa_select(imp_t, t0) - 1.0) * -NEG
        return carry

    lax.fori_loop(0, seq // CMP_ROWS, chunk, 0)


def _nsa_kernel(qall_ref, q_ref, sm_ref, kc_ref, vc_ref, ks_ref, vs_ref, kw_ref, vw_ref, kx_ref,
                cx_ref, nsl_ref, slr_ref, qg_ref, ksg_ref, kwg_ref, bd_ref, ot_ref, egt_ref, o_ref,
                ksaug_scr, kwaug_scr, vst_scr, vwt_scr, qts_scr, qtw_scr, notsel_scr, ocmp_scr,
                kcaug_scr, *, nc, bounded):
    r = pl.program_id(1)
    bd = bd_ref[...]
    seq = ks_ref.shape[1]
    nt = seq // BLK

    @pl.when(r == 0)
    def _():
        _nsa_prepare(qall_ref, kc_ref, vc_ref, ks_ref, vs_ref, kw_ref, vw_ref, kx_ref, cx_ref,
                     nsl_ref, qg_ref, ksg_ref, kwg_ref, bd_ref, ot_ref, ksaug_scr, kwaug_scr,
                     vst_scr, vwt_scr, notsel_scr, ocmp_scr, kcaug_scr, nc=nc)

    qk_bound = _logit_bound(qg_ref, [ksg_ref, kwg_ref])
    qt = _headnorm_pair(q_ref[0].astype(F32), bd, qg_ref[...]).T
    for e in range(2):
        sl = slr_ref[0, e]
        bound = qk_bound + _alibi_at_query(sl, seq) if bounded else None
        qts_scr[e] = _query_feats(qt, e, sl, notsel_scr[e], bound)
        qtw_scr[e] = _query_feats(qt, e, sl, None, bound)

    key = lax.broadcasted_iota(jnp.int32, (BLK, BLK), 0)
    qry = lax.broadcasted_iota(jnp.int32, (BLK, BLK), 1)
    diag = key <= qry
    above = key > qry
    vs_rows = [lambda ks, e=e: vst_scr[e, :, ks] for e in range(2)]
    slc = _causal_blocks(nt, ksaug_scr, vs_rows, qts_scr)

    def win_block(j, qh, e, mask):
        ks = slice(j * BLK, (j + 1) * BLK)
        qs = slice(qh * BLK, (qh + 1) * BLK)
        return (lambda: kwaug_scr[ks, :], lambda: vwt_scr[e, :, ks], lambda: qtw_scr[e, :, qs],
                mask, 2 * nt + e * nt + qh)

    blocks = []
    pos = 0
    for j in range(nt):
        n_j = 2 * (nt - j)
        blocks.extend(slc[pos:pos + n_j])
        pos += n_j
        blocks.extend(win_block(j, j, e, diag) for e in range(2))
        if j > 0:
            blocks.extend(win_block(j - 1, j, e, above) for e in range(2))
    acc = _flash(blocks, 4 * nt, HEAD_DIM, bounded)
    normed = lambda a: a[0:HEAD_DIM] / a[HEAD_DIM:HEAD_DIM + 1]

    gate_t = jax.nn.sigmoid(sm_ref[0]).T
    g_cmp_t, g_slc_t, g_win_t = [_dot01_left(egt_ref[c], gate_t) for c in range(3)]
    for qh in range(nt):
        cols = slice(qh * BLK, (qh + 1) * BLK)
        o_s = jnp.concatenate([normed(acc[e * nt + qh]) for e in range(2)], axis=0)
        o_w = jnp.concatenate([normed(acc[2 * nt + e * nt + qh]) for e in range(2)], axis=0)
        mix = (g_cmp_t[:, cols] * ocmp_scr[r, :, cols] + g_slc_t[:, cols] * o_s
               + g_win_t[:, cols] * o_w)
        o_ref[0, cols, :] = mix.T.astype(o_ref.dtype)


def _nsa_part(proj3, small3, kc, vc, kx, cx, nsl, qg, ksg, kwg, bd, ot, egt):
    s = proj3.shape[1]
    npair = NSA_HEADS // 2
    ns = s // SLC_BLOCK
    nc = (s - CMP_LEN) // CMP_STRIDE + 1
    row = lambda a: pl.BlockSpec((1,) + a.shape[1:], lambda bi, p: (bi,) + (0,) * (a.ndim - 1))
    kvspec = lambda c: pl.BlockSpec((1, s, LANES), lambda bi, p: (bi, 0, c))
    specs = [pl.BlockSpec((1, s, npair * LANES), lambda bi, p: (bi, 0, COL_NSA_Q // npair)),
             _col_spec(s, COL_NSA_Q), row(small3), row(kc), row(vc),
             kvspec(COL_K_SLC), kvspec(COL_V_SLC), kvspec(COL_K_WIN), kvspec(COL_V_WIN),
             _const_spec(kx), _const_spec(cx), _const_spec(nsl),
             pl.BlockSpec((1, 2, 8, s), lambda bi, p: (p, 0, 0, 0)),
             _const_spec(qg), _const_spec(ksg), _const_spec(kwg), _const_spec(bd), _const_spec(ot),
             pl.BlockSpec((3, LANES, LANES), lambda bi, p: (0, p, 0))]
    scratch = [pltpu.VMEM((s, KA), BF16), pltpu.VMEM((s, KA), BF16),
               pltpu.VMEM((2, HEAD_DIM + VAUG, s), BF16), pltpu.VMEM((2, HEAD_DIM + VAUG, s), BF16),
               pltpu.VMEM((2, KA, s), BF16), pltpu.VMEM((2, KA, s), BF16),
               pltpu.VMEM((NSA_GROUPS, ns, s), F32), pltpu.VMEM((npair, LANES, s), F32),
               pltpu.VMEM((kc.shape[1], KA), BF16)]
    arrays = [proj3, proj3, small3, kc, vc, proj3, proj3, proj3, proj3, kx, cx, nsl, nsl, qg, ksg,
              kwg, bd, ot, egt]
    return functools.partial(_nsa_kernel, nc=nc), arrays, specs, scratch


def _attention_kernel(*refs, layout, bounded):
    pos = 0
    inputs = []
    for _, n_in, _ in layout:
        inputs.append(refs[pos:pos + n_in])
        pos += n_in
    outputs = refs[pos:pos + len(layout)]
    pos += len(layout)
    for (fn, _, n_scr), ins, out in zip(layout, inputs, outputs):
        fn(*ins, out, *refs[pos:pos + n_scr], bounded=bounded)
        pos += n_scr


def _attention(layout, specs, scratch, b, s, *arrays, bounded):
    out = pl.BlockSpec((1, s, LANES), lambda bi, p: (bi, 0, p))
    npair = NSA_HEADS // 2
    return pl.pallas_call(
        functools.partial(_attention_kernel, layout=layout, bounded=bounded),
        grid=(b, npair),
        in_specs=specs,
        out_specs=[out] * len(layout),
        out_shape=[jax.ShapeDtypeStruct((b, s, npair * LANES), BF16)] * len(layout),
        scratch_shapes=scratch,
        compiler_params=_cparams(("parallel", "arbitrary")),
        name="attention",
    )(*arrays)


def _merge_kernel(oa_ref, ob_ref, oc_ref, g0_ref, g1_ref, g2_ref, h_ref, wbr_ref, wo_ref, o_ref):
    merged = None
    for c, (o_r, g_r) in enumerate(((oa_ref, g0_ref), (ob_ref, g1_ref), (oc_ref, g2_ref))):
        y = jnp.dot(o_r[...], wbr_ref[c], preferred_element_type=F32)
        term = jax.nn.sigmoid(g_r[...].astype(F32)) * y
        merged = term if merged is None else merged + term
    o_ref[...] = h_ref[...] + jnp.dot(merged.astype(BF16), wo_ref[...], preferred_element_type=F32)


def _merge(oa, ob, oc, proj2, h2, wbr, wo, *, tm=1024):
    t, d = h2.shape
    bw = oa.shape[1]
    row = lambda w: pl.BlockSpec((tm, w), lambda i: (i, 0))
    gate = lambda c: pl.BlockSpec((tm, d), lambda i: (i, COL_MG * LANES // d + c))
    resident = lambda a: pl.BlockSpec(a.shape, lambda i: (0,) * a.ndim, pipeline_mode=pl.Buffered(1))
    return pl.pallas_call(
        _merge_kernel,
        grid=(t // tm,),
        in_specs=[row(bw), row(bw), row(bw), gate(0), gate(1), gate(2), row(d),
                  resident(wbr), resident(wo)],
        out_specs=row(d),
        out_shape=jax.ShapeDtypeStruct((t, d), F32),
        compiler_params=_cparams(("parallel",)),
        name="merge",
    )(oa, ob, oc, proj2, proj2, proj2, h2, wbr, wo)


HALO = 16


def _ffn_kernel(x_ref, xh_ref, g_ref, wup_ref, cw_ref, cb_ref, wd_ref, o_ref, u_scr,
                *, tm, tf, tiles_per_seq):
    i = pl.program_id(0)
    dff = wd_ref.shape[0]

    def norm(x):
        return (x * lax.rsqrt(jnp.mean(x * x, axis=-1, keepdims=True) + EPS)
                * g_ref[...]).astype(BF16)

    a_main = norm(x_ref[...])
    a_all = jnp.concatenate([norm(xh_ref[...]), a_main], axis=0)
    seq_start = (i % tiles_per_seq) == 0
    acts = []
    for c in range(dff // tf):
        cols = slice(c * tf, (c + 1) * tf)
        u = jnp.dot(a_all, wup_ref[:, cols], preferred_element_type=F32)
        rows = lax.broadcasted_iota(jnp.int32, u.shape, 0)
        u_scr[...] = jnp.where((rows < HALO) & seq_start, 0.0, u)
        uc = cb_ref[:, cols]
        for t in range(CONV_W):
            uc = uc + cw_ref[t:t + 1, cols] * u_scr[pl.ds(HALO - (CONV_W - 1) + t, tm), :]
        gt = jnp.dot(a_main, wup_ref[:, dff + c * tf:dff + (c + 1) * tf],
                     preferred_element_type=F32)
        acts.append((jax.nn.gelu(uc) * gt).astype(BF16))
    act = jnp.concatenate(acts, axis=1)
    o_ref[...] = x_ref[...] + jnp.dot(act, wd_ref[...], preferred_element_type=F32)


def _ffn(h2, g, wup, cw, cb, wd, seq, *, tm=512, tf=1408):
    t, d = h2.shape
    resident = lambda a: pl.BlockSpec(a.shape, lambda i: (0,) * a.ndim, pipeline_mode=pl.Buffered(1))
    return pl.pallas_call(
        functools.partial(_ffn_kernel, tm=tm, tf=tf, tiles_per_seq=seq // tm),
        grid=(t // tm,),
        in_specs=[
            pl.BlockSpec((tm, d), lambda i: (i, 0)),
            pl.BlockSpec((HALO, d), lambda i: (jnp.maximum(i * (tm // HALO) - 1, 0), 0)),
            resident(g), resident(wup), resident(cw), resident(cb), resident(wd),
        ],
        out_specs=pl.BlockSpec((tm, d), lambda i: (i, 0)),
        out_shape=jax.ShapeDtypeStruct((t, d), F32),
        scratch_shapes=[pltpu.VMEM((HALO + tm, tf), F32)],
        compiler_params=_cparams(("parallel",)),
        name="ffn",
    )(h2, h2, g, wup, cw, cb, wd)


def _nsa_head_order():
    hpg = NSA_HEADS // NSA_GROUPS
    return [h for r in range(hpg) for h in (r, hpg + r)]


def _constants(seq):
    ns = seq // SLC_BLOCK
    nc = (seq - CMP_LEN) // CMP_STRIDE + 1
    assert MASK0 + ns <= LANES and MASK0 >= 8 and N_BIAS <= 8
    bd = np.kron(np.eye(2), np.ones((HEAD_DIM, HEAD_DIM))).astype(np.float32)
    c_start = np.arange(LANES) * CMP_STRIDE
    s_start = np.arange(ns) * SLC_BLOCK
    ot = ((c_start[None, :] < s_start[:, None] + SLC_BLOCK)
          & (c_start[None, :] + CMP_LEN > s_start[:, None])
          & (np.arange(LANES)[None, :] < nc)).astype(np.float32)
    pos = np.arange(seq)
    kx = np.zeros((seq, LANES), np.float32)
    for k in range(N_BIAS):
        kx[:, k] = (pos // 8) * 8 if k % 2 == 0 else pos % 8
    kx[pos, MASK0 + pos // SLC_BLOCK] = 1.0
    kx[:, BOUND0:BOUND0 + 3] = 1.0
    cx = kx[np.minimum(np.arange(LANES) * CMP_STRIDE + (CMP_LEN - 1), seq - 1)]
    order = _nsa_head_order()
    eg = np.zeros((3, LANES, NSA_HEADS * HEAD_DIM), np.float32)
    for c in range(3):
        for slot, h in enumerate(order):
            eg[c, SMALL_GATE0 + c * NSA_HEADS + h, slot * HEAD_DIM:(slot + 1) * HEAD_DIM] = 1.0
    u = (np.arange(BLK)[:, None] <= np.arange(BLK)[None, :]).astype(np.float32)
    dsl = np.zeros((DIFF_HEADS, 8, seq), np.float32)
    for h in range(DIFF_HEADS):
        rows = _slope_rows(2.0 ** (-8.0 * (h + 1) / DIFF_HEADS))
        dsl[h, :N_BIAS, :] = np.asarray(rows, np.float32)[:, None]
    nsl = np.zeros((NSA_HEADS // 2, 2, 8, seq), np.float32)
    for slot, h in enumerate(order):
        rows = _slope_rows(2.0 ** (-8.0 * (h + 1) / NSA_HEADS))
        nsl[slot // 2, slot % 2, :N_BIAS, :] = np.asarray(rows, np.float32)[:, None]
    as_bf = lambda a: jnp.asarray(a, BF16)
    return dict(bd=as_bf(bd), ot=as_bf(ot), kx=as_bf(kx), cx=as_bf(cx),
                egt=as_bf(np.transpose(eg, (0, 2, 1))), u=as_bf(u), dsl=jnp.asarray(dsl, F32),
                nsl=jnp.asarray(nsl, F32))


def _pack_w_in(w):
    d = w.shape[0]
    hd = HEAD_DIM
    sizes = [512, 768, 24, 512, 512, 512, 512, 512, 512, 8, 3 * d]
    offs = np.concatenate([[0], np.cumsum(sizes)])
    nq, nkv, ngate, dq, dk, dv, fq, fk, fv, ff, mg = [w[:, offs[k]:offs[k + 1]] for k in range(11)]
    nq = jnp.concatenate([nq[:, h * hd:(h + 1) * hd] for h in _nsa_head_order()], axis=1)
    kv = lambda c, k: nkv[:, (c * 2 + k) * LANES:(c * 2 + k + 1) * LANES]
    cols = [mg, nq, dq, dk, dv, fq, fk, fv,
            kv(1, 0), kv(1, 1), kv(2, 0), kv(2, 1), kv(0, 0), kv(0, 1)]
    main = jnp.concatenate(cols, axis=1)
    main = jnp.pad(main, ((0, 0), (0, PROJ_UNITS * LANES - main.shape[1])))
    small = jnp.pad(jnp.concatenate([ngate, ff], axis=1), ((0, 0), (0, LANES - 32)))
    return main.astype(BF16), small.astype(BF16)


def kernel(x, attn_norm_g, w_in, nsa_q_g, nsa_k_g, cmp_pe, cmp_w1, cmp_w2, diff_q_g, diff_k_g,
           diff_lam, diff_subln_g, fox_q_g, fox_k_g, fox_b, w_br, w_o, ffn_norm_g, w_up, conv_w,
           conv_b, w_down):
    b, s, d = x.shape
    depth = w_in.shape[0]
    hd = HEAD_DIM
    qscale = hd ** -0.5 * L2E
    cst = _constants(s)
    rows16 = s // CMP_STRIDE
    assert rows16 == LANES and s % CMP_ROWS == 0, "NSA kernel keeps all compressed blocks in one 128-lane tile"
    order = _nsa_head_order()
    tile2 = lambda g: jnp.tile(g, 2).reshape(1, LANES).astype(F32)

    h = x.reshape(b * s, d)
    for l in range(depth):
        wmain, wsmall = _pack_w_in(w_in[l])
        proj, small = _proj(h, attn_norm_g[l].reshape(1, d), wmain, wsmall)
        proj3 = proj.reshape(b, s, PROJ_UNITS * LANES)
        small3 = small.reshape(b, s, LANES)

        w1 = cmp_w1[l].astype(BF16).reshape(2, CMP_LEN, hd, 2 * hd)
        z1 = jnp.zeros_like(w1)
        w1p = jnp.concatenate([jnp.concatenate([w1, z1], axis=3),
                               jnp.concatenate([z1, w1], axis=3)], axis=2)
        w2 = cmp_w2[l].astype(BF16)
        z2 = jnp.zeros_like(w2)
        w2p = jnp.concatenate([jnp.concatenate([w2, z2], axis=2),
                               jnp.concatenate([z2, w2], axis=2)], axis=1)
        pe8 = jnp.broadcast_to(cmp_pe[l].reshape(2, 1, CMP_LEN * hd), (2, 8, CMP_LEN * hd)).astype(BF16)
        kc, vc = _compress(proj3, w1p, pe8, cmp_w1[l].astype(BF16), w2p, tile2(nsa_k_g[l, 0]),
                           cst["bd"])

        fb_row = jnp.zeros((1, LANES), F32).at[0, SMALL_FF0:SMALL_FF0 + FOX_HEADS].set(fox_b[l])
        negf = _fcum(small3, fb_row, cst["u"])

        nsa_qg, diff_qg, fox_qg = nsa_q_g[l] * qscale, diff_q_g[l] * qscale, fox_q_g[l] * qscale
        lam_init = 0.8 - 0.6 * math.exp(-0.3 * l)
        parts = [
            _nsa_part(proj3, small3, kc, vc, cst["kx"], cst["cx"], cst["nsl"], tile2(nsa_qg),
                      tile2(nsa_k_g[l, 1]), tile2(nsa_k_g[l, 2]), cst["bd"], cst["ot"], cst["egt"]),
            _diff_part(proj3, cst["kx"], cst["dsl"], diff_lam[l], tile2(diff_qg),
                       tile2(diff_k_g[l]), diff_subln_g[l].reshape(1, LANES), cst["bd"], lam_init),
            _fox_part(proj3, negf, tile2(fox_qg), tile2(fox_k_g[l]), cst["bd"]),
        ]
        layout = tuple((fn, len(arrs), len(scr)) for fn, arrs, _, scr in parts)
        specs = [sp for _, _, sps, _ in parts for sp in sps]
        scratch = [sc for _, _, _, scs in parts for sc in scs]
        arrays = [a for _, arrs, _, _ in parts for a in arrs]
        amax = lambda g: jnp.max(jnp.abs(g))
        qk_max = functools.reduce(jnp.maximum, [
            amax(nsa_qg) * jnp.maximum(amax(nsa_k_g[l, 1]), amax(nsa_k_g[l, 2])),
            amax(diff_qg) * amax(diff_k_g[l]), amax(fox_qg) * amax(fox_k_g[l])])
        call = functools.partial(_attention, layout, specs, scratch, b, s)
        o_a, o_b, o_c = lax.cond(2.0 * HEAD_DIM * BOUND_SLACK * qk_max <= BOUND_LIMIT,
                                 functools.partial(call, bounded=True),
                                 functools.partial(call, bounded=False), *arrays)

        wbr = w_br[l]
        wbr_a = jnp.concatenate([wbr[0, hh * hd:(hh + 1) * hd] for hh in order], axis=0)
        wbr_p = jnp.stack([wbr_a, wbr[1], wbr[2]]).astype(BF16)
        bw = NSA_HEADS * hd
        h = _merge(o_a.reshape(b * s, bw), o_b.reshape(b * s, bw), o_c.reshape(b * s, bw),
                   proj, h, wbr_p, w_o[l].astype(BF16))
        h = _ffn(h, ffn_norm_g[l].reshape(1, d), w_up[l].astype(BF16), conv_w[l],
                 conv_b[l].reshape(1, -1), w_down[l].astype(BF16), s)
    return h.reshape(b, s, d)
```

```python
import functools
import math

import numpy as np
import jax
import jax.numpy as jnp
from jax import lax
from jax.experimental import pallas as pl
from jax.experimental.pallas import tpu as pltpu

F32 = jnp.float32
BF16 = jnp.bfloat16

HEAD_DIM = 64
NSA_HEADS = 8
NSA_GROUPS = 2
CMP_LEN = 32
CMP_STRIDE = 16
SLC_BLOCK = 64
SLC_TOPN = 8
WINDOW = 256
FORCE_BONUS = 1.0e4
DIFF_HEADS = 4
FOX_HEADS = 8
CONV_W = 3
EPS = 1e-6
NEG = -1.0e30
L2E = 1.4426950408889634

LANES = 128
BLK = 256
KA = 2 * LANES
N_BIAS = 6
MASK0 = 8
BOUND0 = 40
VAUG = 16
BOUND_SLACK = 1.02
BOUND_LIMIT = 100.0
VMEM_LIMIT = 56 * 1024 * 1024

COL_MG = 0
COL_NSA_Q = 24
COL_DIFF_Q, COL_DIFF_K, COL_DIFF_V = 28, 32, 36
COL_FOX_Q, COL_FOX_K, COL_FOX_V = 40, 44, 48
COL_K_SLC, COL_V_SLC, COL_K_WIN, COL_V_WIN, COL_K_CMP, COL_V_CMP = 52, 53, 54, 55, 56, 57
PROJ_UNITS = 60
SMALL_GATE0 = 0
SMALL_FF0 = 24


def _cparams(sem):
    return pltpu.CompilerParams(dimension_semantics=sem, vmem_limit_bytes=VMEM_LIMIT)


def _split3(x):
    hi = x.astype(BF16).astype(F32)
    r1 = x - hi
    mid = r1.astype(BF16).astype(F32)
    lo = (r1 - mid).astype(BF16).astype(F32)
    return hi, mid, lo


def _dot01(x, m):
    hi = x.astype(BF16)
    lo = (x - hi.astype(F32)).astype(BF16)
    return (jnp.dot(hi, m, preferred_element_type=F32)
            + jnp.dot(lo, m, preferred_element_type=F32))


def _dot01_left(m, x):
    hi = x.astype(BF16)
    lo = (x - hi.astype(F32)).astype(BF16)
    return (jnp.dot(m, hi, preferred_element_type=F32)
            + jnp.dot(m, lo, preferred_element_type=F32))


def _dot01_3(x, m):
    hi, mid, lo = _split3(x)
    return (jnp.dot(hi.astype(BF16), m, preferred_element_type=F32)
            + jnp.dot(mid.astype(BF16), m, preferred_element_type=F32)
            + jnp.dot(lo.astype(BF16), m, preferred_element_type=F32))


def _qk(q, k):
    return lax.dot_general(q, k, (((1,), (1,)), ((), ())), preferred_element_type=F32)


def _headnorm_pair(x, bd, gain):
    ss = jnp.dot((x * x).astype(BF16), bd, preferred_element_type=F32)
    return x * lax.rsqrt(ss * (1.0 / HEAD_DIM) + EPS) * gain


def _rows8(vals, width):
    row = lax.broadcasted_iota(jnp.int32, (8, width), 0)
    out = jnp.zeros((8, width), F32)
    for k, v in enumerate(vals):
        out = jnp.where(row == k, v, out)
    return out


def _query_feats(qt, half, bias8, mask_rows, bound_row):
    tq = qt.shape[1]
    row = lax.broadcasted_iota(jnp.int32, (LANES, tq), 0)
    keep = (row < HEAD_DIM) if half == 0 else (row >= HEAD_DIM)
    nmask = BOUND0 - MASK0
    parts = [jnp.where(keep, qt, 0.0), bias8,
             mask_rows if mask_rows is not None else jnp.zeros((nmask, tq), F32),
             _rows8(_split3(-bound_row), tq) if bound_row is not None else jnp.zeros((8, tq), F32),
             jnp.zeros((LANES - BOUND0 - 8, tq), F32)]
    return jnp.concatenate(parts, axis=0).astype(BF16)


def _logit_bound(qg_ref, kg_refs):
    kmax = None
    for kg_ref in kg_refs:
        k = jnp.max(jnp.abs(kg_ref[...]), axis=-1, keepdims=True)
        kmax = k if kmax is None else jnp.maximum(kmax, k)
    return jnp.max(jnp.abs(qg_ref[...]), axis=-1, keepdims=True) * kmax * (HEAD_DIM * BOUND_SLACK)


def _store_values_t(vt_view, rows, seq):
    dv = rows.shape[0]
    vt_view[0:dv, :] = rows.astype(BF16)
    vt_view[dv:dv + 8, :] = _rows8([1.0], seq).astype(BF16)
    vt_view[dv + 8:dv + VAUG, :] = jnp.zeros((VAUG - 8, seq), BF16)


QK_AHEAD = 2


def _flash(blocks, n_chain, dv, bounded):
    acc = [jnp.zeros((dv + VAUG, BLK), F32) for _ in range(n_chain)]
    mx = [jnp.full((1, BLK), NEG, F32) for _ in range(n_chain)]
    scores = {}

    def issue(k):
        if k < len(blocks):
            scores[k] = jnp.dot(blocks[k][0](), blocks[k][2](), preferred_element_type=F32)

    for k in range(2 * QK_AHEAD):
        issue(k)
    for k0 in range(0, len(blocks), 2):
        issue(k0 + 2 * QK_AHEAD)
        issue(k0 + 2 * QK_AHEAD + 1)
        pending = []
        for k in range(k0, min(k0 + 2, len(blocks))):
            _, vt, _, mask, chain = blocks[k]
            s = scores.pop(k)
            if mask is not None:
                s = jnp.where(mask, s, NEG)
            if bounded:
                pending.append((vt, chain, None, jnp.exp2(s).astype(BF16)))
            else:
                m_new = jnp.maximum(mx[chain], jnp.max(s, axis=0, keepdims=True))
                alpha = jnp.exp2(mx[chain] - m_new)
                mx[chain] = m_new
                pending.append((vt, chain, alpha, jnp.exp2(s - m_new).astype(BF16)))
        assert len({c for _, c, _, _ in pending}) == len(pending), "a pair must not share a chain"
        for vt, chain, alpha, p in pending:
            prev = acc[chain] if alpha is None else alpha * acc[chain]
            acc[chain] = prev + jnp.dot(vt(), p, preferred_element_type=F32)
    return acc


def _causal_blocks(nt, kaug_scr, vt_rows, qt_scr):
    key = lax.broadcasted_iota(jnp.int32, (BLK, BLK), 0)
    qry = lax.broadcasted_iota(jnp.int32, (BLK, BLK), 1)
    diag = key <= qry
    blocks = []
    for j in range(nt):
        ks = slice(j * BLK, (j + 1) * BLK)
        for qh in range(j, nt):
            qs = slice(qh * BLK, (qh + 1) * BLK)
            for e in range(2):
                blocks.append((lambda ks=ks: kaug_scr[ks, :],
                               lambda e=e, ks=ks: vt_rows[e](ks),
                               lambda e=e, qs=qs: qt_scr[e, :, qs],
                               diag if qh == j else None, e * nt + qh))
    return blocks


def _proj_kernel(x_ref, g_ref, w_ref, ws_ref, o_ref, os_ref, a_scr):
    j = pl.program_id(1)

    @pl.when(j == 0)
    def _():
        x = x_ref[...]
        inv = lax.rsqrt(jnp.mean(x * x, axis=-1, keepdims=True) + EPS)
        a = (x * inv * g_ref[...]).astype(BF16)
        a_scr[...] = a
        os_ref[...] = jnp.dot(a, ws_ref[...], preferred_element_type=F32)

    o_ref[...] = jnp.dot(a_scr[...], w_ref[...], preferred_element_type=F32).astype(o_ref.dtype)


def _proj(x2d, g, w, ws, *, tm=1024, tn=2560):
    t, d = x2d.shape
    n = w.shape[1]
    return pl.pallas_call(
        _proj_kernel,
        grid=(t // tm, n // tn),
        in_specs=[
            pl.BlockSpec((tm, d), lambda i, j: (i, 0)),
            pl.BlockSpec((1, d), lambda i, j: (0, 0)),
            pl.BlockSpec((d, tn), lambda i, j: (0, j)),
            pl.BlockSpec((d, LANES), lambda i, j: (0, 0)),
        ],
        out_specs=[
            pl.BlockSpec((tm, tn), lambda i, j: (i, j)),
            pl.BlockSpec((tm, LANES), lambda i, j: (i, 0)),
        ],
        out_shape=[jax.ShapeDtypeStruct((t, n), BF16), jax.ShapeDtypeStruct((t, LANES), F32)],
        scratch_shapes=[pltpu.VMEM((tm, d), BF16)],
        compiler_params=_cparams(("parallel", "arbitrary")),
        name="proj",
    )(x2d, g, w, ws)


CMP_BATCH = 4


def _compress_kernel(k_ref, v_ref, w1p_ref, pe_ref, w1f_ref, w2p_ref, kg_ref, bd_ref,
                     kc_ref, vc_ref, raw_scr):
    nb, seq = k_ref.shape[0], k_ref.shape[1]
    nrow = kc_ref.shape[1]
    tail = raw_scr.shape[1] - seq
    for bi in range(nb):
        raw_scr[bi, seq:, :] = jnp.zeros((tail, LANES), F32)
    for kv, (src, dst) in enumerate(((k_ref, kc_ref), (v_ref, vc_ref))):
        for bi in range(nb):
            raw_scr[bi, 0:seq, :] = src[bi].astype(F32)
        c1 = jnp.dot(pe_ref[kv], w1f_ref[kv], preferred_element_type=F32)[0:1]
        pre = jnp.concatenate([c1, c1], axis=1)
        for l in range(CMP_LEN):
            rows = jnp.concatenate([raw_scr[bi, pl.ds(l, nrow, stride=CMP_STRIDE), :]
                                    for bi in range(nb)], axis=0).astype(BF16)
            pre = pre + jnp.dot(rows, w1p_ref[kv, l], preferred_element_type=F32)
        hid = jax.nn.gelu(pre).astype(BF16)
        o = jnp.dot(hid, w2p_ref[kv], preferred_element_type=F32)
        if kv == 0:
            o = _headnorm_pair(o, bd_ref[...], kg_ref[...])
        for bi in range(nb):
            dst[bi] = o[bi * nrow:(bi + 1) * nrow].astype(BF16)


def _compress(proj3, w1p, pe8, w1f, w2p, kg2, bd):
    b, s, _ = proj3.shape
    nrow = s // CMP_STRIDE
    nb = math.gcd(b, CMP_BATCH)
    full = lambda a: pl.BlockSpec(a.shape, lambda i: (0,) * a.ndim)
    return pl.pallas_call(
        _compress_kernel,
        grid=(b // nb,),
        in_specs=[pl.BlockSpec((nb, s, LANES), lambda i: (i, 0, COL_K_CMP)),
                  pl.BlockSpec((nb, s, LANES), lambda i: (i, 0, COL_V_CMP)),
                  full(w1p), full(pe8), full(w1f), full(w2p), full(kg2), full(bd)],
        out_specs=[pl.BlockSpec((nb, nrow, LANES), lambda i: (i, 0, 0)),
                   pl.BlockSpec((nb, nrow, LANES), lambda i: (i, 0, 0))],
        out_shape=[jax.ShapeDtypeStruct((b, nrow, LANES), BF16),
                   jax.ShapeDtypeStruct((b, nrow, LANES), BF16)],
        scratch_shapes=[pltpu.VMEM((nb, s + CMP_LEN, LANES), F32)],
        compiler_params=_cparams(("parallel",)),
        name="compress",
    )(proj3, proj3, w1p, pe8, w1f, w2p, kg2, bd)


def _fcum_kernel(s_ref, fb_ref, u_ref, o_ref, *, chunk):
    z = s_ref[0] + fb_ref[...]
    lf = jax.nn.log_sigmoid(z)
    lft = lf.T[SMALL_FF0:SMALL_FF0 + FOX_HEADS]
    seq = lft.shape[1]
    carry = jnp.zeros((FOX_HEADS, 1), F32)
    for c in range(seq // chunk):
        fc = _dot01_3(lft[:, c * chunk:(c + 1) * chunk], u_ref[...]) + carry
        o_ref[0, :, c * chunk:(c + 1) * chunk] = -fc
        carry = fc[:, chunk - 1:chunk]


def _fcum(small3, fb_row, u):
    b, s, _ = small3.shape
    chunk = u.shape[0]
    return pl.pallas_call(
        functools.partial(_fcum_kernel, chunk=chunk),
        grid=(b,),
        in_specs=[pl.BlockSpec((1, s, LANES), lambda i: (i, 0, 0)),
                  pl.BlockSpec((1, LANES), lambda i: (0, 0)),
                  pl.BlockSpec(u.shape, lambda i: (0, 0))],
        out_specs=pl.BlockSpec((1, FOX_HEADS, s), lambda i: (i, 0, 0)),
        out_shape=jax.ShapeDtypeStruct((b, FOX_HEADS, s), F32),
        compiler_params=_cparams(("parallel",)),
        name="fcum",
    )(small3, fb_row, u)


def _fox_kernel(q_ref, k_ref, v_ref, nf_ref, qg_ref, kg_ref, bd_ref, o_ref,
                kaug_scr, vt_scr, qt_scr, *, bounded):
    p = pl.program_id(1)
    bd = bd_ref[...]
    seq = k_ref.shape[1]
    nt = seq // BLK

    kaug_scr[:, 0:LANES] = _headnorm_pair(k_ref[0].astype(F32), bd, kg_ref[...]).astype(BF16)
    cb = [nf_ref[0, pl.ds(2 * p + e, 1), :] * L2E for e in range(2)]
    rows = []
    for e in range(2):
        rows.extend(_split3(cb[e]))
    feats = jnp.concatenate([_rows8(rows, seq), jnp.zeros((BOUND0 - 8, seq), F32),
                             _rows8([1.0] * 3, seq), jnp.zeros((LANES - BOUND0 - 8, seq), F32)],
                            axis=0)
    kaug_scr[:, LANES:KA] = feats.T.astype(BF16)
    vt = v_ref[0].astype(F32).T
    for e in range(2):
        _store_values_t(vt_scr.at[e], vt[e * HEAD_DIM:(e + 1) * HEAD_DIM], seq)

    qk_bound = _logit_bound(qg_ref, [kg_ref])
    qt = _headnorm_pair(q_ref[0].astype(F32), bd, qg_ref[...]).T
    for e in range(2):
        qt_scr[e] = _query_feats(qt, e, _rows8([0.0] * (3 * e) + [1.0] * 3, seq), None,
                                 qk_bound + cb[e] if bounded else None)

    vt_rows = [lambda ks, e=e: vt_scr[e, :, ks] for e in range(2)]
    acc = _flash(_causal_blocks(nt, kaug_scr, vt_rows, qt_scr), 2 * nt, HEAD_DIM, bounded)
    for qh in range(nt):
        ot = jnp.concatenate([acc[e * nt + qh][0:HEAD_DIM] / acc[e * nt + qh][HEAD_DIM:HEAD_DIM + 1]
                              for e in range(2)], axis=0)
        o_ref[0, qh * BLK:(qh + 1) * BLK, :] = ot.T.astype(o_ref.dtype)


def _const_spec(a):
    return pl.BlockSpec(a.shape, lambda bi, p: (0,) * a.ndim)


def _col_spec(s, c):
    return pl.BlockSpec((1, s, LANES), lambda bi, p: (bi, 0, c + p))


def _fox_part(proj3, negf, qg, kg, bd):
    s = proj3.shape[1]
    specs = [_col_spec(s, COL_FOX_Q), _col_spec(s, COL_FOX_K), _col_spec(s, COL_FOX_V),
             pl.BlockSpec((1, FOX_HEADS, s), lambda bi, p: (bi, 0, 0)),
             _const_spec(qg), _const_spec(kg), _const_spec(bd)]
    scratch = [pltpu.VMEM((s, KA), BF16), pltpu.VMEM((2, HEAD_DIM + VAUG, s), BF16),
               pltpu.VMEM((2, KA, s), BF16)]
    return _fox_kernel, [proj3, proj3, proj3, negf, qg, kg, bd], specs, scratch


def _alibi_at_query(sl_rows, seq):
    pos = lax.broadcasted_iota(jnp.int32, (1, seq), 1).astype(F32)
    return (sl_rows[0:1] + sl_rows[2:3] + sl_rows[4:5]) * pos


def _diff_kernel(q_ref, k_ref, v_ref, kx_ref, sl_ref, lam_ref, qg_ref, kg_ref, sg_ref, bd_ref,
                 o_ref, kaug_scr, vt_scr, qt_scr, *, lam_init, bounded):
    bd = bd_ref[...]
    seq = k_ref.shape[1]
    nt = seq // BLK

    kaug_scr[:, 0:LANES] = _headnorm_pair(k_ref[0].astype(F32), bd, kg_ref[...]).astype(BF16)
    kaug_scr[:, LANES:KA] = kx_ref[...]
    _store_values_t(vt_scr, v_ref[0].astype(F32).T, seq)
    sl = sl_ref[0]
    bound = _logit_bound(qg_ref, [kg_ref]) + _alibi_at_query(sl, seq) if bounded else None
    qt = _headnorm_pair(q_ref[0].astype(F32), bd, qg_ref[...]).T
    for e in range(2):
        qt_scr[e] = _query_feats(qt, e, sl, None, bound)

    vt_rows = [lambda ks: vt_scr[:, ks]] * 2
    acc = _flash(_causal_blocks(nt, kaug_scr, vt_rows, qt_scr), 2 * nt, LANES, bounded)

    lv = lam_ref[...]
    lam = (jnp.exp(jnp.sum(lv[0:1] * lv[1:2], axis=-1, keepdims=True))
           - jnp.exp(jnp.sum(lv[2:3] * lv[3:4], axis=-1, keepdims=True)) + lam_init)
    for qh in range(nt):
        a0, a1 = acc[qh], acc[nt + qh]
        ob = (a0[0:LANES] / a0[LANES:LANES + 1] - lam * (a1[0:LANES] / a1[LANES:LANES + 1])).T
        ob = ob * lax.rsqrt(jnp.mean(ob * ob, axis=-1, keepdims=True) + EPS) * sg_ref[...]
        o_ref[0, qh * BLK:(qh + 1) * BLK, :] = (ob * (1.0 - lam_init)).astype(o_ref.dtype)


def _diff_part(proj3, kx, slrows, lam_p, qg, kg, sg, bd, lam_init):
    s = proj3.shape[1]
    specs = [_col_spec(s, COL_DIFF_Q), _col_spec(s, COL_DIFF_K), _col_spec(s, COL_DIFF_V),
             _const_spec(kx), pl.BlockSpec((1, 8, s), lambda bi, p: (p, 0, 0)),
             _const_spec(lam_p), _const_spec(qg), _const_spec(kg), _const_spec(sg), _const_spec(bd)]
    scratch = [pltpu.VMEM((s, KA), BF16), pltpu.VMEM((LANES + VAUG, s), BF16),
               pltpu.VMEM((2, KA, s), BF16)]
    return (functools.partial(_diff_kernel, lam_init=lam_init),
            [proj3, proj3, proj3, kx, slrows, lam_p, qg, kg, sg, bd], specs, scratch)


def _nsa_select(imp_t, t0):
    ns, tq = imp_t.shape
    blk = lax.broadcasted_iota(jnp.int32, (ns, tq), 0)
    cur = (t0 + lax.broadcasted_iota(jnp.int32, (ns, tq), 1)) // SLC_BLOCK
    forced = (blk == 0) | (blk == cur) | (blk == cur - 1)
    score = jnp.where(blk <= cur, jnp.where(forced, imp_t + FORCE_BONUS, imp_t), NEG)
    sel = jnp.zeros((ns, tq), F32)
    blk_f = blk.astype(F32)
    for _ in range(min(SLC_TOPN, ns)):
        best = jnp.max(score, axis=0, keepdims=True)
        first = jnp.min(jnp.where(score == best, blk_f, float(ns)), axis=0, keepdims=True)
        take = blk_f == first
        sel = jnp.where(take, 1.0, sel)
        score = jnp.where(take, 2.0 * NEG, score)
    return sel


def _slope_rows(slope):
    hi, mid, lo = [float(np.float32(v)) for v in _np_split3(slope * L2E)]
    return [hi, hi, mid, mid, lo, lo]


def _np_split3(x):
    x = np.float32(x)
    hi = np.float32(x.astype(BF16))
    mid = np.float32((x - hi).astype(BF16))
    lo = np.float32((x - hi - mid).astype(BF16))
    return hi, mid, lo


CMP_ROWS = 512


def _nsa_prepare(qall_ref, kc_ref, vc_ref, ks_ref, vs_ref, kw_ref, vw_ref, kx_ref, cx_ref, nsl_ref,
                 qg_ref, ksg_ref, kwg_ref, bd_ref, ot_ref, ksaug_scr, kwaug_scr, vst_scr, vwt_scr,
                 notsel_scr, ocmp_scr, kcaug_scr, *, nc):
    bd = bd_ref[...]
    npair = NSA_HEADS // 2
    seq = ks_ref.shape[1]
    ksaug_scr[:, 0:LANES] = _headnorm_pair(ks_ref[0].astype(F32), bd, ksg_ref[...]).astype(BF16)
    kwaug_scr[:, 0:LANES] = _headnorm_pair(kw_ref[0].astype(F32), bd, kwg_ref[...]).astype(BF16)
    ksaug_scr[:, LANES:KA] = kx_ref[...]
    kwaug_scr[:, LANES:KA] = kx_ref[...]
    for v_ref, vt_scr in ((vs_ref, vst_scr), (vw_ref, vwt_scr)):
        vt = v_ref[0].astype(F32).T
        for e in range(2):
            _store_values_t(vt_scr.at[e], vt[e * HEAD_DIM:(e + 1) * HEAD_DIM], seq)
    kcaug_scr[:, 0:LANES] = kc_ref[0]
    kcaug_scr[:, LANES:KA] = cx_ref[...]
    vct = vc_ref[0].astype(F32).T.astype(BF16)

    nidx = lax.broadcasted_iota(jnp.int32, (LANES, CMP_ROWS), 0)
    cend = nidx * CMP_STRIDE + (CMP_LEN - 1)

    def chunk(c, carry):
        t0 = pl.multiple_of(c * CMP_ROWS, CMP_ROWS)
        tq = t0 + lax.broadcasted_iota(jnp.int32, (LANES, CMP_ROWS), 1)
        cmask = (tq >= cend) & (nidx < nc)
        psum = [jnp.zeros((LANES, CMP_ROWS), F32), jnp.zeros((LANES, CMP_ROWS), F32)]
        scores = []
        for r in range(npair):
            qt = _headnorm_pair(qall_ref[0, pl.ds(t0, CMP_ROWS), r * LANES:(r + 1) * LANES].astype(F32),
                                bd, qg_ref[...]).T
            for e in range(2):
                qf = _query_feats(qt, e, nsl_ref[r, e, :, 0:CMP_ROWS], None, None)
                scores.append(jnp.dot(kcaug_scr[...], qf, preferred_element_type=F32))
        probs = []
        for k, s in enumerate(scores):
            s = jnp.where(cmask, s, NEG)
            m = jnp.max(s, axis=0, keepdims=True)
            pe = jnp.where(cmask, jnp.exp2(s - m), 0.0)
            l = jnp.sum(pe, axis=0, keepdims=True)
            pn = pe / jnp.where(l > 0.0, l, 1.0)
            psum[k % 2] = psum[k % 2] + pn
            probs.append(pn.astype(BF16))
        for k, p in enumerate(probs):
            hs = slice((k % 2) * HEAD_DIM, (k % 2 + 1) * HEAD_DIM)
            ocmp_scr[k // 2, hs, pl.ds(t0, CMP_ROWS)] = jnp.dot(vct[hs], p,
                                                                preferred_element_type=F32)
        for g in range(NSA_GROUPS):
            hi, mid, lo = _split3(psum[g])
            imp_t = (jnp.dot(ot_ref[...], hi.astype(BF16), preferred_element_type=F32)
                     + jnp.dot(ot_ref[...], mid.astype(BF16), preferred_element_type=F32)
                     + jnp.dot(ot_ref[...], lo.astype(BF16), preferred_element_type=F32))
            notsel_scr[g, :, pl.ds(t0, CMP_ROWS)] = (_nsa_select(imp_t, t0) - 1.0) * -NEG
        return carry

    lax.fori_loop(0, seq // CMP_ROWS, chunk, 0)


def _nsa_kernel(qall_ref, q_ref, sm_ref, kc_ref, vc_ref, ks_ref, vs_ref, kw_ref, vw_ref, kx_ref,
                cx_ref, nsl_ref, slr_ref, qg_ref, ksg_ref, kwg_ref, bd_ref, ot_ref, egt_ref, o_ref,
                ksaug_scr, kwaug_scr, vst_scr, vwt_scr, qts_scr, qtw_scr, notsel_scr, ocmp_scr,
                kcaug_scr, *, nc, bounded):
    r = pl.program_id(1)
    bd = bd_ref[...]
    seq = ks_ref.shape[1]
    nt = seq // BLK

    @pl.when(r == 0)
    def _():
        _nsa_prepare(qall_ref, kc_ref, vc_ref, ks_ref, vs_ref, kw_ref, vw_ref, kx_ref, cx_ref,
                     nsl_ref, qg_ref, ksg_ref, kwg_ref, bd_ref, ot_ref, ksaug_scr, kwaug_scr,
                     vst_scr, vwt_scr, notsel_scr, ocmp_scr, kcaug_scr, nc=nc)

    qk_bound = _logit_bound(qg_ref, [ksg_ref, kwg_ref])
    qt = _headnorm_pair(q_ref[0].astype(F32), bd, qg_ref[...]).T
    for e in range(2):
        sl = slr_ref[0, e]
        bound = qk_bound + _alibi_at_query(sl, seq) if bounded else None
        qts_scr[e] = _query_feats(qt, e, sl, notsel_scr[e], bound)
        qtw_scr[e] = _query_feats(qt, e, sl, None, bound)

    key = lax.broadcasted_iota(jnp.int32, (BLK, BLK), 0)
    qry = lax.broadcasted_iota(jnp.int32, (BLK, BLK), 1)
    diag = key <= qry
    above = key > qry
    vs_rows = [lambda ks, e=e: vst_scr[e, :, ks] for e in range(2)]
    slc = _causal_blocks(nt, ksaug_scr, vs_rows, qts_scr)

    def win_block(j, qh, e, mask):
        ks = slice(j * BLK, (j + 1) * BLK)
        qs = slice(qh * BLK, (qh + 1) * BLK)
        return (lambda: kwaug_scr[ks, :], lambda: vwt_scr[e, :, ks], lambda: qtw_scr[e, :, qs],
                mask, 2 * nt + e * nt + qh)

    blocks = []
    pos = 0
    for j in range(nt):
        n_j = 2 * (nt - j)
        blocks.extend(slc[pos:pos + n_j])
        pos += n_j
        blocks.extend(win_block(j, j, e, diag) for e in range(2))
        if j > 0:
            blocks.extend(win_block(j - 1, j, e, above) for e in range(2))
    acc = _flash(blocks, 4 * nt, HEAD_DIM, bounded)
    normed = lambda a: a[0:HEAD_DIM] / a[HEAD_DIM:HEAD_DIM + 1]

    gate_t = jax.nn.sigmoid(sm_ref[0]).T
    g_cmp_t, g_slc_t, g_win_t = [_dot01_left(egt_ref[c], gate_t) for c in range(3)]
    for qh in range(nt):
        cols = slice(qh * BLK, (qh + 1) * BLK)
        o_s = jnp.concatenate([normed(acc[e * nt + qh]) for e in range(2)], axis=0)
        o_w = jnp.concatenate([normed(acc[2 * nt + e * nt + qh]) for e in range(2)], axis=0)
        mix = (g_cmp_t[:, cols] * ocmp_scr[r, :, cols] + g_slc_t[:, cols] * o_s
               + g_win_t[:, cols] * o_w)
        o_ref[0, cols, :] = mix.T.astype(o_ref.dtype)


def _nsa_part(proj3, small3, kc, vc, kx, cx, nsl, qg, ksg, kwg, bd, ot, egt):
    s = proj3.shape[1]
    npair = NSA_HEADS // 2
    ns = s // SLC_BLOCK
    nc = (s - CMP_LEN) // CMP_STRIDE + 1
    row = lambda a: pl.BlockSpec((1,) + a.shape[1:], lambda bi, p: (bi,) + (0,) * (a.ndim - 1))
    kvspec = lambda c: pl.BlockSpec((1, s, LANES), lambda bi, p: (bi, 0, c))
    specs = [pl.BlockSpec((1, s, npair * LANES), lambda bi, p: (bi, 0, COL_NSA_Q // npair)),
             _col_spec(s, COL_NSA_Q), row(small3), row(kc), row(vc),
             kvspec(COL_K_SLC), kvspec(COL_V_SLC), kvspec(COL_K_WIN), kvspec(COL_V_WIN),
             _const_spec(kx), _const_spec(cx), _const_spec(nsl),
             pl.BlockSpec((1, 2, 8, s), lambda bi, p: (p, 0, 0, 0)),
             _const_spec(qg), _const_spec(ksg), _const_spec(kwg), _const_spec(bd), _const_spec(ot),
             pl.BlockSpec((3, LANES, LANES), lambda bi, p: (0, p, 0))]
    scratch = [pltpu.VMEM((s, KA), BF16), pltpu.VMEM((s, KA), BF16),
               pltpu.VMEM((2, HEAD_DIM + VAUG, s), BF16), pltpu.VMEM((2, HEAD_DIM + VAUG, s), BF16),
               pltpu.VMEM((2, KA, s), BF16), pltpu.VMEM((2, KA, s), BF16),
               pltpu.VMEM((NSA_GROUPS, ns, s), F32), pltpu.VMEM((npair, LANES, s), F32),
               pltpu.VMEM((kc.shape[1], KA), BF16)]
    arrays = [proj3, proj3, small3, kc, vc, proj3, proj3, proj3, proj3, kx, cx, nsl, nsl, qg, ksg,
              kwg, bd, ot, egt]
    return functools.partial(_nsa_kernel, nc=nc), arrays, specs, scratch


def _attention_kernel(*refs, layout, bounded):
    pos = 0
    inputs = []
    for _, n_in, _ in layout:
        inputs.append(refs[pos:pos + n_in])
        pos += n_in
    outputs = refs[pos:pos + len(layout)]
    pos += len(layout)
    for (fn, _, n_scr), ins, out in zip(layout, inputs, outputs):
        fn(*ins, out, *refs[pos:pos + n_scr], bounded=bounded)
        pos += n_scr


def _attention(layout, specs, scratch, b, s, *arrays, bounded):
    out = pl.BlockSpec((1, s, LANES), lambda bi, p: (bi, 0, p))
    npair = NSA_HEADS // 2
    return pl.pallas_call(
        functools.partial(_attention_kernel, layout=layout, bounded=bounded),
        grid=(b, npair),
        in_specs=specs,
        out_specs=[out] * len(layout),
        out_shape=[jax.ShapeDtypeStruct((b, s, npair * LANES), BF16)] * len(layout),
        scratch_shapes=scratch,
        compiler_params=_cparams(("parallel", "arbitrary")),
        name="attention",
    )(*arrays)


def _merge_kernel(oa_ref, ob_ref, oc_ref, g0_ref, g1_ref, g2_ref, h_ref, wbr_ref, wo_ref, o_ref):
    merged = None
    for c, (o_r, g_r) in enumerate(((oa_ref, g0_ref), (ob_ref, g1_ref), (oc_ref, g2_ref))):
        y = jnp.dot(o_r[...], wbr_ref[c], preferred_element_type=F32)
        term = jax.nn.sigmoid(g_r[...].astype(F32)) * y
        merged = term if merged is None else merged + term
    o_ref[...] = h_ref[...] + jnp.dot(merged.astype(BF16), wo_ref[...], preferred_element_type=F32)


def _merge(oa, ob, oc, proj2, h2, wbr, wo, *, tm=1024):
    t, d = h2.shape
    bw = oa.shape[1]
    row = lambda w: pl.BlockSpec((tm, w), lambda i: (i, 0))
    gate = lambda c: pl.BlockSpec((tm, d), lambda i: (i, COL_MG * LANES // d + c))
    resident = lambda a: pl.BlockSpec(a.shape, lambda i: (0,) * a.ndim, pipeline_mode=pl.Buffered(1))
    return pl.pallas_call(
        _merge_kernel,
        grid=(t // tm,),
        in_specs=[row(bw), row(bw), row(bw), gate(0), gate(1), gate(2), row(d),
                  resident(wbr), resident(wo)],
        out_specs=row(d),
        out_shape=jax.ShapeDtypeStruct((t, d), F32),
        compiler_params=_cparams(("parallel",)),
        name="merge",
    )(oa, ob, oc, proj2, proj2, proj2, h2, wbr, wo)


HALO = 16


def _ffn_kernel(x_ref, xh_ref, g_ref, wup_ref, cw_ref, cb_ref, wd_ref, o_ref, u_scr,
                *, tm, tf, tiles_per_seq):
    i = pl.program_id(0)
    dff = wd_ref.shape[0]

    def norm(x):
        return (x * lax.rsqrt(jnp.mean(x * x, axis=-1, keepdims=True) + EPS)
                * g_ref[...]).astype(BF16)

    a_main = norm(x_ref[...])
    a_all = jnp.concatenate([norm(xh_ref[...]), a_main], axis=0)
    seq_start = (i % tiles_per_seq) == 0
    acts = []
    for c in range(dff // tf):
        cols = slice(c * tf, (c + 1) * tf)
        u = jnp.dot(a_all, wup_ref[:, cols], preferred_element_type=F32)
        rows = lax.broadcasted_iota(jnp.int32, u.shape, 0)
        u_scr[...] = jnp.where((rows < HALO) & seq_start, 0.0, u)
        uc = cb_ref[:, cols]
        for t in range(CONV_W):
            uc = uc + cw_ref[t:t + 1, cols] * u_scr[pl.ds(HALO - (CONV_W - 1) + t, tm), :]
        gt = jnp.dot(a_main, wup_ref[:, dff + c * tf:dff + (c + 1) * tf],
                     preferred_element_type=F32)
        acts.append((jax.nn.gelu(uc) * gt).astype(BF16))
    act = jnp.concatenate(acts, axis=1)
    o_ref[...] = x_ref[...] + jnp.dot(act, wd_ref[...], preferred_element_type=F32)


def _ffn(h2, g, wup, cw, cb, wd, seq, *, tm=512, tf=1408):
    t, d = h2.shape
    resident = lambda a: pl.BlockSpec(a.shape, lambda i: (0,) * a.ndim, pipeline_mode=pl.Buffered(1))
    return pl.pallas_call(
        functools.partial(_ffn_kernel, tm=tm, tf=tf, tiles_per_seq=seq // tm),
        grid=(t // tm,),
        in_specs=[
            pl.BlockSpec((tm, d), lambda i: (i, 0)),
            pl.BlockSpec((HALO, d), lambda i: (jnp.maximum(i * (tm // HALO) - 1, 0), 0)),
            resident(g), resident(wup), resident(cw), resident(cb), resident(wd),
        ],
        out_specs=pl.BlockSpec((tm, d), lambda i: (i, 0)),
        out_shape=jax.ShapeDtypeStruct((t, d), F32),
        scratch_shapes=[pltpu.VMEM((HALO + tm, tf), F32)],
        compiler_params=_cparams(("parallel",)),
        name="ffn",
    )(h2, h2, g, wup, cw, cb, wd)


def _nsa_head_order():
    hpg = NSA_HEADS // NSA_GROUPS
    return [h for r in range(hpg) for h in (r, hpg + r)]


def _constants(seq):
    ns = seq // SLC_BLOCK
    nc = (seq - CMP_LEN) // CMP_STRIDE + 1
    assert MASK0 + ns <= LANES and MASK0 >= 8 and N_BIAS <= 8
    bd = np.kron(np.eye(2), np.ones((HEAD_DIM, HEAD_DIM))).astype(np.float32)
    c_start = np.arange(LANES) * CMP_STRIDE
    s_start = np.arange(ns) * SLC_BLOCK
    ot = ((c_start[None, :] < s_start[:, None] + SLC_BLOCK)
          & (c_start[None, :] + CMP_LEN > s_start[:, None])
          & (np.arange(LANES)[None, :] < nc)).astype(np.float32)
    pos = np.arange(seq)
    kx = np.zeros((seq, LANES), np.float32)
    for k in range(N_BIAS):
        kx[:, k] = (pos // 8) * 8 if k % 2 == 0 else pos % 8
    kx[pos, MASK0 + pos // SLC_BLOCK] = 1.0
    kx[:, BOUND0:BOUND0 + 3] = 1.0
    cx = kx[np.minimum(np.arange(LANES) * CMP_STRIDE + (CMP_LEN - 1), seq - 1)]
    order = _nsa_head_order()
    eg = np.zeros((3, LANES, NSA_HEADS * HEAD_DIM), np.float32)
    for c in range(3):
        for slot, h in enumerate(order):
            eg[c, SMALL_GATE0 + c * NSA_HEADS + h, slot * HEAD_DIM:(slot + 1) * HEAD_DIM] = 1.0
    u = (np.arange(BLK)[:, None] <= np.arange(BLK)[None, :]).astype(np.float32)
    dsl = np.zeros((DIFF_HEADS, 8, seq), np.float32)
    for h in range(DIFF_HEADS):
        rows = _slope_rows(2.0 ** (-8.0 * (h + 1) / DIFF_HEADS))
        dsl[h, :N_BIAS, :] = np.asarray(rows, np.float32)[:, None]
    nsl = np.zeros((NSA_HEADS // 2, 2, 8, seq), np.float32)
    for slot, h in enumerate(order):
        rows = _slope_rows(2.0 ** (-8.0 * (h + 1) / NSA_HEADS))
        nsl[slot // 2, slot % 2, :N_BIAS, :] = np.asarray(rows, np.float32)[:, None]
    as_bf = lambda a: jnp.asarray(a, BF16)
    return dict(bd=as_bf(bd), ot=as_bf(ot), kx=as_bf(kx), cx=as_bf(cx),
                egt=as_bf(np.transpose(eg, (0, 2, 1))), u=as_bf(u), dsl=jnp.asarray(dsl, F32),
                nsl=jnp.asarray(nsl, F32))


def _pack_w_in(w):
    d = w.shape[0]
    hd = HEAD_DIM
    w = w.astype(BF16)
    sizes = [512, 768, 24, 6 * 512, 8, 3 * d]
    offs = np.concatenate([[0], np.cumsum(sizes)])
    nq, nkv, ngate, six, ff, mg = [w[:, offs[k]:offs[k + 1]] for k in range(6)]
    hpg = NSA_HEADS // NSA_GROUPS
    nq = jnp.swapaxes(nq.reshape(d, NSA_GROUPS, hpg, hd), 1, 2).reshape(d, NSA_HEADS * hd)
    assert [g * hpg + r for r in range(hpg) for g in range(NSA_GROUPS)] == _nsa_head_order()
    pad = jnp.zeros((d, (PROJ_UNITS - COL_V_CMP - 1) * LANES), BF16)
    main = jnp.concatenate([mg, nq, six, nkv[:, 2 * LANES:], nkv[:, :2 * LANES], pad], axis=1)
    small = jnp.concatenate([ngate, ff, jnp.zeros((d, LANES - 32), BF16)], axis=1)
    return main, small


def kernel(x, attn_norm_g, w_in, nsa_q_g, nsa_k_g, cmp_pe, cmp_w1, cmp_w2, diff_q_g, diff_k_g,
           diff_lam, diff_subln_g, fox_q_g, fox_k_g, fox_b, w_br, w_o, ffn_norm_g, w_up, conv_w,
           conv_b, w_down):
    b, s, d = x.shape
    depth = w_in.shape[0]
    hd = HEAD_DIM
    qscale = hd ** -0.5 * L2E
    cst = _constants(s)
    rows16 = s // CMP_STRIDE
    assert rows16 == LANES and s % CMP_ROWS == 0, "NSA kernel keeps all compressed blocks in one 128-lane tile"
    order = _nsa_head_order()
    tile2 = lambda g: jnp.tile(g, 2).reshape(1, LANES).astype(F32)

    h = x.reshape(b * s, d)
    for l in range(depth):
        wmain, wsmall = _pack_w_in(w_in[l])
        proj, small = _proj(h, attn_norm_g[l].reshape(1, d), wmain, wsmall)
        proj3 = proj.reshape(b, s, PROJ_UNITS * LANES)
        small3 = small.reshape(b, s, LANES)

        w1 = cmp_w1[l].astype(BF16).reshape(2, CMP_LEN, hd, 2 * hd)
        z1 = jnp.zeros_like(w1)
        w1p = jnp.concatenate([jnp.concatenate([w1, z1], axis=3),
                               jnp.concatenate([z1, w1], axis=3)], axis=2)
        w2 = cmp_w2[l].astype(BF16)
        z2 = jnp.zeros_like(w2)
        w2p = jnp.concatenate([jnp.concatenate([w2, z2], axis=2),
                               jnp.concatenate([z2, w2], axis=2)], axis=1)
        pe8 = jnp.broadcast_to(cmp_pe[l].reshape(2, 1, CMP_LEN * hd), (2, 8, CMP_LEN * hd)).astype(BF16)
        kc, vc = _compress(proj3, w1p, pe8, cmp_w1[l].astype(BF16), w2p, tile2(nsa_k_g[l, 0]),
                           cst["bd"])

        fb_row = jnp.zeros((1, LANES), F32).at[0, SMALL_FF0:SMALL_FF0 + FOX_HEADS].set(fox_b[l])
        negf = _fcum(small3, fb_row, cst["u"])

        nsa_qg, diff_qg, fox_qg = nsa_q_g[l] * qscale, diff_q_g[l] * qscale, fox_q_g[l] * qscale
        lam_init = 0.8 - 0.6 * math.exp(-0.3 * l)
        parts = [
            _nsa_part(proj3, small3, kc, vc, cst["kx"], cst["cx"], cst["nsl"], tile2(nsa_qg),
                      tile2(nsa_k_g[l, 1]), tile2(nsa_k_g[l, 2]), cst["bd"], cst["ot"], cst["egt"]),
            _diff_part(proj3, cst["kx"], cst["dsl"], diff_lam[l], tile2(diff_qg),
                       tile2(diff_k_g[l]), diff_subln_g[l].reshape(1, LANES), cst["bd"], lam_init),
            _fox_part(proj3, negf, tile2(fox_qg), tile2(fox_k_g[l]), cst["bd"]),
        ]
        layout = tuple((fn, len(arrs), len(scr)) for fn, arrs, _, scr in parts)
        specs = [sp for _, _, sps, _ in parts for sp in sps]
        scratch = [sc for _, _, _, scs in parts for sc in scs]
        arrays = [a for _, arrs, _, _ in parts for a in arrs]
        amax = lambda g: jnp.max(jnp.abs(g))
        qk_max = functools.reduce(jnp.maximum, [
            amax(nsa_qg) * jnp.maximum(amax(nsa_k_g[l, 1]), amax(nsa_k_g[l, 2])),
            amax(diff_qg) * amax(diff_k_g[l]), amax(fox_qg) * amax(fox_k_g[l])])
        call = functools.partial(_attention, layout, specs, scratch, b, s)
        o_a, o_b, o_c = lax.cond(2.0 * HEAD_DIM * BOUND_SLACK * qk_max <= BOUND_LIMIT,
                                 functools.partial(call, bounded=True),
                                 functools.partial(call, bounded=False), *arrays)

        wbr = w_br[l]
        wbr_a = jnp.concatenate([wbr[0, hh * hd:(hh + 1) * hd] for hh in order], axis=0)
        wbr_p = jnp.stack([wbr_a, wbr[1], wbr[2]]).astype(BF16)
        bw = NSA_HEADS * hd
        h = _merge(o_a.reshape(b * s, bw), o_b.reshape(b * s, bw), o_c.reshape(b * s, bw),
                   proj, h, wbr_p, w_o[l].astype(BF16))
        h = _ffn(h, ffn_norm_g[l].reshape(1, d), w_up[l].astype(BF16), conv_w[l],
                 conv_b[l].reshape(1, -1), w_down[l].astype(BF16), s)
    return h.reshape(b, s, d)
```

```python
import functools
import math

import numpy as np
import jax
import jax.numpy as jnp
from jax import lax
from jax.experimental import pallas as pl
from jax.experimental.pallas import tpu as pltpu

F32 = jnp.float32
BF16 = jnp.bfloat16

HEAD_DIM = 64
NSA_HEADS = 8
NSA_GROUPS = 2
CMP_LEN = 32
CMP_STRIDE = 16
SLC_BLOCK = 64
SLC_TOPN = 8
WINDOW = 256
FORCE_BONUS = 1.0e4
DIFF_HEADS = 4
FOX_HEADS = 8
CONV_W = 3
EPS = 1e-6
NEG = -1.0e30
L2E = 1.4426950408889634

LANES = 128
BLK = 256
KA = 2 * LANES
N_BIAS = 6
MASK0 = 8
BOUND0 = 40
VAUG = 16
BOUND_SLACK = 1.02
BOUND_LIMIT = 100.0
VMEM_LIMIT = 56 * 1024 * 1024

COL_MG = 0
COL_NSA_Q = 24
COL_DIFF_Q, COL_DIFF_K, COL_DIFF_V = 28, 32, 36
COL_FOX_Q, COL_FOX_K, COL_FOX_V = 40, 44, 48
COL_K_SLC, COL_V_SLC, COL_K_WIN, COL_V_WIN, COL_K_CMP, COL_V_CMP = 52, 53, 54, 55, 56, 57
PROJ_UNITS = 60
SMALL_GATE0 = 0
SMALL_FF0 = 24


def _cparams(sem):
    return pltpu.CompilerParams(dimension_semantics=sem, vmem_limit_bytes=VMEM_LIMIT)


def _split3(x):
    hi = x.astype(BF16).astype(F32)
    r1 = x - hi
    mid = r1.astype(BF16).astype(F32)
    lo = (r1 - mid).astype(BF16).astype(F32)
    return hi, mid, lo


def _dot01(x, m):
    hi = x.astype(BF16)
    lo = (x - hi.astype(F32)).astype(BF16)
    return (jnp.dot(hi, m, preferred_element_type=F32)
            + jnp.dot(lo, m, preferred_element_type=F32))


def _dot01_left(m, x):
    hi = x.astype(BF16)
    lo = (x - hi.astype(F32)).astype(BF16)
    return (jnp.dot(m, hi, preferred_element_type=F32)
            + jnp.dot(m, lo, preferred_element_type=F32))


def _dot01_3(x, m):
    hi, mid, lo = _split3(x)
    return (jnp.dot(hi.astype(BF16), m, preferred_element_type=F32)
            + jnp.dot(mid.astype(BF16), m, preferred_element_type=F32)
            + jnp.dot(lo.astype(BF16), m, preferred_element_type=F32))


def _qk(q, k):
    return lax.dot_general(q, k, (((1,), (1,)), ((), ())), preferred_element_type=F32)


def _headnorm_pair(x, bd, gain):
    ss = jnp.dot((x * x).astype(BF16), bd, preferred_element_type=F32)
    return x * lax.rsqrt(ss * (1.0 / HEAD_DIM) + EPS) * gain


def _rows8(vals, width):
    row = lax.broadcasted_iota(jnp.int32, (8, width), 0)
    out = jnp.zeros((8, width), F32)
    for k, v in enumerate(vals):
        out = jnp.where(row == k, v, out)
    return out


def _query_feats(qt, half, bias8, mask_rows, bound_row):
    tq = qt.shape[1]
    row = lax.broadcasted_iota(jnp.int32, (LANES, tq), 0)
    keep = (row < HEAD_DIM) if half == 0 else (row >= HEAD_DIM)
    nmask = BOUND0 - MASK0
    parts = [jnp.where(keep, qt, 0.0), bias8,
             mask_rows if mask_rows is not None else jnp.zeros((nmask, tq), F32),
             _rows8(_split3(-bound_row), tq) if bound_row is not None else jnp.zeros((8, tq), F32),
             jnp.zeros((LANES - BOUND0 - 8, tq), F32)]
    return jnp.concatenate(parts, axis=0).astype(BF16)


def _logit_bound(qg_ref, kg_refs):
    kmax = None
    for kg_ref in kg_refs:
        k = jnp.max(jnp.abs(kg_ref[...]), axis=-1, keepdims=True)
        kmax = k if kmax is None else jnp.maximum(kmax, k)
    return jnp.max(jnp.abs(qg_ref[...]), axis=-1, keepdims=True) * kmax * (HEAD_DIM * BOUND_SLACK)


def _store_values_t(vt_view, rows, seq):
    dv = rows.shape[0]
    vt_view[0:dv, :] = rows.astype(BF16)
    vt_view[dv:dv + 8, :] = _rows8([1.0], seq).astype(BF16)
    vt_view[dv + 8:dv + VAUG, :] = jnp.zeros((VAUG - 8, seq), BF16)


QK_AHEAD = 2


def _flash(blocks, n_chain, dv, bounded):
    acc = [jnp.zeros((dv + VAUG, BLK), F32) for _ in range(n_chain)]
    mx = [jnp.full((1, BLK), NEG, F32) for _ in range(n_chain)]
    scores = {}

    def issue(k):
        if k < len(blocks):
            scores[k] = jnp.dot(blocks[k][0](), blocks[k][2](), preferred_element_type=F32)

    for k in range(2 * QK_AHEAD):
        issue(k)
    for k0 in range(0, len(blocks), 2):
        issue(k0 + 2 * QK_AHEAD)
        issue(k0 + 2 * QK_AHEAD + 1)
        pending = []
        for k in range(k0, min(k0 + 2, len(blocks))):
            _, vt, _, mask, chain = blocks[k]
            s = scores.pop(k)
            if mask is not None:
                s = jnp.where(mask, s, NEG)
            if bounded:
                pending.append((vt, chain, None, jnp.exp2(s).astype(BF16)))
            else:
                m_new = jnp.maximum(mx[chain], jnp.max(s, axis=0, keepdims=True))
                alpha = jnp.exp2(mx[chain] - m_new)
                mx[chain] = m_new
                pending.append((vt, chain, alpha, jnp.exp2(s - m_new).astype(BF16)))
        assert len({c for _, c, _, _ in pending}) == len(pending), "a pair must not share a chain"
        for vt, chain, alpha, p in pending:
            prev = acc[chain] if alpha is None else alpha * acc[chain]
            acc[chain] = prev + jnp.dot(vt(), p, preferred_element_type=F32)
    return acc


def _causal_blocks(nt, kaug_scr, vt_rows, qt_scr):
    key = lax.broadcasted_iota(jnp.int32, (BLK, BLK), 0)
    qry = lax.broadcasted_iota(jnp.int32, (BLK, BLK), 1)
    diag = key <= qry
    blocks = []
    for j in range(nt):
        ks = slice(j * BLK, (j + 1) * BLK)
        for qh in range(j, nt):
            qs = slice(qh * BLK, (qh + 1) * BLK)
            for e in range(2):
                blocks.append((lambda ks=ks: kaug_scr[ks, :],
                               lambda e=e, ks=ks: vt_rows[e](ks),
                               lambda e=e, qs=qs: qt_scr[e, :, qs],
                               diag if qh == j else None, e * nt + qh))
    return blocks


def _proj_kernel(x_ref, g_ref, w_ref, ws_ref, o_ref, os_ref, a_scr):
    j = pl.program_id(1)

    @pl.when(j == 0)
    def _():
        x = x_ref[...]
        inv = lax.rsqrt(jnp.mean(x * x, axis=-1, keepdims=True) + EPS)
        a = (x * inv * g_ref[...]).astype(BF16)
        a_scr[...] = a
        os_ref[...] = jnp.dot(a, ws_ref[...], preferred_element_type=F32)

    o_ref[...] = jnp.dot(a_scr[...], w_ref[...], preferred_element_type=F32).astype(o_ref.dtype)


def _proj(x2d, g, w, ws, *, tm=1024, tn=2560):
    t, d = x2d.shape
    n = w.shape[1]
    return pl.pallas_call(
        _proj_kernel,
        grid=(t // tm, n // tn),
        in_specs=[
            pl.BlockSpec((tm, d), lambda i, j: (i, 0)),
            pl.BlockSpec((1, d), lambda i, j: (0, 0)),
            pl.BlockSpec((d, tn), lambda i, j: (0, j)),
            pl.BlockSpec((d, LANES), lambda i, j: (0, 0)),
        ],
        out_specs=[
            pl.BlockSpec((tm, tn), lambda i, j: (i, j)),
            pl.BlockSpec((tm, LANES), lambda i, j: (i, 0)),
        ],
        out_shape=[jax.ShapeDtypeStruct((t, n), BF16), jax.ShapeDtypeStruct((t, LANES), F32)],
        scratch_shapes=[pltpu.VMEM((tm, d), BF16)],
        compiler_params=_cparams(("parallel", "arbitrary")),
        name="proj",
    )(x2d, g, w, ws)


CMP_BATCH = 4


def _compress_kernel(k_ref, v_ref, w1p_ref, pe_ref, w1f_ref, w2p_ref, kg_ref, bd_ref,
                     kc_ref, vc_ref, raw_scr):
    nb, seq = k_ref.shape[0], k_ref.shape[1]
    nrow = kc_ref.shape[1]
    tail = raw_scr.shape[1] - seq
    for bi in range(nb):
        raw_scr[bi, seq:, :] = jnp.zeros((tail, LANES), F32)
    for kv, (src, dst) in enumerate(((k_ref, kc_ref), (v_ref, vc_ref))):
        for bi in range(nb):
            raw_scr[bi, 0:seq, :] = src[bi].astype(F32)
        c1 = jnp.dot(pe_ref[kv], w1f_ref[kv], preferred_element_type=F32)[0:1]
        pre = jnp.concatenate([c1, c1], axis=1)
        for l in range(CMP_LEN):
            rows = jnp.concatenate([raw_scr[bi, pl.ds(l, nrow, stride=CMP_STRIDE), :]
                                    for bi in range(nb)], axis=0).astype(BF16)
            pre = pre + jnp.dot(rows, w1p_ref[kv, l], preferred_element_type=F32)
        hid = jax.nn.gelu(pre).astype(BF16)
        o = jnp.dot(hid, w2p_ref[kv], preferred_element_type=F32)
        if kv == 0:
            o = _headnorm_pair(o, bd_ref[...], kg_ref[...])
        for bi in range(nb):
            dst[bi] = o[bi * nrow:(bi + 1) * nrow].astype(BF16)


def _compress(proj3, w1p, pe8, w1f, w2p, kg2, bd):
    b, s, _ = proj3.shape
    nrow = s // CMP_STRIDE
    nb = math.gcd(b, CMP_BATCH)
    full = lambda a: pl.BlockSpec(a.shape, lambda i: (0,) * a.ndim)
    return pl.pallas_call(
        _compress_kernel,
        grid=(b // nb,),
        in_specs=[pl.BlockSpec((nb, s, LANES), lambda i: (i, 0, COL_K_CMP)),
                  pl.BlockSpec((nb, s, LANES), lambda i: (i, 0, COL_V_CMP)),
                  full(w1p), full(pe8), full(w1f), full(w2p), full(kg2), full(bd)],
        out_specs=[pl.BlockSpec((nb, nrow, LANES), lambda i: (i, 0, 0)),
                   pl.BlockSpec((nb, nrow, LANES), lambda i: (i, 0, 0))],
        out_shape=[jax.ShapeDtypeStruct((b, nrow, LANES), BF16),
                   jax.ShapeDtypeStruct((b, nrow, LANES), BF16)],
        scratch_shapes=[pltpu.VMEM((nb, s + CMP_LEN, LANES), F32)],
        compiler_params=_cparams(("parallel",)),
        name="compress",
    )(proj3, proj3, w1p, pe8, w1f, w2p, kg2, bd)


def _fcum_kernel(s_ref, fb_ref, u_ref, o_ref, *, chunk):
    z = s_ref[0] + fb_ref[...]
    lf = jax.nn.log_sigmoid(z)
    lft = lf.T[SMALL_FF0:SMALL_FF0 + FOX_HEADS]
    seq = lft.shape[1]
    carry = jnp.zeros((FOX_HEADS, 1), F32)
    for c in range(seq // chunk):
        fc = _dot01_3(lft[:, c * chunk:(c + 1) * chunk], u_ref[...]) + carry
        o_ref[0, :, c * chunk:(c + 1) * chunk] = -fc
        carry = fc[:, chunk - 1:chunk]


def _fcum(small3, fb_row, u):
    b, s, _ = small3.shape
    chunk = u.shape[0]
    return pl.pallas_call(
        functools.partial(_fcum_kernel, chunk=chunk),
        grid=(b,),
        in_specs=[pl.BlockSpec((1, s, LANES), lambda i: (i, 0, 0)),
                  pl.BlockSpec((1, LANES), lambda i: (0, 0)),
                  pl.BlockSpec(u.shape, lambda i: (0, 0))],
        out_specs=pl.BlockSpec((1, FOX_HEADS, s), lambda i: (i, 0, 0)),
        out_shape=jax.ShapeDtypeStruct((b, FOX_HEADS, s), F32),
        compiler_params=_cparams(("parallel",)),
        name="fcum",
    )(small3, fb_row, u)


def _fox_kernel(q_ref, k_ref, v_ref, nf_ref, qg_ref, kg_ref, bd_ref, o_ref,
                kaug_scr, vt_scr, qt_scr, *, bounded):
    p = pl.program_id(1)
    bd = bd_ref[...]
    seq = k_ref.shape[1]
    nt = seq // BLK

    kaug_scr[:, 0:LANES] = _headnorm_pair(k_ref[0].astype(F32), bd, kg_ref[...]).astype(BF16)
    cb = [nf_ref[0, pl.ds(2 * p + e, 1), :] * L2E for e in range(2)]
    rows = []
    for e in range(2):
        rows.extend(_split3(cb[e]))
    feats = jnp.concatenate([_rows8(rows, seq), jnp.zeros((BOUND0 - 8, seq), F32),
                             _rows8([1.0] * 3, seq), jnp.zeros((LANES - BOUND0 - 8, seq), F32)],
                            axis=0)
    kaug_scr[:, LANES:KA] = feats.T.astype(BF16)
    vt = v_ref[0].astype(F32).T
    for e in range(2):
        _store_values_t(vt_scr.at[e], vt[e * HEAD_DIM:(e + 1) * HEAD_DIM], seq)

    qk_bound = _logit_bound(qg_ref, [kg_ref])
    qt = _headnorm_pair(q_ref[0].astype(F32), bd, qg_ref[...]).T
    for e in range(2):
        qt_scr[e] = _query_feats(qt, e, _rows8([0.0] * (3 * e) + [1.0] * 3, seq), None,
                                 qk_bound + cb[e] if bounded else None)

    vt_rows = [lambda ks, e=e: vt_scr[e, :, ks] for e in range(2)]
    acc = _flash(_causal_blocks(nt, kaug_scr, vt_rows, qt_scr), 2 * nt, HEAD_DIM, bounded)
    for qh in range(nt):
        ot = jnp.concatenate([acc[e * nt + qh][0:HEAD_DIM] / acc[e * nt + qh][HEAD_DIM:HEAD_DIM + 1]
                              for e in range(2)], axis=0)
        o_ref[0, qh * BLK:(qh + 1) * BLK, :] = ot.T.astype(o_ref.dtype)


def _const_spec(a):
    return pl.BlockSpec(a.shape, lambda bi, p: (0,) * a.ndim)


def _col_spec(s, c):
    return pl.BlockSpec((1, s, LANES), lambda bi, p: (bi, 0, c + p))


def _fox_part(proj3, negf, qg, kg, bd):
    s = proj3.shape[1]
    specs = [_col_spec(s, COL_FOX_Q), _col_spec(s, COL_FOX_K), _col_spec(s, COL_FOX_V),
             pl.BlockSpec((1, FOX_HEADS, s), lambda bi, p: (bi, 0, 0)),
             _const_spec(qg), _const_spec(kg), _const_spec(bd)]
    scratch = [pltpu.VMEM((s, KA), BF16), pltpu.VMEM((2, HEAD_DIM + VAUG, s), BF16),
               pltpu.VMEM((2, KA, s), BF16)]
    return _fox_kernel, [proj3, proj3, proj3, negf, qg, kg, bd], specs, scratch


def _alibi_at_query(sl_rows, seq):
    pos = lax.broadcasted_iota(jnp.int32, (1, seq), 1).astype(F32)
    return (sl_rows[0:1] + sl_rows[2:3] + sl_rows[4:5]) * pos


def _diff_kernel(q_ref, k_ref, v_ref, kx_ref, sl_ref, lam_ref, qg_ref, kg_ref, sg_ref, bd_ref,
                 o_ref, kaug_scr, vt_scr, qt_scr, *, lam_init, bounded):
    bd = bd_ref[...]
    seq = k_ref.shape[1]
    nt = seq // BLK

    kaug_scr[:, 0:LANES] = _headnorm_pair(k_ref[0].astype(F32), bd, kg_ref[...]).astype(BF16)
    kaug_scr[:, LANES:KA] = kx_ref[...]
    _store_values_t(vt_scr, v_ref[0].astype(F32).T, seq)
    sl = sl_ref[0]
    bound = _logit_bound(qg_ref, [kg_ref]) + _alibi_at_query(sl, seq) if bounded else None
    qt = _headnorm_pair(q_ref[0].astype(F32), bd, qg_ref[...]).T
    for e in range(2):
        qt_scr[e] = _query_feats(qt, e, sl, None, bound)

    vt_rows = [lambda ks: vt_scr[:, ks]] * 2
    acc = _flash(_causal_blocks(nt, kaug_scr, vt_rows, qt_scr), 2 * nt, LANES, bounded)

    lv = lam_ref[...]
    lam = (jnp.exp(jnp.sum(lv[0:1] * lv[1:2], axis=-1, keepdims=True))
           - jnp.exp(jnp.sum(lv[2:3] * lv[3:4], axis=-1, keepdims=True)) + lam_init)
    for qh in range(nt):
        a0, a1 = acc[qh], acc[nt + qh]
        ob = (a0[0:LANES] / a0[LANES:LANES + 1] - lam * (a1[0:LANES] / a1[LANES:LANES + 1])).T
        ob = ob * lax.rsqrt(jnp.mean(ob * ob, axis=-1, keepdims=True) + EPS) * sg_ref[...]
        o_ref[0, qh * BLK:(qh + 1) * BLK, :] = (ob * (1.0 - lam_init)).astype(o_ref.dtype)


def _diff_part(proj3, kx, slrows, lam_p, qg, kg, sg, bd, lam_init):
    s = proj3.shape[1]
    specs = [_col_spec(s, COL_DIFF_Q), _col_spec(s, COL_DIFF_K), _col_spec(s, COL_DIFF_V),
             _const_spec(kx), pl.BlockSpec((1, 8, s), lambda bi, p: (p, 0, 0)),
             _const_spec(lam_p), _const_spec(qg), _const_spec(kg), _const_spec(sg), _const_spec(bd)]
    scratch = [pltpu.VMEM((s, KA), BF16), pltpu.VMEM((LANES + VAUG, s), BF16),
               pltpu.VMEM((2, KA, s), BF16)]
    return (functools.partial(_diff_kernel, lam_init=lam_init),
            [proj3, proj3, proj3, kx, slrows, lam_p, qg, kg, sg, bd], specs, scratch)


def _nsa_select(imp_t, t0):
    ns, tq = imp_t.shape
    blk = lax.broadcasted_iota(jnp.int32, (ns, tq), 0)
    cur = (t0 + lax.broadcasted_iota(jnp.int32, (ns, tq), 1)) // SLC_BLOCK
    forced = (blk == 0) | (blk == cur) | (blk == cur - 1)
    score = jnp.where(blk <= cur, jnp.where(forced, imp_t + FORCE_BONUS, imp_t), NEG)
    sel = jnp.zeros((ns, tq), F32)
    blk_f = blk.astype(F32)
    for _ in range(min(SLC_TOPN, ns)):
        best = jnp.max(score, axis=0, keepdims=True)
        first = jnp.min(jnp.where(score == best, blk_f, float(ns)), axis=0, keepdims=True)
        take = blk_f == first
        sel = jnp.where(take, 1.0, sel)
        score = jnp.where(take, 2.0 * NEG, score)
    return sel


def _slope_rows(slope):
    hi, mid, lo = [float(np.float32(v)) for v in _np_split3(slope * L2E)]
    return [hi, hi, mid, mid, lo, lo]


def _np_split3(x):
    x = np.float32(x)
    hi = np.float32(x.astype(BF16))
    mid = np.float32((x - hi).astype(BF16))
    lo = np.float32((x - hi - mid).astype(BF16))
    return hi, mid, lo


CMP_ROWS = 512


def _nsa_prepare(qall_ref, kc_ref, vc_ref, ks_ref, vs_ref, kw_ref, vw_ref, kx_ref, cx_ref, nsl_ref,
                 qg_ref, ksg_ref, kwg_ref, bd_ref, ot_ref, ksaug_scr, kwaug_scr, vst_scr, vwt_scr,
                 notsel_scr, ocmp_scr, kcaug_scr, *, nc):
    bd = bd_ref[...]
    npair = NSA_HEADS // 2
    seq = ks_ref.shape[1]
    ksaug_scr[:, 0:LANES] = _headnorm_pair(ks_ref[0].astype(F32), bd, ksg_ref[...]).astype(BF16)
    kwaug_scr[:, 0:LANES] = _headnorm_pair(kw_ref[0].astype(F32), bd, kwg_ref[...]).astype(BF16)
    ksaug_scr[:, LANES:KA] = kx_ref[...]
    kwaug_scr[:, LANES:KA] = kx_ref[...]
    for v_ref, vt_scr in ((vs_ref, vst_scr), (vw_ref, vwt_scr)):
        vt = v_ref[0].astype(F32).T
        for e in range(2):
            _store_values_t(vt_scr.at[e], vt[e * HEAD_DIM:(e + 1) * HEAD_DIM], seq)
    kcaug_scr[:, 0:LANES] = kc_ref[0]
    kcaug_scr[:, LANES:KA] = cx_ref[...]
    vct = vc_ref[0].astype(F32).T.astype(BF16)

    nidx = lax.broadcasted_iota(jnp.int32, (LANES, CMP_ROWS), 0)
    cend = nidx * CMP_STRIDE + (CMP_LEN - 1)

    def chunk(c, carry):
        t0 = pl.multiple_of(c * CMP_ROWS, CMP_ROWS)
        tq = t0 + lax.broadcasted_iota(jnp.int32, (LANES, CMP_ROWS), 1)
        cmask = (tq >= cend) & (nidx < nc)
        psum = [jnp.zeros((LANES, CMP_ROWS), F32), jnp.zeros((LANES, CMP_ROWS), F32)]
        scores = []
        for r in range(npair):
            qt = _headnorm_pair(qall_ref[0, pl.ds(t0, CMP_ROWS), r * LANES:(r + 1) * LANES].astype(F32),
                                bd, qg_ref[...]).T
            for e in range(2):
                qf = _query_feats(qt, e, nsl_ref[r, e, :, 0:CMP_ROWS], None, None)
                scores.append(jnp.dot(kcaug_scr[...], qf, preferred_element_type=F32))
        probs = []
        for k, s in enumerate(scores):
            s = jnp.where(cmask, s, NEG)
            m = jnp.max(s, axis=0, keepdims=True)
            pe = jnp.where(cmask, jnp.exp2(s - m), 0.0)
            l = jnp.sum(pe, axis=0, keepdims=True)
            pn = pe / jnp.where(l > 0.0, l, 1.0)
            psum[k % 2] = psum[k % 2] + pn
            probs.append(pn.astype(BF16))
        for k, p in enumerate(probs):
            hs = slice((k % 2) * HEAD_DIM, (k % 2 + 1) * HEAD_DIM)
            ocmp_scr[k // 2, hs, pl.ds(t0, CMP_ROWS)] = jnp.dot(vct[hs], p,
                                                                preferred_element_type=F32)
        for g in range(NSA_GROUPS):
            hi, mid, lo = _split3(psum[g])
            imp_t = (jnp.dot(ot_ref[...], hi.astype(BF16), preferred_element_type=F32)
                     + jnp.dot(ot_ref[...], mid.astype(BF16), preferred_element_type=F32)
                     + jnp.dot(ot_ref[...], lo.astype(BF16), preferred_element_type=F32))
            notsel_scr[g, :, pl.ds(t0, CMP_ROWS)] = (_nsa_select(imp_t, t0) - 1.0) * -NEG
        return carry

    lax.fori_loop(0, seq // CMP_ROWS, chunk, 0)


def _nsa_kernel(qall_ref, q_ref, sm_ref, kc_ref, vc_ref, ks_ref, vs_ref, kw_ref, vw_ref, kx_ref,
                cx_ref, nsl_ref, slr_ref, qg_ref, ksg_ref, kwg_ref, bd_ref, ot_ref, egt_ref, o_ref,
                ksaug_scr, kwaug_scr, vst_scr, vwt_scr, qts_scr, qtw_scr, notsel_scr, ocmp_scr,
                kcaug_scr, *, nc, bounded):
    r = pl.program_id(1)
    bd = bd_ref[...]
    seq = ks_ref.shape[1]
    nt = seq // BLK

    @pl.when(r == 0)
    def _():
        _nsa_prepare(qall_ref, kc_ref, vc_ref, ks_ref, vs_ref, kw_ref, vw_ref, kx_ref, cx_ref,
                     nsl_ref, qg_ref, ksg_ref, kwg_ref, bd_ref, ot_ref, ksaug_scr, kwaug_scr,
                     vst_scr, vwt_scr, notsel_scr, ocmp_scr, kcaug_scr, nc=nc)

    qk_bound = _logit_bound(qg_ref, [ksg_ref, kwg_ref])
    qt = _headnorm_pair(q_ref[0].astype(F32), bd, qg_ref[...]).T
    for e in range(2):
        sl = slr_ref[0, e]
        bound = qk_bound + _alibi_at_query(sl, seq) if bounded else None
        qts_scr[e] = _query_feats(qt, e, sl, notsel_scr[e], bound)
        qtw_scr[e] = _query_feats(qt, e, sl, None, bound)

    key = lax.broadcasted_iota(jnp.int32, (BLK, BLK), 0)
    qry = lax.broadcasted_iota(jnp.int32, (BLK, BLK), 1)
    diag = key <= qry
    above = key > qry
    vs_rows = [lambda ks, e=e: vst_scr[e, :, ks] for e in range(2)]
    slc = _causal_blocks(nt, ksaug_scr, vs_rows, qts_scr)

    def win_block(j, qh, e, mask):
        ks = slice(j * BLK, (j + 1) * BLK)
        qs = slice(qh * BLK, (qh + 1) * BLK)
        return (lambda: kwaug_scr[ks, :], lambda: vwt_scr[e, :, ks], lambda: qtw_scr[e, :, qs],
                mask, 2 * nt + e * nt + qh)

    blocks = []
    pos = 0
    for j in range(nt):
        n_j = 2 * (nt - j)
        blocks.extend(slc[pos:pos + n_j])
        pos += n_j
        blocks.extend(win_block(j, j, e, diag) for e in range(2))
        if j > 0:
            blocks.extend(win_block(j - 1, j, e, above) for e in range(2))
    acc = _flash(blocks, 4 * nt, HEAD_DIM, bounded)
    normed = lambda a: a[0:HEAD_DIM] / a[HEAD_DIM:HEAD_DIM + 1]

    gate_t = jax.nn.sigmoid(sm_ref[0]).T
    g_cmp_t, g_slc_t, g_win_t = [_dot01_left(egt_ref[c], gate_t) for c in range(3)]
    for qh in range(nt):
        cols = slice(qh * BLK, (qh + 1) * BLK)
        o_s = jnp.concatenate([normed(acc[e * nt + qh]) for e in range(2)], axis=0)
        o_w = jnp.concatenate([normed(acc[2 * nt + e * nt + qh]) for e in range(2)], axis=0)
        mix = (g_cmp_t[:, cols] * ocmp_scr[r, :, cols] + g_slc_t[:, cols] * o_s
               + g_win_t[:, cols] * o_w)
        o_ref[0, cols, :] = mix.T.astype(o_ref.dtype)


def _nsa_part(proj3, small3, kc, vc, kx, cx, nsl, qg, ksg, kwg, bd, ot, egt):
    s = proj3.shape[1]
    npair = NSA_HEADS // 2
    ns = s // SLC_BLOCK
    nc = (s - CMP_LEN) // CMP_STRIDE + 1
    row = lambda a: pl.BlockSpec((1,) + a.shape[1:], lambda bi, p: (bi,) + (0,) * (a.ndim - 1))
    kvspec = lambda c: pl.BlockSpec((1, s, LANES), lambda bi, p: (bi, 0, c))
    specs = [pl.BlockSpec((1, s, npair * LANES), lambda bi, p: (bi, 0, COL_NSA_Q // npair)),
             _col_spec(s, COL_NSA_Q), row(small3), row(kc), row(vc),
             kvspec(COL_K_SLC), kvspec(COL_V_SLC), kvspec(COL_K_WIN), kvspec(COL_V_WIN),
             _const_spec(kx), _const_spec(cx), _const_spec(nsl),
             pl.BlockSpec((1, 2, 8, s), lambda bi, p: (p, 0, 0, 0)),
             _const_spec(qg), _const_spec(ksg), _const_spec(kwg), _const_spec(bd), _const_spec(ot),
             pl.BlockSpec((3, LANES, LANES), lambda bi, p: (0, p, 0))]
    scratch = [pltpu.VMEM((s, KA), BF16), pltpu.VMEM((s, KA), BF16),
               pltpu.VMEM((2, HEAD_DIM + VAUG, s), BF16), pltpu.VMEM((2, HEAD_DIM + VAUG, s), BF16),
               pltpu.VMEM((2, KA, s), BF16), pltpu.VMEM((2, KA, s), BF16),
               pltpu.VMEM((NSA_GROUPS, ns, s), F32), pltpu.VMEM((npair, LANES, s), F32),
               pltpu.VMEM((kc.shape[1], KA), BF16)]
    arrays = [proj3, proj3, small3, kc, vc, proj3, proj3, proj3, proj3, kx, cx, nsl, nsl, qg, ksg,
              kwg, bd, ot, egt]
    return functools.partial(_nsa_kernel, nc=nc), arrays, specs, scratch


def _attention_kernel(*refs, layout, bounded):
    pos = 0
    inputs = []
    for _, n_in, _ in layout:
        inputs.append(refs[pos:pos + n_in])
        pos += n_in
    outputs = refs[pos:pos + len(layout)]
    pos += len(layout)
    for (fn, _, n_scr), ins, out in zip(layout, inputs, outputs):
        fn(*ins, out, *refs[pos:pos + n_scr], bounded=bounded)
        pos += n_scr


def _attention(layout, specs, scratch, b, s, *arrays, bounded):
    out = pl.BlockSpec((1, s, LANES), lambda bi, p: (bi, 0, p))
    npair = NSA_HEADS // 2
    return pl.pallas_call(
        functools.partial(_attention_kernel, layout=layout, bounded=bounded),
        grid=(b, npair),
        in_specs=specs,
        out_specs=[out] * len(layout),
        out_shape=[jax.ShapeDtypeStruct((b, s, npair * LANES), BF16)] * len(layout),
        scratch_shapes=scratch,
        compiler_params=_cparams(("parallel", "arbitrary")),
        name="attention",
    )(*arrays)


def _merge_kernel(oa_ref, ob_ref, oc_ref, g0_ref, g1_ref, g2_ref, h_ref, wbr_ref, wo_ref, o_ref):
    merged = None
    for c, (o_r, g_r) in enumerate(((oa_ref, g0_ref), (ob_ref, g1_ref), (oc_ref, g2_ref))):
        y = jnp.dot(o_r[...], wbr_ref[c], preferred_element_type=F32)
        term = jax.nn.sigmoid(g_r[...].astype(F32)) * y
        merged = term if merged is None else merged + term
    o_ref[...] = h_ref[...] + jnp.dot(merged.astype(BF16), wo_ref[...], preferred_element_type=F32)


def _merge(oa, ob, oc, proj2, h2, wbr, wo, *, tm=1024):
    t, d = h2.shape
    bw = oa.shape[1]
    row = lambda w: pl.BlockSpec((tm, w), lambda i: (i, 0))
    gate = lambda c: pl.BlockSpec((tm, d), lambda i: (i, COL_MG * LANES // d + c))
    resident = lambda a: pl.BlockSpec(a.shape, lambda i: (0,) * a.ndim, pipeline_mode=pl.Buffered(1))
    return pl.pallas_call(
        _merge_kernel,
        grid=(t // tm,),
        in_specs=[row(bw), row(bw), row(bw), gate(0), gate(1), gate(2), row(d),
                  resident(wbr), resident(wo)],
        out_specs=row(d),
        out_shape=jax.ShapeDtypeStruct((t, d), F32),
        compiler_params=_cparams(("parallel",)),
        name="merge",
    )(oa, ob, oc, proj2, proj2, proj2, h2, wbr, wo)


HALO = 16


def _ffn_kernel(x_ref, xh_ref, g_ref, wup_ref, cw_ref, cb_ref, wd_ref, o_ref, u_scr,
                *, tm, tf, tiles_per_seq):
    i = pl.program_id(0)
    dff = wd_ref.shape[0]

    def norm(x):
        return (x * lax.rsqrt(jnp.mean(x * x, axis=-1, keepdims=True) + EPS)
                * g_ref[...]).astype(BF16)

    a_main = norm(x_ref[...])
    a_all = jnp.concatenate([norm(xh_ref[...]), a_main], axis=0)
    seq_start = (i % tiles_per_seq) == 0
    acts = []
    for lo in range(0, dff, tf):
        hi = min(lo + tf, dff)
        cols = slice(lo, hi)
        u = jnp.dot(a_all, wup_ref[:, cols], preferred_element_type=F32)
        rows = lax.broadcasted_iota(jnp.int32, u.shape, 0)
        u_scr[:, 0:hi - lo] = jnp.where((rows < HALO) & seq_start, 0.0, u)
        uc = cb_ref[:, cols]
        for t in range(CONV_W):
            uc = uc + (cw_ref[t:t + 1, cols]
                       * u_scr[pl.ds(HALO - (CONV_W - 1) + t, tm), 0:hi - lo])
        gt = jnp.dot(a_main, wup_ref[:, dff + lo:dff + hi], preferred_element_type=F32)
        acts.append((jax.nn.gelu(uc) * gt).astype(BF16))
    act = jnp.concatenate(acts, axis=1)
    o_ref[...] = x_ref[...] + jnp.dot(act, wd_ref[...], preferred_element_type=F32)


FFN_CHUNK = 6 * 256


def _ffn(h2, g, wup, cw, cb, wd, seq, *, tm=512, tf=FFN_CHUNK):
    t, d = h2.shape
    resident = lambda a: pl.BlockSpec(a.shape, lambda i: (0,) * a.ndim, pipeline_mode=pl.Buffered(1))
    return pl.pallas_call(
        functools.partial(_ffn_kernel, tm=tm, tf=tf, tiles_per_seq=seq // tm),
        grid=(t // tm,),
        in_specs=[
            pl.BlockSpec((tm, d), lambda i: (i, 0)),
            pl.BlockSpec((HALO, d), lambda i: (jnp.maximum(i * (tm // HALO) - 1, 0), 0)),
            resident(g), resident(wup), resident(cw), resident(cb), resident(wd),
        ],
        out_specs=pl.BlockSpec((tm, d), lambda i: (i, 0)),
        out_shape=jax.ShapeDtypeStruct((t, d), F32),
        scratch_shapes=[pltpu.VMEM((HALO + tm, tf), F32)],
        compiler_params=_cparams(("parallel",)),
        name="ffn",
    )(h2, h2, g, wup, cw, cb, wd)


def _nsa_head_order():
    hpg = NSA_HEADS // NSA_GROUPS
    return [h for r in range(hpg) for h in (r, hpg + r)]


def _constants(seq):
    ns = seq // SLC_BLOCK
    nc = (seq - CMP_LEN) // CMP_STRIDE + 1
    assert MASK0 + ns <= LANES and MASK0 >= 8 and N_BIAS <= 8
    bd = np.kron(np.eye(2), np.ones((HEAD_DIM, HEAD_DIM))).astype(np.float32)
    c_start = np.arange(LANES) * CMP_STRIDE
    s_start = np.arange(ns) * SLC_BLOCK
    ot = ((c_start[None, :] < s_start[:, None] + SLC_BLOCK)
          & (c_start[None, :] + CMP_LEN > s_start[:, None])
          & (np.arange(LANES)[None, :] < nc)).astype(np.float32)
    pos = np.arange(seq)
    kx = np.zeros((seq, LANES), np.float32)
    for k in range(N_BIAS):
        kx[:, k] = (pos // 8) * 8 if k % 2 == 0 else pos % 8
    kx[pos, MASK0 + pos // SLC_BLOCK] = 1.0
    kx[:, BOUND0:BOUND0 + 3] = 1.0
    cx = kx[np.minimum(np.arange(LANES) * CMP_STRIDE + (CMP_LEN - 1), seq - 1)]
    order = _nsa_head_order()
    eg = np.zeros((3, LANES, NSA_HEADS * HEAD_DIM), np.float32)
    for c in range(3):
        for slot, h in enumerate(order):
            eg[c, SMALL_GATE0 + c * NSA_HEADS + h, slot * HEAD_DIM:(slot + 1) * HEAD_DIM] = 1.0
    u = (np.arange(BLK)[:, None] <= np.arange(BLK)[None, :]).astype(np.float32)
    dsl = np.zeros((DIFF_HEADS, 8, seq), np.float32)
    for h in range(DIFF_HEADS):
        rows = _slope_rows(2.0 ** (-8.0 * (h + 1) / DIFF_HEADS))
        dsl[h, :N_BIAS, :] = np.asarray(rows, np.float32)[:, None]
    nsl = np.zeros((NSA_HEADS // 2, 2, 8, seq), np.float32)
    for slot, h in enumerate(order):
        rows = _slope_rows(2.0 ** (-8.0 * (h + 1) / NSA_HEADS))
        nsl[slot // 2, slot % 2, :N_BIAS, :] = np.asarray(rows, np.float32)[:, None]
    as_bf = lambda a: jnp.asarray(a, BF16)
    return dict(bd=as_bf(bd), ot=as_bf(ot), kx=as_bf(kx), cx=as_bf(cx),
                egt=as_bf(np.transpose(eg, (0, 2, 1))), u=as_bf(u), dsl=jnp.asarray(dsl, F32),
                nsl=jnp.asarray(nsl, F32))


def _pack_w_in(w):
    d = w.shape[0]
    hd = HEAD_DIM
    w = w.astype(BF16)
    sizes = [512, 768, 24, 6 * 512, 8, 3 * d]
    offs = np.concatenate([[0], np.cumsum(sizes)])
    nq, nkv, ngate, six, ff, mg = [w[:, offs[k]:offs[k + 1]] for k in range(6)]
    hpg = NSA_HEADS // NSA_GROUPS
    nq = jnp.swapaxes(nq.reshape(d, NSA_GROUPS, hpg, hd), 1, 2).reshape(d, NSA_HEADS * hd)
    assert [g * hpg + r for r in range(hpg) for g in range(NSA_GROUPS)] == _nsa_head_order()
    pad = jnp.zeros((d, (PROJ_UNITS - COL_V_CMP - 1) * LANES), BF16)
    main = jnp.concatenate([mg, nq, six, nkv[:, 2 * LANES:], nkv[:, :2 * LANES], pad], axis=1)
    small = jnp.concatenate([ngate, ff, jnp.zeros((d, LANES - 32), BF16)], axis=1)
    return main, small


def kernel(x, attn_norm_g, w_in, nsa_q_g, nsa_k_g, cmp_pe, cmp_w1, cmp_w2, diff_q_g, diff_k_g,
           diff_lam, diff_subln_g, fox_q_g, fox_k_g, fox_b, w_br, w_o, ffn_norm_g, w_up, conv_w,
           conv_b, w_down):
    b, s, d = x.shape
    depth = w_in.shape[0]
    hd = HEAD_DIM
    qscale = hd ** -0.5 * L2E
    cst = _constants(s)
    rows16 = s // CMP_STRIDE
    assert rows16 == LANES and s % CMP_ROWS == 0, "NSA kernel keeps all compressed blocks in one 128-lane tile"
    order = _nsa_head_order()
    tile2 = lambda g: jnp.tile(g, 2).reshape(1, LANES).astype(F32)

    h = x.reshape(b * s, d)
    for l in range(depth):
        wmain, wsmall = _pack_w_in(w_in[l])
        proj, small = _proj(h, attn_norm_g[l].reshape(1, d), wmain, wsmall)
        proj3 = proj.reshape(b, s, PROJ_UNITS * LANES)
        small3 = small.reshape(b, s, LANES)

        w1 = cmp_w1[l].astype(BF16).reshape(2, CMP_LEN, hd, 2 * hd)
        z1 = jnp.zeros_like(w1)
        w1p = jnp.concatenate([jnp.concatenate([w1, z1], axis=3),
                               jnp.concatenate([z1, w1], axis=3)], axis=2)
        w2 = cmp_w2[l].astype(BF16)
        z2 = jnp.zeros_like(w2)
        w2p = jnp.concatenate([jnp.concatenate([w2, z2], axis=2),
                               jnp.concatenate([z2, w2], axis=2)], axis=1)
        pe8 = jnp.broadcast_to(cmp_pe[l].reshape(2, 1, CMP_LEN * hd), (2, 8, CMP_LEN * hd)).astype(BF16)
        kc, vc = _compress(proj3, w1p, pe8, cmp_w1[l].astype(BF16), w2p, tile2(nsa_k_g[l, 0]),
                           cst["bd"])

        fb_row = jnp.zeros((1, LANES), F32).at[0, SMALL_FF0:SMALL_FF0 + FOX_HEADS].set(fox_b[l])
        negf = _fcum(small3, fb_row, cst["u"])

        nsa_qg, diff_qg, fox_qg = nsa_q_g[l] * qscale, diff_q_g[l] * qscale, fox_q_g[l] * qscale
        lam_init = 0.8 - 0.6 * math.exp(-0.3 * l)
        parts = [
            _nsa_part(proj3, small3, kc, vc, cst["kx"], cst["cx"], cst["nsl"], tile2(nsa_qg),
                      tile2(nsa_k_g[l, 1]), tile2(nsa_k_g[l, 2]), cst["bd"], cst["ot"], cst["egt"]),
            _diff_part(proj3, cst["kx"], cst["dsl"], diff_lam[l], tile2(diff_qg),
                       tile2(diff_k_g[l]), diff_subln_g[l].reshape(1, LANES), cst["bd"], lam_init),
            _fox_part(proj3, negf, tile2(fox_qg), tile2(fox_k_g[l]), cst["bd"]),
        ]
        layout = tuple((fn, len(arrs), len(scr)) for fn, arrs, _, scr in parts)
        specs = [sp for _, _, sps, _ in parts for sp in sps]
        scratch = [sc for _, _, _, scs in parts for sc in scs]
        arrays = [a for _, arrs, _, _ in parts for a in arrs]
        amax = lambda g: jnp.max(jnp.abs(g))
        qk_max = functools.reduce(jnp.maximum, [
            amax(nsa_qg) * jnp.maximum(amax(nsa_k_g[l, 1]), amax(nsa_k_g[l, 2])),
            amax(diff_qg) * amax(diff_k_g[l]), amax(fox_qg) * amax(fox_k_g[l])])
        call = functools.partial(_attention, layout, specs, scratch, b, s)
        o_a, o_b, o_c = lax.cond(2.0 * HEAD_DIM * BOUND_SLACK * qk_max <= BOUND_LIMIT,
                                 functools.partial(call, bounded=True),
                                 functools.partial(call, bounded=False), *arrays)

        wbr = w_br[l]
        wbr_a = jnp.concatenate([wbr[0, hh * hd:(hh + 1) * hd] for hh in order], axis=0)
        wbr_p = jnp.stack([wbr_a, wbr[1], wbr[2]]).astype(BF16)
        bw = NSA_HEADS * hd
        h = _merge(o_a.reshape(b * s, bw), o_b.reshape(b * s, bw), o_c.reshape(b * s, bw),
                   proj, h, wbr_p, w_o[l].astype(BF16))
        h = _ffn(h, ffn_norm_g[l].reshape(1, d), w_up[l].astype(BF16), conv_w[l],
                 conv_b[l].reshape(1, -1), w_down[l].astype(BF16), s)
    return h.reshape(b, s, d)
```

```python
import functools
import math

import numpy as np
import jax
import jax.numpy as jnp
from jax import lax
from jax.experimental import pallas as pl
from jax.experimental.pallas import tpu as pltpu

F32 = jnp.float32
BF16 = jnp.bfloat16

HEAD_DIM = 64
NSA_HEADS = 8
NSA_GROUPS = 2
CMP_LEN = 32
CMP_STRIDE = 16
SLC_BLOCK = 64
SLC_TOPN = 8
WINDOW = 256
FORCE_BONUS = 1.0e4
DIFF_HEADS = 4
FOX_HEADS = 8
CONV_W = 3
EPS = 1e-6
NEG = -1.0e30
L2E = 1.4426950408889634

LANES = 128
BLK = 256
KA = 2 * LANES
N_BIAS = 6
MASK0 = 8
BOUND0 = 40
VAUG = 16
BOUND_SLACK = 1.02
BOUND_LIMIT = 100.0
VMEM_LIMIT = 56 * 1024 * 1024

COL_MG = 0
COL_NSA_Q = 24
COL_DIFF_Q, COL_DIFF_K, COL_DIFF_V = 28, 32, 36
COL_FOX_Q, COL_FOX_K, COL_FOX_V = 40, 44, 48
COL_K_SLC, COL_V_SLC, COL_K_WIN, COL_V_WIN, COL_K_CMP, COL_V_CMP = 52, 53, 54, 55, 56, 57
PROJ_UNITS = 60
SMALL_GATE0 = 0
SMALL_FF0 = 24


def _cparams(sem):
    return pltpu.CompilerParams(dimension_semantics=sem, vmem_limit_bytes=VMEM_LIMIT)


def _split3(x):
    hi = x.astype(BF16).astype(F32)
    r1 = x - hi
    mid = r1.astype(BF16).astype(F32)
    lo = (r1 - mid).astype(BF16).astype(F32)
    return hi, mid, lo


def _dot01(x, m):
    hi = x.astype(BF16)
    lo = (x - hi.astype(F32)).astype(BF16)
    return (jnp.dot(hi, m, preferred_element_type=F32)
            + jnp.dot(lo, m, preferred_element_type=F32))


def _dot01_left(m, x):
    hi = x.astype(BF16)
    lo = (x - hi.astype(F32)).astype(BF16)
    return (jnp.dot(m, hi, preferred_element_type=F32)
            + jnp.dot(m, lo, preferred_element_type=F32))


def _dot01_3(x, m):
    hi, mid, lo = _split3(x)
    return (jnp.dot(hi.astype(BF16), m, preferred_element_type=F32)
            + jnp.dot(mid.astype(BF16), m, preferred_element_type=F32)
            + jnp.dot(lo.astype(BF16), m, preferred_element_type=F32))


def _qk(q, k):
    return lax.dot_general(q, k, (((1,), (1,)), ((), ())), preferred_element_type=F32)


def _headnorm_pair(x, bd, gain):
    ss = jnp.dot((x * x).astype(BF16), bd, preferred_element_type=F32)
    return x * lax.rsqrt(ss * (1.0 / HEAD_DIM) + EPS) * gain


def _headnorm_pair_t(x, gain):
    xt = x.T
    n = xt.shape[1]
    gain_t = jnp.broadcast_to(gain, (LANES, LANES)).T
    x2 = xt * xt
    inv = [lax.rsqrt(jnp.sum(x2[h * HEAD_DIM:(h + 1) * HEAD_DIM], axis=0, keepdims=True)
                     * (1.0 / HEAD_DIM) + EPS) for h in range(2)]
    low = lax.broadcasted_iota(jnp.int32, (LANES, n), 0) < HEAD_DIM
    xs = xt * jnp.where(low, inv[0], inv[1])
    return jnp.concatenate([xs[:, c * LANES:(c + 1) * LANES] * gain_t for c in range(n // LANES)],
                           axis=1)


def _rows8(vals, width):
    row = lax.broadcasted_iota(jnp.int32, (8, width), 0)
    out = jnp.zeros((8, width), F32)
    for k, v in enumerate(vals):
        out = jnp.where(row == k, v, out)
    return out


def _query_feats(qt, half, bias8, mask_rows, bound_row):
    tq = qt.shape[1]
    row = lax.broadcasted_iota(jnp.int32, (LANES, tq), 0)
    keep = (row < HEAD_DIM) if half == 0 else (row >= HEAD_DIM)
    nmask = BOUND0 - MASK0
    parts = [jnp.where(keep, qt, 0.0), bias8,
             mask_rows if mask_rows is not None else jnp.zeros((nmask, tq), F32),
             _rows8(_split3(-bound_row), tq) if bound_row is not None else jnp.zeros((8, tq), F32),
             jnp.zeros((LANES - BOUND0 - 8, tq), F32)]
    return jnp.concatenate(parts, axis=0).astype(BF16)


def _logit_bound(qg_ref, kg_refs):
    kmax = None
    for kg_ref in kg_refs:
        k = jnp.max(jnp.abs(kg_ref[...]), axis=-1, keepdims=True)
        kmax = k if kmax is None else jnp.maximum(kmax, k)
    return jnp.max(jnp.abs(qg_ref[...]), axis=-1, keepdims=True) * kmax * (HEAD_DIM * BOUND_SLACK)


def _store_values_t(vt_view, rows, seq):
    dv = rows.shape[0]
    vt_view[0:dv, :] = rows.astype(BF16)
    vt_view[dv:dv + 8, :] = _rows8([1.0], seq).astype(BF16)
    vt_view[dv + 8:dv + VAUG, :] = jnp.zeros((VAUG - 8, seq), BF16)


QK_AHEAD = 2


def _flash(blocks, n_chain, dv, bounded):
    acc = [jnp.zeros((dv + VAUG, BLK), F32) for _ in range(n_chain)]
    mx = [jnp.full((1, BLK), NEG, F32) for _ in range(n_chain)]
    scores = {}

    def issue(k):
        if k < len(blocks):
            scores[k] = jnp.dot(blocks[k][0](), blocks[k][2](), preferred_element_type=F32)

    for k in range(2 * QK_AHEAD):
        issue(k)
    for k0 in range(0, len(blocks), 2):
        issue(k0 + 2 * QK_AHEAD)
        issue(k0 + 2 * QK_AHEAD + 1)
        pending = []
        for k in range(k0, min(k0 + 2, len(blocks))):
            _, vt, _, mask, chain = blocks[k]
            s = scores.pop(k)
            if mask is not None:
                s = jnp.where(mask, s, NEG)
            if bounded:
                pending.append((vt, chain, None, jnp.exp2(s).astype(BF16)))
            else:
                m_new = jnp.maximum(mx[chain], jnp.max(s, axis=0, keepdims=True))
                alpha = jnp.exp2(mx[chain] - m_new)
                mx[chain] = m_new
                pending.append((vt, chain, alpha, jnp.exp2(s - m_new).astype(BF16)))
        assert len({c for _, c, _, _ in pending}) == len(pending), "a pair must not share a chain"
        for vt, chain, alpha, p in pending:
            prev = acc[chain] if alpha is None else alpha * acc[chain]
            acc[chain] = prev + jnp.dot(vt(), p, preferred_element_type=F32)
    return acc


def _causal_blocks(nt, kaug_scr, vt_rows, qt_scr):
    key = lax.broadcasted_iota(jnp.int32, (BLK, BLK), 0)
    qry = lax.broadcasted_iota(jnp.int32, (BLK, BLK), 1)
    diag = key <= qry
    blocks = []
    for j in range(nt):
        ks = slice(j * BLK, (j + 1) * BLK)
        for qh in range(j, nt):
            qs = slice(qh * BLK, (qh + 1) * BLK)
            for e in range(2):
                blocks.append((lambda ks=ks: kaug_scr[ks, :],
                               lambda e=e, ks=ks: vt_rows[e](ks),
                               lambda e=e, qs=qs: qt_scr[e, :, qs],
                               diag if qh == j else None, e * nt + qh))
    return blocks


def _proj_kernel(x_ref, g_ref, w_ref, ws_ref, o_ref, os_ref, a_scr):
    j = pl.program_id(1)

    @pl.when(j == 0)
    def _():
        x = x_ref[...]
        inv = lax.rsqrt(jnp.mean(x * x, axis=-1, keepdims=True) + EPS)
        a = (x * inv * g_ref[...]).astype(BF16)
        a_scr[...] = a
        os_ref[...] = jnp.dot(a, ws_ref[...], preferred_element_type=F32)

    o_ref[...] = jnp.dot(a_scr[...], w_ref[...], preferred_element_type=F32).astype(o_ref.dtype)


def _proj(x2d, g, w, ws, *, tm=1024, tn=2560):
    t, d = x2d.shape
    n = w.shape[1]
    return pl.pallas_call(
        _proj_kernel,
        grid=(t // tm, n // tn),
        in_specs=[
            pl.BlockSpec((tm, d), lambda i, j: (i, 0)),
            pl.BlockSpec((1, d), lambda i, j: (0, 0)),
            pl.BlockSpec((d, tn), lambda i, j: (0, j)),
            pl.BlockSpec((d, LANES), lambda i, j: (0, 0)),
        ],
        out_specs=[
            pl.BlockSpec((tm, tn), lambda i, j: (i, j)),
            pl.BlockSpec((tm, LANES), lambda i, j: (i, 0)),
        ],
        out_shape=[jax.ShapeDtypeStruct((t, n), BF16), jax.ShapeDtypeStruct((t, LANES), F32)],
        scratch_shapes=[pltpu.VMEM((tm, d), BF16)],
        compiler_params=_cparams(("parallel", "arbitrary")),
        name="proj",
    )(x2d, g, w, ws)


CMP_BATCH = 4


def _compress_kernel(k_ref, v_ref, w1p_ref, pe_ref, w1f_ref, w2p_ref, kg_ref, bd_ref,
                     kc_ref, vc_ref, raw_scr):
    nb, seq = k_ref.shape[0], k_ref.shape[1]
    nrow = kc_ref.shape[1]
    tail = raw_scr.shape[1] - seq
    for bi in range(nb):
        raw_scr[bi, seq:, :] = jnp.zeros((tail, LANES), F32)
    for kv, (src, dst) in enumerate(((k_ref, kc_ref), (v_ref, vc_ref))):
        for bi in range(nb):
            raw_scr[bi, 0:seq, :] = src[bi].astype(F32)
        c1 = jnp.dot(pe_ref[kv], w1f_ref[kv], preferred_element_type=F32)[0:1]
        pre = jnp.concatenate([c1, c1], axis=1)
        for l in range(CMP_LEN):
            rows = jnp.concatenate([raw_scr[bi, pl.ds(l, nrow, stride=CMP_STRIDE), :]
                                    for bi in range(nb)], axis=0).astype(BF16)
            pre = pre + jnp.dot(rows, w1p_ref[kv, l], preferred_element_type=F32)
        hid = jax.nn.gelu(pre).astype(BF16)
        o = jnp.dot(hid, w2p_ref[kv], preferred_element_type=F32)
        if kv == 0:
            o = _headnorm_pair(o, bd_ref[...], kg_ref[...])
        for bi in range(nb):
            dst[bi] = o[bi * nrow:(bi + 1) * nrow].astype(BF16)


def _compress(proj3, w1p, pe8, w1f, w2p, kg2, bd):
    b, s, _ = proj3.shape
    nrow = s // CMP_STRIDE
    nb = math.gcd(b, CMP_BATCH)
    full = lambda a: pl.BlockSpec(a.shape, lambda i: (0,) * a.ndim)
    return pl.pallas_call(
        _compress_kernel,
        grid=(b // nb,),
        in_specs=[pl.BlockSpec((nb, s, LANES), lambda i: (i, 0, COL_K_CMP)),
                  pl.BlockSpec((nb, s, LANES), lambda i: (i, 0, COL_V_CMP)),
                  full(w1p), full(pe8), full(w1f), full(w2p), full(kg2), full(bd)],
        out_specs=[pl.BlockSpec((nb, nrow, LANES), lambda i: (i, 0, 0)),
                   pl.BlockSpec((nb, nrow, LANES), lambda i: (i, 0, 0))],
        out_shape=[jax.ShapeDtypeStruct((b, nrow, LANES), BF16),
                   jax.ShapeDtypeStruct((b, nrow, LANES), BF16)],
        scratch_shapes=[pltpu.VMEM((nb, s + CMP_LEN, LANES), F32)],
        compiler_params=_cparams(("parallel",)),
        name="compress",
    )(proj3, proj3, w1p, pe8, w1f, w2p, kg2, bd)


def _fcum_kernel(s_ref, fb_ref, u_ref, o_ref, *, chunk):
    z = s_ref[0] + fb_ref[...]
    lf = jax.nn.log_sigmoid(z)
    lft = lf.T[SMALL_FF0:SMALL_FF0 + FOX_HEADS]
    seq = lft.shape[1]
    carry = jnp.zeros((FOX_HEADS, 1), F32)
    for c in range(seq // chunk):
        fc = _dot01_3(lft[:, c * chunk:(c + 1) * chunk], u_ref[...]) + carry
        o_ref[0, :, c * chunk:(c + 1) * chunk] = -fc
        carry = fc[:, chunk - 1:chunk]


def _fcum(small3, fb_row, u):
    b, s, _ = small3.shape
    chunk = u.shape[0]
    return pl.pallas_call(
        functools.partial(_fcum_kernel, chunk=chunk),
        grid=(b,),
        in_specs=[pl.BlockSpec((1, s, LANES), lambda i: (i, 0, 0)),
                  pl.BlockSpec((1, LANES), lambda i: (0, 0)),
                  pl.BlockSpec(u.shape, lambda i: (0, 0))],
        out_specs=pl.BlockSpec((1, FOX_HEADS, s), lambda i: (i, 0, 0)),
        out_shape=jax.ShapeDtypeStruct((b, FOX_HEADS, s), F32),
        compiler_params=_cparams(("parallel",)),
        name="fcum",
    )(small3, fb_row, u)


def _fox_kernel(q_ref, k_ref, v_ref, nf_ref, qg_ref, kg_ref, o_ref,
                kaug_scr, vt_scr, qt_scr, *, bounded):
    p = pl.program_id(1)
    seq = k_ref.shape[1]
    nt = seq // BLK

    kaug_scr[:, 0:LANES] = _headnorm_pair_t(k_ref[0].astype(F32), kg_ref[...]).T.astype(BF16)
    cb = [nf_ref[0, pl.ds(2 * p + e, 1), :] * L2E for e in range(2)]
    rows = []
    for e in range(2):
        rows.extend(_split3(cb[e]))
    feats = jnp.concatenate([_rows8(rows, seq), jnp.zeros((BOUND0 - 8, seq), F32),
                             _rows8([1.0] * 3, seq), jnp.zeros((LANES - BOUND0 - 8, seq), F32)],
                            axis=0)
    kaug_scr[:, LANES:KA] = feats.T.astype(BF16)
    vt = v_ref[0].astype(F32).T
    for e in range(2):
        _store_values_t(vt_scr.at[e], vt[e * HEAD_DIM:(e + 1) * HEAD_DIM], seq)

    qk_bound = _logit_bound(qg_ref, [kg_ref])
    qt = _headnorm_pair_t(q_ref[0].astype(F32), qg_ref[...])
    for e in range(2):
        qt_scr[e] = _query_feats(qt, e, _rows8([0.0] * (3 * e) + [1.0] * 3, seq), None,
                                 qk_bound + cb[e] if bounded else None)

    vt_rows = [lambda ks, e=e: vt_scr[e, :, ks] for e in range(2)]
    acc = _flash(_causal_blocks(nt, kaug_scr, vt_rows, qt_scr), 2 * nt, HEAD_DIM, bounded)
    for qh in range(nt):
        ot = jnp.concatenate([acc[e * nt + qh][0:HEAD_DIM] / acc[e * nt + qh][HEAD_DIM:HEAD_DIM + 1]
                              for e in range(2)], axis=0)
        o_ref[0, qh * BLK:(qh + 1) * BLK, :] = ot.T.astype(o_ref.dtype)


def _const_spec(a):
    return pl.BlockSpec(a.shape, lambda bi, p: (0,) * a.ndim)


def _col_spec(s, c):
    return pl.BlockSpec((1, s, LANES), lambda bi, p: (bi, 0, c + p))


def _fox_part(proj3, negf, qg, kg):
    s = proj3.shape[1]
    specs = [_col_spec(s, COL_FOX_Q), _col_spec(s, COL_FOX_K), _col_spec(s, COL_FOX_V),
             pl.BlockSpec((1, FOX_HEADS, s), lambda bi, p: (bi, 0, 0)),
             _const_spec(qg), _const_spec(kg)]
    scratch = [pltpu.VMEM((s, KA), BF16), pltpu.VMEM((2, HEAD_DIM + VAUG, s), BF16),
               pltpu.VMEM((2, KA, s), BF16)]
    return _fox_kernel, [proj3, proj3, proj3, negf, qg, kg], specs, scratch


def _alibi_at_query(sl_rows, seq):
    pos = lax.broadcasted_iota(jnp.int32, (1, seq), 1).astype(F32)
    return (sl_rows[0:1] + sl_rows[2:3] + sl_rows[4:5]) * pos


def _diff_kernel(q_ref, k_ref, v_ref, kx_ref, sl_ref, lam_ref, qg_ref, kg_ref, sg_ref,
                 o_ref, kaug_scr, vt_scr, qt_scr, *, lam_init, bounded):
    seq = k_ref.shape[1]
    nt = seq // BLK

    kaug_scr[:, 0:LANES] = _headnorm_pair_t(k_ref[0].astype(F32), kg_ref[...]).T.astype(BF16)
    kaug_scr[:, LANES:KA] = kx_ref[...]
    _store_values_t(vt_scr, v_ref[0].astype(F32).T, seq)
    sl = sl_ref[0]
    bound = _logit_bound(qg_ref, [kg_ref]) + _alibi_at_query(sl, seq) if bounded else None
    qt = _headnorm_pair_t(q_ref[0].astype(F32), qg_ref[...])
    for e in range(2):
        qt_scr[e] = _query_feats(qt, e, sl, None, bound)

    vt_rows = [lambda ks: vt_scr[:, ks]] * 2
    acc = _flash(_causal_blocks(nt, kaug_scr, vt_rows, qt_scr), 2 * nt, LANES, bounded)

    lv = lam_ref[...]
    lam = (jnp.exp(jnp.sum(lv[0:1] * lv[1:2], axis=-1, keepdims=True))
           - jnp.exp(jnp.sum(lv[2:3] * lv[3:4], axis=-1, keepdims=True)) + lam_init)
    for qh in range(nt):
        a0, a1 = acc[qh], acc[nt + qh]
        ob = (a0[0:LANES] / a0[LANES:LANES + 1] - lam * (a1[0:LANES] / a1[LANES:LANES + 1])).T
        ob = ob * lax.rsqrt(jnp.mean(ob * ob, axis=-1, keepdims=True) + EPS) * sg_ref[...]
        o_ref[0, qh * BLK:(qh + 1) * BLK, :] = (ob * (1.0 - lam_init)).astype(o_ref.dtype)


def _diff_part(proj3, kx, slrows, lam_p, qg, kg, sg, lam_init):
    s = proj3.shape[1]
    specs = [_col_spec(s, COL_DIFF_Q), _col_spec(s, COL_DIFF_K), _col_spec(s, COL_DIFF_V),
             _const_spec(kx), pl.BlockSpec((1, 8, s), lambda bi, p: (p, 0, 0)),
             _const_spec(lam_p), _const_spec(qg), _const_spec(kg), _const_spec(sg)]
    scratch = [pltpu.VMEM((s, KA), BF16), pltpu.VMEM((LANES + VAUG, s), BF16),
               pltpu.VMEM((2, KA, s), BF16)]
    return (functools.partial(_diff_kernel, lam_init=lam_init),
            [proj3, proj3, proj3, kx, slrows, lam_p, qg, kg, sg], specs, scratch)


def _nsa_select(imp_t, t0):
    ns, tq = imp_t.shape
    blk = lax.broadcasted_iota(jnp.int32, (ns, tq), 0)
    cur = (t0 + lax.broadcasted_iota(jnp.int32, (ns, tq), 1)) // SLC_BLOCK
    forced = (blk == 0) | (blk == cur) | (blk == cur - 1)
    score = jnp.where(blk <= cur, jnp.where(forced, imp_t + FORCE_BONUS, imp_t), NEG)
    sel = jnp.zeros((ns, tq), F32)
    blk_f = blk.astype(F32)
    for _ in range(min(SLC_TOPN, ns)):
        best = jnp.max(score, axis=0, keepdims=True)
        first = jnp.min(jnp.where(score == best, blk_f, float(ns)), axis=0, keepdims=True)
        take = blk_f == first
        sel = jnp.where(take, 1.0, sel)
        score = jnp.where(take, 2.0 * NEG, score)
    return sel


def _slope_rows(slope):
    hi, mid, lo = [float(np.float32(v)) for v in _np_split3(slope * L2E)]
    return [hi, hi, mid, mid, lo, lo]


def _np_split3(x):
    x = np.float32(x)
    hi = np.float32(x.astype(BF16))
    mid = np.float32((x - hi).astype(BF16))
    lo = np.float32((x - hi - mid).astype(BF16))
    return hi, mid, lo


CMP_ROWS = 512


def _nsa_prepare(qall_ref, kc_ref, vc_ref, ks_ref, vs_ref, kw_ref, vw_ref, kx_ref, cx_ref, nsl_ref,
                 qg_ref, ksg_ref, kwg_ref, bd_ref, ot_ref, ksaug_scr, kwaug_scr, vst_scr, vwt_scr,
                 notsel_scr, ocmp_scr, kcaug_scr, *, nc):
    bd = bd_ref[...]
    npair = NSA_HEADS // 2
    seq = ks_ref.shape[1]
    ksaug_scr[:, 0:LANES] = _headnorm_pair(ks_ref[0].astype(F32), bd, ksg_ref[...]).astype(BF16)
    kwaug_scr[:, 0:LANES] = _headnorm_pair(kw_ref[0].astype(F32), bd, kwg_ref[...]).astype(BF16)
    ksaug_scr[:, LANES:KA] = kx_ref[...]
    kwaug_scr[:, LANES:KA] = kx_ref[...]
    for v_ref, vt_scr in ((vs_ref, vst_scr), (vw_ref, vwt_scr)):
        vt = v_ref[0].astype(F32).T
        for e in range(2):
            _store_values_t(vt_scr.at[e], vt[e * HEAD_DIM:(e + 1) * HEAD_DIM], seq)
    kcaug_scr[:, 0:LANES] = kc_ref[0]
    kcaug_scr[:, LANES:KA] = cx_ref[...]
    vct = vc_ref[0].astype(F32).T.astype(BF16)

    nidx = lax.broadcasted_iota(jnp.int32, (LANES, CMP_ROWS), 0)
    cend = nidx * CMP_STRIDE + (CMP_LEN - 1)

    def chunk(c, carry):
        t0 = pl.multiple_of(c * CMP_ROWS, CMP_ROWS)
        tq = t0 + lax.broadcasted_iota(jnp.int32, (LANES, CMP_ROWS), 1)
        cmask = (tq >= cend) & (nidx < nc)
        psum = [jnp.zeros((LANES, CMP_ROWS), F32), jnp.zeros((LANES, CMP_ROWS), F32)]
        scores = []
        for r in range(npair):
            qt = _headnorm_pair_t(
                qall_ref[0, pl.ds(t0, CMP_ROWS), r * LANES:(r + 1) * LANES].astype(F32),
                qg_ref[...])
            for e in range(2):
                qf = _query_feats(qt, e, nsl_ref[r, e, :, 0:CMP_ROWS], None, None)
                scores.append(jnp.dot(kcaug_scr[...], qf, preferred_element_type=F32))
        probs = []
        for k, s in enumerate(scores):
            s = jnp.where(cmask, s, NEG)
            m = jnp.max(s, axis=0, keepdims=True)
            pe = jnp.where(cmask, jnp.exp2(s - m), 0.0)
            l = jnp.sum(pe, axis=0, keepdims=True)
            pn = pe / jnp.where(l > 0.0, l, 1.0)
            psum[k % 2] = psum[k % 2] + pn
            probs.append(pn.astype(BF16))
        for k, p in enumerate(probs):
            hs = slice((k % 2) * HEAD_DIM, (k % 2 + 1) * HEAD_DIM)
            ocmp_scr[k // 2, hs, pl.ds(t0, CMP_ROWS)] = jnp.dot(vct[hs], p,
                                                                preferred_element_type=F32)
        for g in range(NSA_GROUPS):
            hi, mid, lo = _split3(psum[g])
            imp_t = (jnp.dot(ot_ref[...], hi.astype(BF16), preferred_element_type=F32)
                     + jnp.dot(ot_ref[...], mid.astype(BF16), preferred_element_type=F32)
                     + jnp.dot(ot_ref[...], lo.astype(BF16), preferred_element_type=F32))
            notsel_scr[g, :, pl.ds(t0, CMP_ROWS)] = (_nsa_select(imp_t, t0) - 1.0) * -NEG
        return carry

    lax.fori_loop(0, seq // CMP_ROWS, chunk, 0)


def _nsa_kernel(qall_ref, q_ref, sm_ref, kc_ref, vc_ref, ks_ref, vs_ref, kw_ref, vw_ref, kx_ref,
                cx_ref, nsl_ref, slr_ref, qg_ref, ksg_ref, kwg_ref, bd_ref, ot_ref, egt_ref, o_ref,
                ksaug_scr, kwaug_scr, vst_scr, vwt_scr, qts_scr, qtw_scr, notsel_scr, ocmp_scr,
                kcaug_scr, *, nc, bounded):
    r = pl.program_id(1)
    seq = ks_ref.shape[1]
    nt = seq // BLK

    @pl.when(r == 0)
    def _():
        _nsa_prepare(qall_ref, kc_ref, vc_ref, ks_ref, vs_ref, kw_ref, vw_ref, kx_ref, cx_ref,
                     nsl_ref, qg_ref, ksg_ref, kwg_ref, bd_ref, ot_ref, ksaug_scr, kwaug_scr,
                     vst_scr, vwt_scr, notsel_scr, ocmp_scr, kcaug_scr, nc=nc)

    qk_bound = _logit_bound(qg_ref, [ksg_ref, kwg_ref])
    qt = _headnorm_pair_t(q_ref[0].astype(F32), qg_ref[...])
    for e in range(2):
        sl = slr_ref[0, e]
        bound = qk_bound + _alibi_at_query(sl, seq) if bounded else None
        qts_scr[e] = _query_feats(qt, e, sl, notsel_scr[e], bound)
        qtw_scr[e] = _query_feats(qt, e, sl, None, bound)

    key = lax.broadcasted_iota(jnp.int32, (BLK, BLK), 0)
    qry = lax.broadcasted_iota(jnp.int32, (BLK, BLK), 1)
    diag = key <= qry
    above = key > qry
    vs_rows = [lambda ks, e=e: vst_scr[e, :, ks] for e in range(2)]
    slc = _causal_blocks(nt, ksaug_scr, vs_rows, qts_scr)

    def win_block(j, qh, e, mask):
        ks = slice(j * BLK, (j + 1) * BLK)
        qs = slice(qh * BLK, (qh + 1) * BLK)
        return (lambda: kwaug_scr[ks, :], lambda: vwt_scr[e, :, ks], lambda: qtw_scr[e, :, qs],
                mask, 2 * nt + e * nt + qh)

    blocks = []
    pos = 0
    for j in range(nt):
        n_j = 2 * (nt - j)
        blocks.extend(slc[pos:pos + n_j])
        pos += n_j
        blocks.extend(win_block(j, j, e, diag) for e in range(2))
        if j > 0:
            blocks.extend(win_block(j - 1, j, e, above) for e in range(2))
    acc = _flash(blocks, 4 * nt, HEAD_DIM, bounded)
    normed = lambda a: a[0:HEAD_DIM] / a[HEAD_DIM:HEAD_DIM + 1]

    gate_t = jax.nn.sigmoid(sm_ref[0]).T
    g_cmp_t, g_slc_t, g_win_t = [_dot01_left(egt_ref[c], gate_t) for c in range(3)]
    for qh in range(nt):
        cols = slice(qh * BLK, (qh + 1) * BLK)
        o_s = jnp.concatenate([normed(acc[e * nt + qh]) for e in range(2)], axis=0)
        o_w = jnp.concatenate([normed(acc[2 * nt + e * nt + qh]) for e in range(2)], axis=0)
        mix = (g_cmp_t[:, cols] * ocmp_scr[r, :, cols] + g_slc_t[:, cols] * o_s
               + g_win_t[:, cols] * o_w)
        o_ref[0, cols, :] = mix.T.astype(o_ref.dtype)


def _nsa_part(proj3, small3, kc, vc, kx, cx, nsl, qg, ksg, kwg, bd, ot, egt):
    s = proj3.shape[1]
    npair = NSA_HEADS // 2
    ns = s // SLC_BLOCK
    nc = (s - CMP_LEN) // CMP_STRIDE + 1
    row = lambda a: pl.BlockSpec((1,) + a.shape[1:], lambda bi, p: (bi,) + (0,) * (a.ndim - 1))
    kvspec = lambda c: pl.BlockSpec((1, s, LANES), lambda bi, p: (bi, 0, c))
    specs = [pl.BlockSpec((1, s, npair * LANES), lambda bi, p: (bi, 0, COL_NSA_Q // npair)),
             _col_spec(s, COL_NSA_Q), row(small3), row(kc), row(vc),
             kvspec(COL_K_SLC), kvspec(COL_V_SLC), kvspec(COL_K_WIN), kvspec(COL_V_WIN),
             _const_spec(kx), _const_spec(cx), _const_spec(nsl),
             pl.BlockSpec((1, 2, 8, s), lambda bi, p: (p, 0, 0, 0)),
             _const_spec(qg), _const_spec(ksg), _const_spec(kwg), _const_spec(bd), _const_spec(ot),
             pl.BlockSpec((3, LANES, LANES), lambda bi, p: (0, p, 0))]
    scratch = [pltpu.VMEM((s, KA), BF16), pltpu.VMEM((s, KA), BF16),
               pltpu.VMEM((2, HEAD_DIM + VAUG, s), BF16), pltpu.VMEM((2, HEAD_DIM + VAUG, s), BF16),
               pltpu.VMEM((2, KA, s), BF16), pltpu.VMEM((2, KA, s), BF16),
               pltpu.VMEM((NSA_GROUPS, ns, s), F32), pltpu.VMEM((npair, LANES, s), F32),
               pltpu.VMEM((kc.shape[1], KA), BF16)]
    arrays = [proj3, proj3, small3, kc, vc, proj3, proj3, proj3, proj3, kx, cx, nsl, nsl, qg, ksg,
              kwg, bd, ot, egt]
    return functools.partial(_nsa_kernel, nc=nc), arrays, specs, scratch


def _attention_kernel(*refs, layout, bounded):
    pos = 0
    inputs = []
    for _, n_in, _ in layout:
        inputs.append(refs[pos:pos + n_in])
        pos += n_in
    outputs = refs[pos:pos + len(layout)]
    pos += len(layout)
    for (fn, _, n_scr), ins, out in zip(layout, inputs, outputs):
        fn(*ins, out, *refs[pos:pos + n_scr], bounded=bounded)
        pos += n_scr


def _attention(layout, specs, scratch, b, s, *arrays, bounded):
    out = pl.BlockSpec((1, s, LANES), lambda bi, p: (bi, 0, p))
    npair = NSA_HEADS // 2
    return pl.pallas_call(
        functools.partial(_attention_kernel, layout=layout, bounded=bounded),
        grid=(b, npair),
        in_specs=specs,
        out_specs=[out] * len(layout),
        out_shape=[jax.ShapeDtypeStruct((b, s, npair * LANES), BF16)] * len(layout),
        scratch_shapes=scratch,
        compiler_params=_cparams(("parallel", "arbitrary")),
        name="attention",
    )(*arrays)


def _merge_kernel(oa_ref, ob_ref, oc_ref, g0_ref, g1_ref, g2_ref, h_ref, wbr_ref, wo_ref, o_ref):
    merged = None
    for c, (o_r, g_r) in enumerate(((oa_ref, g0_ref), (ob_ref, g1_ref), (oc_ref, g2_ref))):
        y = jnp.dot(o_r[...], wbr_ref[c], preferred_element_type=F32)
        term = jax.nn.sigmoid(g_r[...].astype(F32)) * y
        merged = term if merged is None else merged + term
    o_ref[...] = h_ref[...] + jnp.dot(merged.astype(BF16), wo_ref[...], preferred_element_type=F32)


def _merge(oa, ob, oc, proj2, h2, wbr, wo, *, tm=1024):
    t, d = h2.shape
    bw = oa.shape[1]
    row = lambda w: pl.BlockSpec((tm, w), lambda i: (i, 0))
    gate = lambda c: pl.BlockSpec((tm, d), lambda i: (i, COL_MG * LANES // d + c))
    resident = lambda a: pl.BlockSpec(a.shape, lambda i: (0,) * a.ndim, pipeline_mode=pl.Buffered(1))
    return pl.pallas_call(
        _merge_kernel,
        grid=(t // tm,),
        in_specs=[row(bw), row(bw), row(bw), gate(0), gate(1), gate(2), row(d),
                  resident(wbr), resident(wo)],
        out_specs=row(d),
        out_shape=jax.ShapeDtypeStruct((t, d), F32),
        compiler_params=_cparams(("parallel",)),
        name="merge",
    )(oa, ob, oc, proj2, proj2, proj2, h2, wbr, wo)


HALO = 16


def _ffn_kernel(x_ref, xh_ref, g_ref, wup_ref, cw_ref, cb_ref, wd_ref, o_ref, u_scr,
                *, tm, tf, tiles_per_seq):
    i = pl.program_id(0)
    dff = wd_ref.shape[0]

    def norm(x):
        return (x * lax.rsqrt(jnp.mean(x * x, axis=-1, keepdims=True) + EPS)
                * g_ref[...]).astype(BF16)

    a_main = norm(x_ref[...])
    a_all = jnp.concatenate([norm(xh_ref[...]), a_main], axis=0)
    seq_start = (i % tiles_per_seq) == 0
    acts = []
    for lo in range(0, dff, tf):
        hi = min(lo + tf, dff)
        cols = slice(lo, hi)
        u = jnp.dot(a_all, wup_ref[:, cols], preferred_element_type=F32)
        rows = lax.broadcasted_iota(jnp.int32, u.shape, 0)
        u_scr[:, 0:hi - lo] = jnp.where((rows < HALO) & seq_start, 0.0, u)
        uc = cb_ref[:, cols]
        for t in range(CONV_W):
            uc = uc + (cw_ref[t:t + 1, cols]
                       * u_scr[pl.ds(HALO - (CONV_W - 1) + t, tm), 0:hi - lo])
        gt = jnp.dot(a_main, wup_ref[:, dff + lo:dff + hi], preferred_element_type=F32)
        acts.append((jax.nn.gelu(uc) * gt).astype(BF16))
    act = jnp.concatenate(acts, axis=1)
    o_ref[...] = x_ref[...] + jnp.dot(act, wd_ref[...], preferred_element_type=F32)


FFN_CHUNK = 6 * 256


def _ffn(h2, g, wup, cw, cb, wd, seq, *, tm=512, tf=FFN_CHUNK):
    t, d = h2.shape
    resident = lambda a: pl.BlockSpec(a.shape, lambda i: (0,) * a.ndim, pipeline_mode=pl.Buffered(1))
    return pl.pallas_call(
        functools.partial(_ffn_kernel, tm=tm, tf=tf, tiles_per_seq=seq // tm),
        grid=(t // tm,),
        in_specs=[
            pl.BlockSpec((tm, d), lambda i: (i, 0)),
            pl.BlockSpec((HALO, d), lambda i: (jnp.maximum(i * (tm // HALO) - 1, 0), 0)),
            resident(g), resident(wup), resident(cw), resident(cb), resident(wd),
        ],
        out_specs=pl.BlockSpec((tm, d), lambda i: (i, 0)),
        out_shape=jax.ShapeDtypeStruct((t, d), F32),
        scratch_shapes=[pltpu.VMEM((HALO + tm, tf), F32)],
        compiler_params=_cparams(("parallel",)),
        name="ffn",
    )(h2, h2, g, wup, cw, cb, wd)


def _nsa_head_order():
    hpg = NSA_HEADS // NSA_GROUPS
    return [h for r in range(hpg) for h in (r, hpg + r)]


def _constants(seq):
    ns = seq // SLC_BLOCK
    nc = (seq - CMP_LEN) // CMP_STRIDE + 1
    assert MASK0 + ns <= LANES and MASK0 >= 8 and N_BIAS <= 8
    bd = np.kron(np.eye(2), np.ones((HEAD_DIM, HEAD_DIM))).astype(np.float32)
    c_start = np.arange(LANES) * CMP_STRIDE
    s_start = np.arange(ns) * SLC_BLOCK
    ot = ((c_start[None, :] < s_start[:, None] + SLC_BLOCK)
          & (c_start[None, :] + CMP_LEN > s_start[:, None])
          & (np.arange(LANES)[None, :] < nc)).astype(np.float32)
    pos = np.arange(seq)
    kx = np.zeros((seq, LANES), np.float32)
    for k in range(N_BIAS):
        kx[:, k] = (pos // 8) * 8 if k % 2 == 0 else pos % 8
    kx[pos, MASK0 + pos // SLC_BLOCK] = 1.0
    kx[:, BOUND0:BOUND0 + 3] = 1.0
    cx = kx[np.minimum(np.arange(LANES) * CMP_STRIDE + (CMP_LEN - 1), seq - 1)]
    order = _nsa_head_order()
    eg = np.zeros((3, LANES, NSA_HEADS * HEAD_DIM), np.float32)
    for c in range(3):
        for slot, h in enumerate(order):
            eg[c, SMALL_GATE0 + c * NSA_HEADS + h, slot * HEAD_DIM:(slot + 1) * HEAD_DIM] = 1.0
    u = (np.arange(BLK)[:, None] <= np.arange(BLK)[None, :]).astype(np.float32)
    dsl = np.zeros((DIFF_HEADS, 8, seq), np.float32)
    for h in range(DIFF_HEADS):
        rows = _slope_rows(2.0 ** (-8.0 * (h + 1) / DIFF_HEADS))
        dsl[h, :N_BIAS, :] = np.asarray(rows, np.float32)[:, None]
    nsl = np.zeros((NSA_HEADS // 2, 2, 8, seq), np.float32)
    for slot, h in enumerate(order):
        rows = _slope_rows(2.0 ** (-8.0 * (h + 1) / NSA_HEADS))
        nsl[slot // 2, slot % 2, :N_BIAS, :] = np.asarray(rows, np.float32)[:, None]
    as_bf = lambda a: jnp.asarray(a, BF16)
    return dict(bd=as_bf(bd), ot=as_bf(ot), kx=as_bf(kx), cx=as_bf(cx),
                egt=as_bf(np.transpose(eg, (0, 2, 1))), u=as_bf(u), dsl=jnp.asarray(dsl, F32),
                nsl=jnp.asarray(nsl, F32))


def _pack_w_in(w):
    d = w.shape[0]
    hd = HEAD_DIM
    w = w.astype(BF16)
    sizes = [512, 768, 24, 6 * 512, 8, 3 * d]
    offs = np.concatenate([[0], np.cumsum(sizes)])
    nq, nkv, ngate, six, ff, mg = [w[:, offs[k]:offs[k + 1]] for k in range(6)]
    hpg = NSA_HEADS // NSA_GROUPS
    nq = jnp.swapaxes(nq.reshape(d, NSA_GROUPS, hpg, hd), 1, 2).reshape(d, NSA_HEADS * hd)
    assert [g * hpg + r for r in range(hpg) for g in range(NSA_GROUPS)] == _nsa_head_order()
    pad = jnp.zeros((d, (PROJ_UNITS - COL_V_CMP - 1) * LANES), BF16)
    main = jnp.concatenate([mg, nq, six, nkv[:, 2 * LANES:], nkv[:, :2 * LANES], pad], axis=1)
    small = jnp.concatenate([ngate, ff, jnp.zeros((d, LANES - 32), BF16)], axis=1)
    return main, small


def kernel(x, attn_norm_g, w_in, nsa_q_g, nsa_k_g, cmp_pe, cmp_w1, cmp_w2, diff_q_g, diff_k_g,
           diff_lam, diff_subln_g, fox_q_g, fox_k_g, fox_b, w_br, w_o, ffn_norm_g, w_up, conv_w,
           conv_b, w_down):
    b, s, d = x.shape
    depth = w_in.shape[0]
    hd = HEAD_DIM
    qscale = hd ** -0.5 * L2E
    cst = _constants(s)
    rows16 = s // CMP_STRIDE
    assert rows16 == LANES and s % CMP_ROWS == 0, "NSA kernel keeps all compressed blocks in one 128-lane tile"
    order = _nsa_head_order()
    tile2 = lambda g: jnp.tile(g, 2).reshape(1, LANES).astype(F32)

    h = x.reshape(b * s, d)
    for l in range(depth):
        wmain, wsmall = _pack_w_in(w_in[l])
        proj, small = _proj(h, attn_norm_g[l].reshape(1, d), wmain, wsmall)
        proj3 = proj.reshape(b, s, PROJ_UNITS * LANES)
        small3 = small.reshape(b, s, LANES)

        w1 = cmp_w1[l].astype(BF16).reshape(2, CMP_LEN, hd, 2 * hd)
        z1 = jnp.zeros_like(w1)
        w1p = jnp.concatenate([jnp.concatenate([w1, z1], axis=3),
                               jnp.concatenate([z1, w1], axis=3)], axis=2)
        w2 = cmp_w2[l].astype(BF16)
        z2 = jnp.zeros_like(w2)
        w2p = jnp.concatenate([jnp.concatenate([w2, z2], axis=2),
                               jnp.concatenate([z2, w2], axis=2)], axis=1)
        pe8 = jnp.broadcast_to(cmp_pe[l].reshape(2, 1, CMP_LEN * hd), (2, 8, CMP_LEN * hd)).astype(BF16)
        kc, vc = _compress(proj3, w1p, pe8, cmp_w1[l].astype(BF16), w2p, tile2(nsa_k_g[l, 0]),
                           cst["bd"])

        fb_row = jnp.zeros((1, LANES), F32).at[0, SMALL_FF0:SMALL_FF0 + FOX_HEADS].set(fox_b[l])
        negf = _fcum(small3, fb_row, cst["u"])

        nsa_qg, diff_qg, fox_qg = nsa_q_g[l] * qscale, diff_q_g[l] * qscale, fox_q_g[l] * qscale
        lam_init = 0.8 - 0.6 * math.exp(-0.3 * l)
        parts = [
            _nsa_part(proj3, small3, kc, vc, cst["kx"], cst["cx"], cst["nsl"], tile2(nsa_qg),
                      tile2(nsa_k_g[l, 1]), tile2(nsa_k_g[l, 2]), cst["bd"], cst["ot"], cst["egt"]),
            _diff_part(proj3, cst["kx"], cst["dsl"], diff_lam[l], tile2(diff_qg),
                       tile2(diff_k_g[l]), diff_subln_g[l].reshape(1, LANES), lam_init),
            _fox_part(proj3, negf, tile2(fox_qg), tile2(fox_k_g[l])),
        ]
        layout = tuple((fn, len(arrs), len(scr)) for fn, arrs, _, scr in parts)
        specs = [sp for _, _, sps, _ in parts for sp in sps]
        scratch = [sc for _, _, _, scs in parts for sc in scs]
        arrays = [a for _, arrs, _, _ in parts for a in arrs]
        amax = lambda g: jnp.max(jnp.abs(g))
        qk_max = functools.reduce(jnp.maximum, [
            amax(nsa_qg) * jnp.maximum(amax(nsa_k_g[l, 1]), amax(nsa_k_g[l, 2])),
            amax(diff_qg) * amax(diff_k_g[l]), amax(fox_qg) * amax(fox_k_g[l])])
        call = functools.partial(_attention, layout, specs, scratch, b, s)
        o_a, o_b, o_c = lax.cond(2.0 * HEAD_DIM * BOUND_SLACK * qk_max <= BOUND_LIMIT,
                                 functools.partial(call, bounded=True),
                                 functools.partial(call, bounded=False), *arrays)

        wbr = w_br[l]
        wbr_a = jnp.concatenate([wbr[0, hh * hd:(hh + 1) * hd] for hh in order], axis=0)
        wbr_p = jnp.stack([wbr_a, wbr[1], wbr[2]]).astype(BF16)
        bw = NSA_HEADS * hd
        h = _merge(o_a.reshape(b * s, bw), o_b.reshape(b * s, bw), o_c.reshape(b * s, bw),
                   proj, h, wbr_p, w_o[l].astype(BF16))
        h = _ffn(h, ffn_norm_g[l].reshape(1, d), w_up[l].astype(BF16), conv_w[l],
                 conv_b[l].reshape(1, -1), w_down[l].astype(BF16), s)
    return h.reshape(b, s, d)
```

```python
import functools
import math

import numpy as np
import jax
import jax.numpy as jnp
from jax import lax
from jax.experimental import pallas as pl
from jax.experimental.pallas import tpu as pltpu

F32 = jnp.float32
BF16 = jnp.bfloat16

HEAD_DIM = 64
NSA_HEADS = 8
NSA_GROUPS = 2
CMP_LEN = 32
CMP_STRIDE = 16
SLC_BLOCK = 64
SLC_TOPN = 8
WINDOW = 256
FORCE_BONUS = 1.0e4
DIFF_HEADS = 4
FOX_HEADS = 8
CONV_W = 3
EPS = 1e-6
NEG = -1.0e30
L2E = 1.4426950408889634

LANES = 128
BLK = 256
KA = 2 * LANES
N_BIAS = 6
MASK0 = 8
BOUND0 = 40
VAUG = 16
BOUND_SLACK = 1.02
BOUND_LIMIT = 100.0
VMEM_LIMIT = 56 * 1024 * 1024

COL_MG = 0
COL_NSA_Q = 24
COL_DIFF_Q, COL_DIFF_K, COL_DIFF_V = 28, 32, 36
COL_FOX_Q, COL_FOX_K, COL_FOX_V = 40, 44, 48
COL_K_SLC, COL_V_SLC, COL_K_WIN, COL_V_WIN, COL_K_CMP, COL_V_CMP = 52, 53, 54, 55, 56, 57
PROJ_UNITS = 60
SMALL_GATE0 = 0
SMALL_FF0 = 24


def _cparams(sem):
    return pltpu.CompilerParams(dimension_semantics=sem, vmem_limit_bytes=VMEM_LIMIT)


def _split3(x):
    hi = x.astype(BF16).astype(F32)
    r1 = x - hi
    mid = r1.astype(BF16).astype(F32)
    lo = (r1 - mid).astype(BF16).astype(F32)
    return hi, mid, lo


def _dot01_3(x, m):
    hi, mid, lo = _split3(x)
    return (jnp.dot(hi.astype(BF16), m, preferred_element_type=F32)
            + jnp.dot(mid.astype(BF16), m, preferred_element_type=F32)
            + jnp.dot(lo.astype(BF16), m, preferred_element_type=F32))


def _headnorm_pair(x, bd, gain):
    ss = jnp.dot((x * x).astype(BF16), bd, preferred_element_type=F32)
    return x * lax.rsqrt(ss * (1.0 / HEAD_DIM) + EPS) * gain


def _headnorm_pair_t(x, gain):
    xt = x.T
    n = xt.shape[1]
    gain_t = jnp.broadcast_to(gain, (LANES, LANES)).T
    x2 = xt * xt
    inv = [lax.rsqrt(jnp.sum(x2[h * HEAD_DIM:(h + 1) * HEAD_DIM], axis=0, keepdims=True)
                     * (1.0 / HEAD_DIM) + EPS) for h in range(2)]
    low = lax.broadcasted_iota(jnp.int32, (LANES, n), 0) < HEAD_DIM
    xs = xt * jnp.where(low, inv[0], inv[1])
    return jnp.concatenate([xs[:, c * LANES:(c + 1) * LANES] * gain_t for c in range(n // LANES)],
                           axis=1)


def _rows8(vals, width):
    row = lax.broadcasted_iota(jnp.int32, (8, width), 0)
    out = jnp.zeros((8, width), F32)
    for k, v in enumerate(vals):
        out = jnp.where(row == k, v, out)
    return out


def _query_feats(qt, half, bias8, mask_rows, bound_row):
    tq = qt.shape[1]
    row = lax.broadcasted_iota(jnp.int32, (LANES, tq), 0)
    keep = (row < HEAD_DIM) if half == 0 else (row >= HEAD_DIM)
    nmask = BOUND0 - MASK0
    parts = [jnp.where(keep, qt, 0.0), bias8,
             mask_rows if mask_rows is not None else jnp.zeros((nmask, tq), F32),
             _rows8(_split3(-bound_row), tq) if bound_row is not None else jnp.zeros((8, tq), F32),
             jnp.zeros((LANES - BOUND0 - 8, tq), F32)]
    return jnp.concatenate(parts, axis=0).astype(BF16)


def _logit_bound(qg_ref, kg_refs):
    kmax = None
    for kg_ref in kg_refs:
        k = jnp.max(jnp.abs(kg_ref[...]), axis=-1, keepdims=True)
        kmax = k if kmax is None else jnp.maximum(kmax, k)
    return jnp.max(jnp.abs(qg_ref[...]), axis=-1, keepdims=True) * kmax * (HEAD_DIM * BOUND_SLACK)


def _store_values_t(vt_view, rows, seq):
    dv = rows.shape[0]
    vt_view[0:dv, :] = rows.astype(BF16)
    vt_view[dv:dv + 8, :] = _rows8([1.0], seq).astype(BF16)
    vt_view[dv + 8:dv + VAUG, :] = jnp.zeros((VAUG - 8, seq), BF16)


QK_AHEAD = 2


def _flash(blocks, n_chain, dv, bounded):
    acc = [jnp.zeros((dv + VAUG, BLK), F32) for _ in range(n_chain)]
    mx = [jnp.full((1, BLK), NEG, F32) for _ in range(n_chain)]
    scores = {}

    def issue(k):
        if k < len(blocks):
            scores[k] = jnp.dot(blocks[k][0](), blocks[k][2](), preferred_element_type=F32)

    for k in range(2 * QK_AHEAD):
        issue(k)
    for k0 in range(0, len(blocks), 2):
        issue(k0 + 2 * QK_AHEAD)
        issue(k0 + 2 * QK_AHEAD + 1)
        pending = []
        for k in range(k0, min(k0 + 2, len(blocks))):
            _, vt, _, mask, chain = blocks[k]
            s = scores.pop(k)
            if mask is not None:
                s = jnp.where(mask, s, NEG)
            if bounded:
                pending.append((vt, chain, None, jnp.exp2(s).astype(BF16)))
            else:
                m_new = jnp.maximum(mx[chain], jnp.max(s, axis=0, keepdims=True))
                alpha = jnp.exp2(mx[chain] - m_new)
                mx[chain] = m_new
                pending.append((vt, chain, alpha, jnp.exp2(s - m_new).astype(BF16)))
        assert len({c for _, c, _, _ in pending}) == len(pending), "a pair must not share a chain"
        for vt, chain, alpha, p in pending:
            prev = acc[chain] if alpha is None else alpha * acc[chain]
            acc[chain] = prev + jnp.dot(vt(), p, preferred_element_type=F32)
    return acc


def _causal_blocks(nt, kaug_scr, vt_rows, qt_scr):
    key = lax.broadcasted_iota(jnp.int32, (BLK, BLK), 0)
    qry = lax.broadcasted_iota(jnp.int32, (BLK, BLK), 1)
    diag = key <= qry
    blocks = []
    for j in range(nt):
        ks = slice(j * BLK, (j + 1) * BLK)
        for qh in range(j, nt):
            qs = slice(qh * BLK, (qh + 1) * BLK)
            for e in range(2):
                blocks.append((lambda ks=ks: kaug_scr[ks, :],
                               lambda e=e, ks=ks: vt_rows[e](ks),
                               lambda e=e, qs=qs: qt_scr[e, :, qs],
                               diag if qh == j else None, e * nt + qh))
    return blocks


def _proj_kernel(x_ref, g_ref, w_ref, ws_ref, o_ref, os_ref, a_scr):
    j = pl.program_id(1)

    @pl.when(j == 0)
    def _():
        x = x_ref[...]
        inv = lax.rsqrt(jnp.mean(x * x, axis=-1, keepdims=True) + EPS)
        a = (x * inv * g_ref[...]).astype(BF16)
        a_scr[...] = a
        os_ref[...] = jnp.dot(a, ws_ref[...], preferred_element_type=F32)

    o_ref[...] = jnp.dot(a_scr[...], w_ref[...], preferred_element_type=F32).astype(o_ref.dtype)


def _proj(x2d, g, w, ws, *, tm=1024, tn=2560):
    t, d = x2d.shape
    n = w.shape[1]
    return pl.pallas_call(
        _proj_kernel,
        grid=(t // tm, n // tn),
        in_specs=[
            pl.BlockSpec((tm, d), lambda i, j: (i, 0)),
            pl.BlockSpec((1, d), lambda i, j: (0, 0)),
            pl.BlockSpec((d, tn), lambda i, j: (0, j)),
            pl.BlockSpec((d, LANES), lambda i, j: (0, 0)),
        ],
        out_specs=[
            pl.BlockSpec((tm, tn), lambda i, j: (i, j)),
            pl.BlockSpec((tm, LANES), lambda i, j: (i, 0)),
        ],
        out_shape=[jax.ShapeDtypeStruct((t, n), BF16), jax.ShapeDtypeStruct((t, LANES), F32)],
        scratch_shapes=[pltpu.VMEM((tm, d), BF16)],
        compiler_params=_cparams(("parallel", "arbitrary")),
        name="proj",
    )(x2d, g, w, ws)


CMP_BATCH = 4


def _compress_kernel(k_ref, v_ref, w1p_ref, pe_ref, w1f_ref, w2p_ref, kg_ref, bd_ref,
                     kc_ref, vc_ref, raw_scr):
    nb, seq = k_ref.shape[0], k_ref.shape[1]
    nrow = kc_ref.shape[1]
    tail = raw_scr.shape[1] - seq
    for bi in range(nb):
        raw_scr[bi, seq:, :] = jnp.zeros((tail, LANES), F32)
    for kv, (src, dst) in enumerate(((k_ref, kc_ref), (v_ref, vc_ref))):
        for bi in range(nb):
            raw_scr[bi, 0:seq, :] = src[bi].astype(F32)
        c1 = jnp.dot(pe_ref[kv], w1f_ref[kv], preferred_element_type=F32)[0:1]
        pre = jnp.concatenate([c1, c1], axis=1)
        for l in range(CMP_LEN):
            rows = jnp.concatenate([raw_scr[bi, pl.ds(l, nrow, stride=CMP_STRIDE), :]
                                    for bi in range(nb)], axis=0).astype(BF16)
            pre = pre + jnp.dot(rows, w1p_ref[kv, l], preferred_element_type=F32)
        hid = jax.nn.gelu(pre).astype(BF16)
        o = jnp.dot(hid, w2p_ref[kv], preferred_element_type=F32)
        if kv == 0:
            o = _headnorm_pair(o, bd_ref[...], kg_ref[...])
        for bi in range(nb):
            dst[bi] = o[bi * nrow:(bi + 1) * nrow].astype(BF16)


def _compress(proj3, w1p, pe8, w1f, w2p, kg2, bd):
    b, s, _ = proj3.shape
    nrow = s // CMP_STRIDE
    nb = math.gcd(b, CMP_BATCH)
    full = lambda a: pl.BlockSpec(a.shape, lambda i: (0,) * a.ndim)
    return pl.pallas_call(
        _compress_kernel,
        grid=(b // nb,),
        in_specs=[pl.BlockSpec((nb, s, LANES), lambda i: (i, 0, COL_K_CMP)),
                  pl.BlockSpec((nb, s, LANES), lambda i: (i, 0, COL_V_CMP)),
                  full(w1p), full(pe8), full(w1f), full(w2p), full(kg2), full(bd)],
        out_specs=[pl.BlockSpec((nb, nrow, LANES), lambda i: (i, 0, 0)),
                   pl.BlockSpec((nb, nrow, LANES), lambda i: (i, 0, 0))],
        out_shape=[jax.ShapeDtypeStruct((b, nrow, LANES), BF16),
                   jax.ShapeDtypeStruct((b, nrow, LANES), BF16)],
        scratch_shapes=[pltpu.VMEM((nb, s + CMP_LEN, LANES), F32)],
        compiler_params=_cparams(("parallel",)),
        name="compress",
    )(proj3, proj3, w1p, pe8, w1f, w2p, kg2, bd)


def _fcum_kernel(s_ref, fb_ref, u_ref, o_ref, *, chunk):
    z = s_ref[0] + fb_ref[...]
    lf = jax.nn.log_sigmoid(z)
    lft = lf.T[SMALL_FF0:SMALL_FF0 + FOX_HEADS]
    seq = lft.shape[1]
    carry = jnp.zeros((FOX_HEADS, 1), F32)
    for c in range(seq // chunk):
        fc = _dot01_3(lft[:, c * chunk:(c + 1) * chunk], u_ref[...]) + carry
        o_ref[0, :, c * chunk:(c + 1) * chunk] = -fc
        carry = fc[:, chunk - 1:chunk]


def _fcum(small3, fb_row, u):
    b, s, _ = small3.shape
    chunk = u.shape[0]
    return pl.pallas_call(
        functools.partial(_fcum_kernel, chunk=chunk),
        grid=(b,),
        in_specs=[pl.BlockSpec((1, s, LANES), lambda i: (i, 0, 0)),
                  pl.BlockSpec((1, LANES), lambda i: (0, 0)),
                  pl.BlockSpec(u.shape, lambda i: (0, 0))],
        out_specs=pl.BlockSpec((1, FOX_HEADS, s), lambda i: (i, 0, 0)),
        out_shape=jax.ShapeDtypeStruct((b, FOX_HEADS, s), F32),
        compiler_params=_cparams(("parallel",)),
        name="fcum",
    )(small3, fb_row, u)


def _fox_kernel(q_ref, k_ref, v_ref, nf_ref, qg_ref, kg_ref, o_ref,
                kaug_scr, vt_scr, qt_scr, *, bounded):
    p = pl.program_id(1)
    seq = k_ref.shape[1]
    nt = seq // BLK

    kaug_scr[:, 0:LANES] = _headnorm_pair_t(k_ref[0].astype(F32), kg_ref[...]).T.astype(BF16)
    cb = [nf_ref[0, pl.ds(2 * p + e, 1), :] * L2E for e in range(2)]
    rows = []
    for e in range(2):
        rows.extend(_split3(cb[e]))
    feats = jnp.concatenate([_rows8(rows, seq), jnp.zeros((BOUND0 - 8, seq), F32),
                             _rows8([1.0] * 3, seq), jnp.zeros((LANES - BOUND0 - 8, seq), F32)],
                            axis=0)
    kaug_scr[:, LANES:KA] = feats.T.astype(BF16)
    vt = v_ref[0].astype(F32).T
    for e in range(2):
        _store_values_t(vt_scr.at[e], vt[e * HEAD_DIM:(e + 1) * HEAD_DIM], seq)

    qk_bound = _logit_bound(qg_ref, [kg_ref])
    qt = _headnorm_pair_t(q_ref[0].astype(F32), qg_ref[...])
    for e in range(2):
        qt_scr[e] = _query_feats(qt, e, _rows8([0.0] * (3 * e) + [1.0] * 3, seq), None,
                                 qk_bound + cb[e] if bounded else None)

    vt_rows = [lambda ks, e=e: vt_scr[e, :, ks] for e in range(2)]
    acc = _flash(_causal_blocks(nt, kaug_scr, vt_rows, qt_scr), 2 * nt, HEAD_DIM, bounded)
    for qh in range(nt):
        ot = jnp.concatenate([acc[e * nt + qh][0:HEAD_DIM] / acc[e * nt + qh][HEAD_DIM:HEAD_DIM + 1]
                              for e in range(2)], axis=0)
        o_ref[0, qh * BLK:(qh + 1) * BLK, :] = ot.T.astype(o_ref.dtype)


def _const_spec(a):
    return pl.BlockSpec(a.shape, lambda bi, p: (0,) * a.ndim)


def _col_spec(s, c):
    return pl.BlockSpec((1, s, LANES), lambda bi, p: (bi, 0, c + p))


def _fox_part(proj3, negf, qg, kg):
    s = proj3.shape[1]
    specs = [_col_spec(s, COL_FOX_Q), _col_spec(s, COL_FOX_K), _col_spec(s, COL_FOX_V),
             pl.BlockSpec((1, FOX_HEADS, s), lambda bi, p: (bi, 0, 0)),
             _const_spec(qg), _const_spec(kg)]
    scratch = [pltpu.VMEM((s, KA), BF16), pltpu.VMEM((2, HEAD_DIM + VAUG, s), BF16),
               pltpu.VMEM((2, KA, s), BF16)]
    return _fox_kernel, [proj3, proj3, proj3, negf, qg, kg], specs, scratch


def _alibi_at_query(sl_rows, seq):
    pos = lax.broadcasted_iota(jnp.int32, (1, seq), 1).astype(F32)
    return (sl_rows[0:1] + sl_rows[2:3] + sl_rows[4:5]) * pos


def _diff_kernel(q_ref, k_ref, v_ref, kx_ref, sl_ref, lam_ref, qg_ref, kg_ref, sg_ref,
                 o_ref, kaug_scr, vt_scr, qt_scr, *, lam_init, bounded):
    seq = k_ref.shape[1]
    nt = seq // BLK

    kaug_scr[:, 0:LANES] = _headnorm_pair_t(k_ref[0].astype(F32), kg_ref[...]).T.astype(BF16)
    kaug_scr[:, LANES:KA] = kx_ref[...]
    _store_values_t(vt_scr, v_ref[0].astype(F32).T, seq)
    sl = sl_ref[0]
    bound = _logit_bound(qg_ref, [kg_ref]) + _alibi_at_query(sl, seq) if bounded else None
    qt = _headnorm_pair_t(q_ref[0].astype(F32), qg_ref[...])
    for e in range(2):
        qt_scr[e] = _query_feats(qt, e, sl, None, bound)

    vt_rows = [lambda ks: vt_scr[:, ks]] * 2
    acc = _flash(_causal_blocks(nt, kaug_scr, vt_rows, qt_scr), 2 * nt, LANES, bounded)

    lv = lam_ref[...]
    lam = (jnp.exp(jnp.sum(lv[0:1] * lv[1:2], axis=-1, keepdims=True))
           - jnp.exp(jnp.sum(lv[2:3] * lv[3:4], axis=-1, keepdims=True)) + lam_init)
    for qh in range(nt):
        a0, a1 = acc[qh], acc[nt + qh]
        ob = (a0[0:LANES] / a0[LANES:LANES + 1] - lam * (a1[0:LANES] / a1[LANES:LANES + 1])).T
        ob = ob * lax.rsqrt(jnp.mean(ob * ob, axis=-1, keepdims=True) + EPS) * sg_ref[...]
        o_ref[0, qh * BLK:(qh + 1) * BLK, :] = (ob * (1.0 - lam_init)).astype(o_ref.dtype)


def _diff_part(proj3, kx, slrows, lam_p, qg, kg, sg, lam_init):
    s = proj3.shape[1]
    specs = [_col_spec(s, COL_DIFF_Q), _col_spec(s, COL_DIFF_K), _col_spec(s, COL_DIFF_V),
             _const_spec(kx), pl.BlockSpec((1, 8, s), lambda bi, p: (p, 0, 0)),
             _const_spec(lam_p), _const_spec(qg), _const_spec(kg), _const_spec(sg)]
    scratch = [pltpu.VMEM((s, KA), BF16), pltpu.VMEM((LANES + VAUG, s), BF16),
               pltpu.VMEM((2, KA, s), BF16)]
    return (functools.partial(_diff_kernel, lam_init=lam_init),
            [proj3, proj3, proj3, kx, slrows, lam_p, qg, kg, sg], specs, scratch)


def _nsa_select(imp_t, t0):
    ns, tq = imp_t.shape
    blk = lax.broadcasted_iota(jnp.int32, (ns, tq), 0)
    cur = (t0 + lax.broadcasted_iota(jnp.int32, (ns, tq), 1)) // SLC_BLOCK
    forced = (blk == 0) | (blk == cur) | (blk == cur - 1)
    score = jnp.where(blk <= cur, jnp.where(forced, imp_t + FORCE_BONUS, imp_t), NEG)
    sel = jnp.zeros((ns, tq), F32)
    blk_f = blk.astype(F32)
    for _ in range(min(SLC_TOPN, ns)):
        best = jnp.max(score, axis=0, keepdims=True)
        first = jnp.min(jnp.where(score == best, blk_f, float(ns)), axis=0, keepdims=True)
        take = blk_f == first
        sel = jnp.where(take, 1.0, sel)
        score = jnp.where(take, 2.0 * NEG, score)
    return sel


def _slope_rows(slope):
    hi, mid, lo = [float(np.float32(v)) for v in _np_split3(slope * L2E)]
    return [hi, hi, mid, mid, lo, lo]


def _np_split3(x):
    x = np.float32(x)
    hi = np.float32(x.astype(BF16))
    mid = np.float32((x - hi).astype(BF16))
    lo = np.float32((x - hi - mid).astype(BF16))
    return hi, mid, lo


CMP_ROWS = 512


def _nsa_prepare(qall_ref, kc_ref, vc_ref, ks_ref, vs_ref, kw_ref, vw_ref, kx_ref, cx_ref, nsl_ref,
                 qg_ref, ksg_ref, kwg_ref, bd_ref, ot_ref, ksaug_scr, kwaug_scr, vst_scr, vwt_scr,
                 notsel_scr, ocmp_scr, kcaug_scr, *, nc):
    bd = bd_ref[...]
    npair = NSA_HEADS // 2
    seq = ks_ref.shape[1]
    ksaug_scr[:, 0:LANES] = _headnorm_pair(ks_ref[0].astype(F32), bd, ksg_ref[...]).astype(BF16)
    kwaug_scr[:, 0:LANES] = _headnorm_pair(kw_ref[0].astype(F32), bd, kwg_ref[...]).astype(BF16)
    ksaug_scr[:, LANES:KA] = kx_ref[...]
    kwaug_scr[:, LANES:KA] = kx_ref[...]
    for v_ref, vt_scr in ((vs_ref, vst_scr), (vw_ref, vwt_scr)):
        vt = v_ref[0].astype(F32).T
        for e in range(2):
            _store_values_t(vt_scr.at[e], vt[e * HEAD_DIM:(e + 1) * HEAD_DIM], seq)
    kcaug_scr[:, 0:LANES] = kc_ref[0]
    kcaug_scr[:, LANES:KA] = cx_ref[...]
    vct = vc_ref[0].astype(F32).T.astype(BF16)

    nidx = lax.broadcasted_iota(jnp.int32, (LANES, CMP_ROWS), 0)
    cend = nidx * CMP_STRIDE + (CMP_LEN - 1)

    def chunk(c, carry):
        t0 = pl.multiple_of(c * CMP_ROWS, CMP_ROWS)
        tq = t0 + lax.broadcasted_iota(jnp.int32, (LANES, CMP_ROWS), 1)
        cmask = (tq >= cend) & (nidx < nc)
        psum = [jnp.zeros((LANES, CMP_ROWS), F32), jnp.zeros((LANES, CMP_ROWS), F32)]
        scores = []
        for r in range(npair):
            qt = _headnorm_pair_t(
                qall_ref[0, pl.ds(t0, CMP_ROWS), r * LANES:(r + 1) * LANES].astype(F32),
                qg_ref[...])
            for e in range(2):
                qf = _query_feats(qt, e, nsl_ref[r, e, :, 0:CMP_ROWS], None, None)
                scores.append(jnp.dot(kcaug_scr[...], qf, preferred_element_type=F32))
        probs = []
        for k, s in enumerate(scores):
            s = jnp.where(cmask, s, NEG)
            m = jnp.max(s, axis=0, keepdims=True)
            pe = jnp.where(cmask, jnp.exp2(s - m), 0.0)
            l = jnp.sum(pe, axis=0, keepdims=True)
            pn = pe / jnp.where(l > 0.0, l, 1.0)
            psum[k % 2] = psum[k % 2] + pn
            probs.append(pn.astype(BF16))
        for k, p in enumerate(probs):
            hs = slice((k % 2) * HEAD_DIM, (k % 2 + 1) * HEAD_DIM)
            ocmp_scr[k // 2, hs, pl.ds(t0, CMP_ROWS)] = jnp.dot(vct[hs], p,
                                                                preferred_element_type=F32)
        for g in range(NSA_GROUPS):
            hi, mid, lo = _split3(psum[g])
            imp_t = (jnp.dot(ot_ref[...], hi.astype(BF16), preferred_element_type=F32)
                     + jnp.dot(ot_ref[...], mid.astype(BF16), preferred_element_type=F32)
                     + jnp.dot(ot_ref[...], lo.astype(BF16), preferred_element_type=F32))
            notsel_scr[g, :, pl.ds(t0, CMP_ROWS)] = (_nsa_select(imp_t, t0) - 1.0) * -NEG
        return carry

    lax.fori_loop(0, seq // CMP_ROWS, chunk, 0)


def _nsa_kernel(qall_ref, q_ref, sm_ref, kc_ref, vc_ref, ks_ref, vs_ref, kw_ref, vw_ref, kx_ref,
                cx_ref, nsl_ref, slr_ref, qg_ref, ksg_ref, kwg_ref, bd_ref, ot_ref, o_ref,
                ksaug_scr, kwaug_scr, vst_scr, vwt_scr, qts_scr, qtw_scr, notsel_scr, ocmp_scr,
                kcaug_scr, gate_scr, *, nc, bounded):
    r = pl.program_id(1)
    seq = ks_ref.shape[1]
    nt = seq // BLK

    @pl.when(r == 0)
    def _():
        _nsa_prepare(qall_ref, kc_ref, vc_ref, ks_ref, vs_ref, kw_ref, vw_ref, kx_ref, cx_ref,
                     nsl_ref, qg_ref, ksg_ref, kwg_ref, bd_ref, ot_ref, ksaug_scr, kwaug_scr,
                     vst_scr, vwt_scr, notsel_scr, ocmp_scr, kcaug_scr, nc=nc)

    qk_bound = _logit_bound(qg_ref, [ksg_ref, kwg_ref])
    qt = _headnorm_pair_t(q_ref[0].astype(F32), qg_ref[...])
    for e in range(2):
        sl = slr_ref[0, e]
        bound = qk_bound + _alibi_at_query(sl, seq) if bounded else None
        qts_scr[e] = _query_feats(qt, e, sl, notsel_scr[e], bound)
        qtw_scr[e] = _query_feats(qt, e, sl, None, bound)

    key = lax.broadcasted_iota(jnp.int32, (BLK, BLK), 0)
    qry = lax.broadcasted_iota(jnp.int32, (BLK, BLK), 1)
    diag = key <= qry
    above = key > qry
    vs_rows = [lambda ks, e=e: vst_scr[e, :, ks] for e in range(2)]
    slc = _causal_blocks(nt, ksaug_scr, vs_rows, qts_scr)

    def win_block(j, qh, e, mask):
        ks = slice(j * BLK, (j + 1) * BLK)
        qs = slice(qh * BLK, (qh + 1) * BLK)
        return (lambda: kwaug_scr[ks, :], lambda: vwt_scr[e, :, ks], lambda: qtw_scr[e, :, qs],
                mask, 2 * nt + e * nt + qh)

    blocks = []
    pos = 0
    for j in range(nt):
        n_j = 2 * (nt - j)
        blocks.extend(slc[pos:pos + n_j])
        pos += n_j
        blocks.extend(win_block(j, j, e, diag) for e in range(2))
        if j > 0:
            blocks.extend(win_block(j - 1, j, e, above) for e in range(2))
    acc = _flash(blocks, 4 * nt, HEAD_DIM, bounded)
    normed = lambda a: a[0:HEAD_DIM] / a[HEAD_DIM:HEAD_DIM + 1]

    gate_scr[...] = jax.nn.sigmoid(sm_ref[0]).T
    low = lax.broadcasted_iota(jnp.int32, (LANES, seq), 0) < HEAD_DIM
    hpg = NSA_HEADS // NSA_GROUPS

    def branch_gate(c):
        base = SMALL_GATE0 + c * NSA_HEADS
        return jnp.where(low, gate_scr[pl.ds(base + r, 1), :], gate_scr[pl.ds(base + hpg + r, 1), :])

    g_cmp_t, g_slc_t, g_win_t = [branch_gate(c) for c in range(3)]
    for qh in range(nt):
        cols = slice(qh * BLK, (qh + 1) * BLK)
        o_s = jnp.concatenate([normed(acc[e * nt + qh]) for e in range(2)], axis=0)
        o_w = jnp.concatenate([normed(acc[2 * nt + e * nt + qh]) for e in range(2)], axis=0)
        mix = (g_cmp_t[:, cols] * ocmp_scr[r, :, cols] + g_slc_t[:, cols] * o_s
               + g_win_t[:, cols] * o_w)
        o_ref[0, cols, :] = mix.T.astype(o_ref.dtype)


def _nsa_part(proj3, small3, kc, vc, kx, cx, nsl, qg, ksg, kwg, bd, ot):
    s = proj3.shape[1]
    npair = NSA_HEADS // 2
    ns = s // SLC_BLOCK
    nc = (s - CMP_LEN) // CMP_STRIDE + 1
    row = lambda a: pl.BlockSpec((1,) + a.shape[1:], lambda bi, p: (bi,) + (0,) * (a.ndim - 1))
    kvspec = lambda c: pl.BlockSpec((1, s, LANES), lambda bi, p: (bi, 0, c))
    specs = [pl.BlockSpec((1, s, npair * LANES), lambda bi, p: (bi, 0, COL_NSA_Q // npair)),
             _col_spec(s, COL_NSA_Q), row(small3), row(kc), row(vc),
             kvspec(COL_K_SLC), kvspec(COL_V_SLC), kvspec(COL_K_WIN), kvspec(COL_V_WIN),
             _const_spec(kx), _const_spec(cx), _const_spec(nsl),
             pl.BlockSpec((1, 2, 8, s), lambda bi, p: (p, 0, 0, 0)),
             _const_spec(qg), _const_spec(ksg), _const_spec(kwg), _const_spec(bd), _const_spec(ot)]
    scratch = [pltpu.VMEM((s, KA), BF16), pltpu.VMEM((s, KA), BF16),
               pltpu.VMEM((2, HEAD_DIM + VAUG, s), BF16), pltpu.VMEM((2, HEAD_DIM + VAUG, s), BF16),
               pltpu.VMEM((2, KA, s), BF16), pltpu.VMEM((2, KA, s), BF16),
               pltpu.VMEM((NSA_GROUPS, ns, s), F32), pltpu.VMEM((npair, LANES, s), F32),
               pltpu.VMEM((kc.shape[1], KA), BF16), pltpu.VMEM((LANES, s), F32)]
    arrays = [proj3, proj3, small3, kc, vc, proj3, proj3, proj3, proj3, kx, cx, nsl, nsl, qg, ksg,
              kwg, bd, ot]
    return functools.partial(_nsa_kernel, nc=nc), arrays, specs, scratch


def _attention_kernel(*refs, layout, bounded):
    pos = 0
    inputs = []
    for _, n_in, _ in layout:
        inputs.append(refs[pos:pos + n_in])
        pos += n_in
    outputs = refs[pos:pos + len(layout)]
    pos += len(layout)
    for (fn, _, n_scr), ins, out in zip(layout, inputs, outputs):
        fn(*ins, out, *refs[pos:pos + n_scr], bounded=bounded)
        pos += n_scr


def _attention(layout, specs, scratch, b, s, *arrays, bounded):
    out = pl.BlockSpec((1, s, LANES), lambda bi, p: (bi, 0, p))
    npair = NSA_HEADS // 2
    return pl.pallas_call(
        functools.partial(_attention_kernel, layout=layout, bounded=bounded),
        grid=(b, npair),
        in_specs=specs,
        out_specs=[out] * len(layout),
        out_shape=[jax.ShapeDtypeStruct((b, s, npair * LANES), BF16)] * len(layout),
        scratch_shapes=scratch,
        compiler_params=_cparams(("parallel", "arbitrary")),
        name="attention",
    )(*arrays)


def _merge_kernel(oa_ref, ob_ref, oc_ref, g0_ref, g1_ref, g2_ref, h_ref, wbr_ref, wo_ref, o_ref):
    merged = None
    for c, (o_r, g_r) in enumerate(((oa_ref, g0_ref), (ob_ref, g1_ref), (oc_ref, g2_ref))):
        y = jnp.dot(o_r[...], wbr_ref[c], preferred_element_type=F32)
        term = jax.nn.sigmoid(g_r[...].astype(F32)) * y
        merged = term if merged is None else merged + term
    o_ref[...] = h_ref[...] + jnp.dot(merged.astype(BF16), wo_ref[...], preferred_element_type=F32)


def _merge(oa, ob, oc, proj2, h2, wbr, wo, *, tm=1024):
    t, d = h2.shape
    bw = oa.shape[1]
    row = lambda w: pl.BlockSpec((tm, w), lambda i: (i, 0))
    gate = lambda c: pl.BlockSpec((tm, d), lambda i: (i, COL_MG * LANES // d + c))
    resident = lambda a: pl.BlockSpec(a.shape, lambda i: (0,) * a.ndim, pipeline_mode=pl.Buffered(1))
    return pl.pallas_call(
        _merge_kernel,
        grid=(t // tm,),
        in_specs=[row(bw), row(bw), row(bw), gate(0), gate(1), gate(2), row(d),
                  resident(wbr), resident(wo)],
        out_specs=row(d),
        out_shape=jax.ShapeDtypeStruct((t, d), F32),
        compiler_params=_cparams(("parallel",)),
        name="merge",
    )(oa, ob, oc, proj2, proj2, proj2, h2, wbr, wo)


HALO = 16


def _ffn_kernel(x_ref, xh_ref, g_ref, wup_ref, cw_ref, cb_ref, wd_ref, o_ref, u_scr,
                *, tm, tf, tiles_per_seq):
    i = pl.program_id(0)
    dff = wd_ref.shape[0]

    def norm(x):
        return (x * lax.rsqrt(jnp.mean(x * x, axis=-1, keepdims=True) + EPS)
                * g_ref[...]).astype(BF16)

    a_main = norm(x_ref[...])
    a_all = jnp.concatenate([norm(xh_ref[...]), a_main], axis=0)
    seq_start = (i % tiles_per_seq) == 0
    acts = []
    for lo in range(0, dff, tf):
        hi = min(lo + tf, dff)
        cols = slice(lo, hi)
        u = jnp.dot(a_all, wup_ref[:, cols], preferred_element_type=F32)
        rows = lax.broadcasted_iota(jnp.int32, u.shape, 0)
        u_scr[:, 0:hi - lo] = jnp.where((rows < HALO) & seq_start, 0.0, u)
        uc = cb_ref[:, cols]
        for t in range(CONV_W):
            uc = uc + (cw_ref[t:t + 1, cols]
                       * u_scr[pl.ds(HALO - (CONV_W - 1) + t, tm), 0:hi - lo])
        gt = jnp.dot(a_main, wup_ref[:, dff + lo:dff + hi], preferred_element_type=F32)
        acts.append((jax.nn.gelu(uc) * gt).astype(BF16))
    act = jnp.concatenate(acts, axis=1)
    o_ref[...] = x_ref[...] + jnp.dot(act, wd_ref[...], preferred_element_type=F32)


FFN_CHUNK = 6 * 256


def _ffn(h2, g, wup, cw, cb, wd, seq, *, tm=512, tf=FFN_CHUNK):
    t, d = h2.shape
    resident = lambda a: pl.BlockSpec(a.shape, lambda i: (0,) * a.ndim, pipeline_mode=pl.Buffered(1))
    return pl.pallas_call(
        functools.partial(_ffn_kernel, tm=tm, tf=tf, tiles_per_seq=seq // tm),
        grid=(t // tm,),
        in_specs=[
            pl.BlockSpec((tm, d), lambda i: (i, 0)),
            pl.BlockSpec((HALO, d), lambda i: (jnp.maximum(i * (tm // HALO) - 1, 0), 0)),
            resident(g), resident(wup), resident(cw), resident(cb), resident(wd),
        ],
        out_specs=pl.BlockSpec((tm, d), lambda i: (i, 0)),
        out_shape=jax.ShapeDtypeStruct((t, d), F32),
        scratch_shapes=[pltpu.VMEM((HALO + tm, tf), F32)],
        compiler_params=_cparams(("parallel",)),
        name="ffn",
    )(h2, h2, g, wup, cw, cb, wd)


def _nsa_head_order():
    hpg = NSA_HEADS // NSA_GROUPS
    return [h for r in range(hpg) for h in (r, hpg + r)]


def _constants(seq):
    ns = seq // SLC_BLOCK
    nc = (seq - CMP_LEN) // CMP_STRIDE + 1
    assert MASK0 + ns <= LANES and MASK0 >= 8 and N_BIAS <= 8
    bd = np.kron(np.eye(2), np.ones((HEAD_DIM, HEAD_DIM))).astype(np.float32)
    c_start = np.arange(LANES) * CMP_STRIDE
    s_start = np.arange(ns) * SLC_BLOCK
    ot = ((c_start[None, :] < s_start[:, None] + SLC_BLOCK)
          & (c_start[None, :] + CMP_LEN > s_start[:, None])
          & (np.arange(LANES)[None, :] < nc)).astype(np.float32)
    pos = np.arange(seq)
    kx = np.zeros((seq, LANES), np.float32)
    for k in range(N_BIAS):
        kx[:, k] = (pos // 8) * 8 if k % 2 == 0 else pos % 8
    kx[pos, MASK0 + pos // SLC_BLOCK] = 1.0
    kx[:, BOUND0:BOUND0 + 3] = 1.0
    cx = kx[np.minimum(np.arange(LANES) * CMP_STRIDE + (CMP_LEN - 1), seq - 1)]
    order = _nsa_head_order()
    u = (np.arange(BLK)[:, None] <= np.arange(BLK)[None, :]).astype(np.float32)
    dsl = np.zeros((DIFF_HEADS, 8, seq), np.float32)
    for h in range(DIFF_HEADS):
        rows = _slope_rows(2.0 ** (-8.0 * (h + 1) / DIFF_HEADS))
        dsl[h, :N_BIAS, :] = np.asarray(rows, np.float32)[:, None]
    nsl = np.zeros((NSA_HEADS // 2, 2, 8, seq), np.float32)
    for slot, h in enumerate(order):
        rows = _slope_rows(2.0 ** (-8.0 * (h + 1) / NSA_HEADS))
        nsl[slot // 2, slot % 2, :N_BIAS, :] = np.asarray(rows, np.float32)[:, None]
    as_bf = lambda a: jnp.asarray(a, BF16)
    return dict(bd=as_bf(bd), ot=as_bf(ot), kx=as_bf(kx), cx=as_bf(cx), u=as_bf(u),
                dsl=jnp.asarray(dsl, F32), nsl=jnp.asarray(nsl, F32))


def _pack_w_in(w):
    d = w.shape[0]
    hd = HEAD_DIM
    w = w.astype(BF16)
    sizes = [512, 768, 24, 6 * 512, 8, 3 * d]
    offs = np.concatenate([[0], np.cumsum(sizes)])
    nq, nkv, ngate, six, ff, mg = [w[:, offs[k]:offs[k + 1]] for k in range(6)]
    hpg = NSA_HEADS // NSA_GROUPS
    nq = jnp.swapaxes(nq.reshape(d, NSA_GROUPS, hpg, hd), 1, 2).reshape(d, NSA_HEADS * hd)
    assert [g * hpg + r for r in range(hpg) for g in range(NSA_GROUPS)] == _nsa_head_order()
    pad = jnp.zeros((d, (PROJ_UNITS - COL_V_CMP - 1) * LANES), BF16)
    main = jnp.concatenate([mg, nq, six, nkv[:, 2 * LANES:], nkv[:, :2 * LANES], pad], axis=1)
    small = jnp.concatenate([ngate, ff, jnp.zeros((d, LANES - 32), BF16)], axis=1)
    return main, small


def kernel(x, attn_norm_g, w_in, nsa_q_g, nsa_k_g, cmp_pe, cmp_w1, cmp_w2, diff_q_g, diff_k_g,
           diff_lam, diff_subln_g, fox_q_g, fox_k_g, fox_b, w_br, w_o, ffn_norm_g, w_up, conv_w,
           conv_b, w_down):
    b, s, d = x.shape
    depth = w_in.shape[0]
    hd = HEAD_DIM
    qscale = hd ** -0.5 * L2E
    cst = _constants(s)
    rows16 = s // CMP_STRIDE
    assert rows16 == LANES and s % CMP_ROWS == 0, "NSA kernel keeps all compressed blocks in one 128-lane tile"
    order = _nsa_head_order()
    tile2 = lambda g: jnp.tile(g, 2).reshape(1, LANES).astype(F32)

    h = x.reshape(b * s, d)
    for l in range(depth):
        wmain, wsmall = _pack_w_in(w_in[l])
        proj, small = _proj(h, attn_norm_g[l].reshape(1, d), wmain, wsmall)
        proj3 = proj.reshape(b, s, PROJ_UNITS * LANES)
        small3 = small.reshape(b, s, LANES)

        w1 = cmp_w1[l].astype(BF16).reshape(2, CMP_LEN, hd, 2 * hd)
        z1 = jnp.zeros_like(w1)
        w1p = jnp.concatenate([jnp.concatenate([w1, z1], axis=3),
                               jnp.concatenate([z1, w1], axis=3)], axis=2)
        w2 = cmp_w2[l].astype(BF16)
        z2 = jnp.zeros_like(w2)
        w2p = jnp.concatenate([jnp.concatenate([w2, z2], axis=2),
                               jnp.concatenate([z2, w2], axis=2)], axis=1)
        pe8 = jnp.broadcast_to(cmp_pe[l].reshape(2, 1, CMP_LEN * hd), (2, 8, CMP_LEN * hd)).astype(BF16)
        kc, vc = _compress(proj3, w1p, pe8, cmp_w1[l].astype(BF16), w2p, tile2(nsa_k_g[l, 0]),
                           cst["bd"])

        fb_row = jnp.zeros((1, LANES), F32).at[0, SMALL_FF0:SMALL_FF0 + FOX_HEADS].set(fox_b[l])
        negf = _fcum(small3, fb_row, cst["u"])

        nsa_qg, diff_qg, fox_qg = nsa_q_g[l] * qscale, diff_q_g[l] * qscale, fox_q_g[l] * qscale
        lam_init = 0.8 - 0.6 * math.exp(-0.3 * l)
        parts = [
            _nsa_part(proj3, small3, kc, vc, cst["kx"], cst["cx"], cst["nsl"], tile2(nsa_qg),
                      tile2(nsa_k_g[l, 1]), tile2(nsa_k_g[l, 2]), cst["bd"], cst["ot"]),
            _diff_part(proj3, cst["kx"], cst["dsl"], diff_lam[l], tile2(diff_qg),
                       tile2(diff_k_g[l]), diff_subln_g[l].reshape(1, LANES), lam_init),
            _fox_part(proj3, negf, tile2(fox_qg), tile2(fox_k_g[l])),
        ]
        layout = tuple((fn, len(arrs), len(scr)) for fn, arrs, _, scr in parts)
        specs = [sp for _, _, sps, _ in parts for sp in sps]
        scratch = [sc for _, _, _, scs in parts for sc in scs]
        arrays = [a for _, arrs, _, _ in parts for a in arrs]
        amax = lambda g: jnp.max(jnp.abs(g))
        qk_max = functools.reduce(jnp.maximum, [
            amax(nsa_qg) * jnp.maximum(amax(nsa_k_g[l, 1]), amax(nsa_k_g[l, 2])),
            amax(diff_qg) * amax(diff_k_g[l]), amax(fox_qg) * amax(fox_k_g[l])])
        call = functools.partial(_attention, layout, specs, scratch, b, s)
        o_a, o_b, o_c = lax.cond(2.0 * HEAD_DIM * BOUND_SLACK * qk_max <= BOUND_LIMIT,
                                 functools.partial(call, bounded=True),
                                 functools.partial(call, bounded=False), *arrays)

        wbr = w_br[l]
        wbr_a = jnp.concatenate([wbr[0, hh * hd:(hh + 1) * hd] for hh in order], axis=0)
        wbr_p = jnp.stack([wbr_a, wbr[1], wbr[2]]).astype(BF16)
        bw = NSA_HEADS * hd
        h = _merge(o_a.reshape(b * s, bw), o_b.reshape(b * s, bw), o_c.reshape(b * s, bw),
                   proj, h, wbr_p, w_o[l].astype(BF16))
        h = _ffn(h, ffn_norm_g[l].reshape(1, d), w_up[l].astype(BF16), conv_w[l],
                 conv_b[l].reshape(1, -1), w_down[l].astype(BF16), s)
    return h.reshape(b, s, d)
```

```python
import functools
import math

import numpy as np
import jax
import jax.numpy as jnp
from jax import lax
from jax.experimental import pallas as pl
from jax.experimental.pallas import tpu as pltpu

F32 = jnp.float32
BF16 = jnp.bfloat16

HEAD_DIM = 64
NSA_HEADS = 8
NSA_GROUPS = 2
CMP_LEN = 32
CMP_STRIDE = 16
SLC_BLOCK = 64
SLC_TOPN = 8
WINDOW = 256
FORCE_BONUS = 1.0e4
DIFF_HEADS = 4
FOX_HEADS = 8
CONV_W = 3
EPS = 1e-6
NEG = -1.0e30
L2E = 1.4426950408889634

LANES = 128
BLK = 256
KA = 2 * LANES
N_BIAS = 6
MASK0 = 8
BOUND0 = 40
VAUG = 16
BOUND_SLACK = 1.02
BOUND_LIMIT = 100.0
VMEM_LIMIT = 56 * 1024 * 1024

COL_MG = 0
COL_NSA_Q = 24
COL_DIFF_Q, COL_DIFF_K, COL_DIFF_V = 28, 32, 36
COL_FOX_Q, COL_FOX_K, COL_FOX_V = 40, 44, 48
COL_K_SLC, COL_V_SLC, COL_K_WIN, COL_V_WIN, COL_K_CMP, COL_V_CMP = 52, 53, 54, 55, 56, 57
PROJ_UNITS = 60
COL_SMALL = 58
SMALL_GATE0 = 0
SMALL_FF0 = 24


def _cparams(sem):
    return pltpu.CompilerParams(dimension_semantics=sem, vmem_limit_bytes=VMEM_LIMIT)


def _split3(x):
    hi = x.astype(BF16).astype(F32)
    r1 = x - hi
    mid = r1.astype(BF16).astype(F32)
    lo = (r1 - mid).astype(BF16).astype(F32)
    return hi, mid, lo


def _dot01_3(x, m):
    hi, mid, lo = _split3(x)
    return (jnp.dot(hi.astype(BF16), m, preferred_element_type=F32)
            + jnp.dot(mid.astype(BF16), m, preferred_element_type=F32)
            + jnp.dot(lo.astype(BF16), m, preferred_element_type=F32))


def _headnorm_pair(x, bd, gain):
    ss = jnp.dot((x * x).astype(BF16), bd, preferred_element_type=F32)
    return x * lax.rsqrt(ss * (1.0 / HEAD_DIM) + EPS) * gain


def _headnorm_pair_t(x, gain):
    xt = x.T
    n = xt.shape[1]
    gain_t = jnp.broadcast_to(gain, (LANES, LANES)).T
    x2 = xt * xt
    inv = [lax.rsqrt(jnp.sum(x2[h * HEAD_DIM:(h + 1) * HEAD_DIM], axis=0, keepdims=True)
                     * (1.0 / HEAD_DIM) + EPS) for h in range(2)]
    low = lax.broadcasted_iota(jnp.int32, (LANES, n), 0) < HEAD_DIM
    xs = xt * jnp.where(low, inv[0], inv[1])
    return jnp.concatenate([xs[:, c * LANES:(c + 1) * LANES] * gain_t for c in range(n // LANES)],
                           axis=1)


def _rows8(vals, width):
    row = lax.broadcasted_iota(jnp.int32, (8, width), 0)
    out = jnp.zeros((8, width), F32)
    for k, v in enumerate(vals):
        out = jnp.where(row == k, v, out)
    return out


def _query_feats(qt, half, bias8, mask_rows, bound_row):
    tq = qt.shape[1]
    row = lax.broadcasted_iota(jnp.int32, (LANES, tq), 0)
    keep = (row < HEAD_DIM) if half == 0 else (row >= HEAD_DIM)
    nmask = BOUND0 - MASK0
    parts = [jnp.where(keep, qt, 0.0), bias8,
             mask_rows if mask_rows is not None else jnp.zeros((nmask, tq), F32),
             _rows8(_split3(-bound_row), tq) if bound_row is not None else jnp.zeros((8, tq), F32),
             jnp.zeros((LANES - BOUND0 - 8, tq), F32)]
    return jnp.concatenate(parts, axis=0).astype(BF16)


def _logit_bound(qg_ref, kg_refs):
    kmax = None
    for kg_ref in kg_refs:
        k = jnp.max(jnp.abs(kg_ref[...]), axis=-1, keepdims=True)
        kmax = k if kmax is None else jnp.maximum(kmax, k)
    return jnp.max(jnp.abs(qg_ref[...]), axis=-1, keepdims=True) * kmax * (HEAD_DIM * BOUND_SLACK)


def _store_values_t(vt_view, rows, seq):
    dv = rows.shape[0]
    vt_view[0:dv, :] = rows.astype(BF16)
    vt_view[dv:dv + 8, :] = _rows8([1.0], seq).astype(BF16)
    vt_view[dv + 8:dv + VAUG, :] = jnp.zeros((VAUG - 8, seq), BF16)


QK_AHEAD = 2


def _flash(blocks, n_chain, dv, bounded):
    acc = [jnp.zeros((dv + VAUG, BLK), F32) for _ in range(n_chain)]
    mx = [jnp.full((1, BLK), NEG, F32) for _ in range(n_chain)]
    scores = {}

    def issue(k):
        if k < len(blocks):
            scores[k] = jnp.dot(blocks[k][0](), blocks[k][2](), preferred_element_type=F32)

    for k in range(2 * QK_AHEAD):
        issue(k)
    for k0 in range(0, len(blocks), 2):
        issue(k0 + 2 * QK_AHEAD)
        issue(k0 + 2 * QK_AHEAD + 1)
        pending = []
        for k in range(k0, min(k0 + 2, len(blocks))):
            _, vt, _, mask, chain = blocks[k]
            s = scores.pop(k)
            if mask is not None:
                s = jnp.where(mask, s, NEG)
            if bounded:
                pending.append((vt, chain, None, jnp.exp2(s).astype(BF16)))
            else:
                m_new = jnp.maximum(mx[chain], jnp.max(s, axis=0, keepdims=True))
                alpha = jnp.exp2(mx[chain] - m_new)
                mx[chain] = m_new
                pending.append((vt, chain, alpha, jnp.exp2(s - m_new).astype(BF16)))
        assert len({c for _, c, _, _ in pending}) == len(pending), "a pair must not share a chain"
        for vt, chain, alpha, p in pending:
            prev = acc[chain] if alpha is None else alpha * acc[chain]
            acc[chain] = prev + jnp.dot(vt(), p, preferred_element_type=F32)
    return acc


def _causal_blocks(nt, kaug_scr, vt_rows, qt_scr):
    key = lax.broadcasted_iota(jnp.int32, (BLK, BLK), 0)
    qry = lax.broadcasted_iota(jnp.int32, (BLK, BLK), 1)
    diag = key <= qry
    blocks = []
    for j in range(nt):
        ks = slice(j * BLK, (j + 1) * BLK)
        for qh in range(j, nt):
            qs = slice(qh * BLK, (qh + 1) * BLK)
            for e in range(2):
                blocks.append((lambda ks=ks: kaug_scr[ks, :],
                               lambda e=e, ks=ks: vt_rows[e](ks),
                               lambda e=e, qs=qs: qt_scr[e, :, qs],
                               diag if qh == j else None, e * nt + qh))
    return blocks


def _proj_kernel(x_ref, g_ref, w_ref, o_ref, a_scr):
    j = pl.program_id(1)

    @pl.when(j == 0)
    def _():
        x = x_ref[...]
        inv = lax.rsqrt(jnp.mean(x * x, axis=-1, keepdims=True) + EPS)
        a_scr[...] = (x * inv * g_ref[...]).astype(BF16)

    o_ref[...] = jnp.dot(a_scr[...], w_ref[...], preferred_element_type=F32).astype(o_ref.dtype)


def _proj(x2d, g, w, *, tm=1024, tn=2560):
    t, d = x2d.shape
    n = w.shape[1]
    return pl.pallas_call(
        _proj_kernel,
        grid=(t // tm, n // tn),
        in_specs=[
            pl.BlockSpec((tm, d), lambda i, j: (i, 0)),
            pl.BlockSpec((1, d), lambda i, j: (0, 0)),
            pl.BlockSpec((d, tn), lambda i, j: (0, j)),
        ],
        out_specs=pl.BlockSpec((tm, tn), lambda i, j: (i, j)),
        out_shape=jax.ShapeDtypeStruct((t, n), BF16),
        scratch_shapes=[pltpu.VMEM((tm, d), BF16)],
        compiler_params=_cparams(("parallel", "arbitrary")),
        name="proj",
    )(x2d, g, w)


CMP_BATCH = 4


def _compress_kernel(k_ref, v_ref, w1p_ref, pe_ref, w1f_ref, w2p_ref, kg_ref, bd_ref,
                     kc_ref, vc_ref, raw_scr):
    nb, seq = k_ref.shape[0], k_ref.shape[1]
    nrow = kc_ref.shape[1]
    tail = raw_scr.shape[1] - seq
    for bi in range(nb):
        raw_scr[bi, seq:, :] = jnp.zeros((tail, LANES), F32)
    for kv, (src, dst) in enumerate(((k_ref, kc_ref), (v_ref, vc_ref))):
        for bi in range(nb):
            raw_scr[bi, 0:seq, :] = src[bi].astype(F32)
        c1 = jnp.dot(pe_ref[kv], w1f_ref[kv], preferred_element_type=F32)[0:1]
        pre = jnp.concatenate([c1, c1], axis=1)
        for l in range(CMP_LEN):
            rows = jnp.concatenate([raw_scr[bi, pl.ds(l, nrow, stride=CMP_STRIDE), :]
                                    for bi in range(nb)], axis=0).astype(BF16)
            pre = pre + jnp.dot(rows, w1p_ref[kv, l], preferred_element_type=F32)
        hid = jax.nn.gelu(pre).astype(BF16)
        o = jnp.dot(hid, w2p_ref[kv], preferred_element_type=F32)
        if kv == 0:
            o = _headnorm_pair(o, bd_ref[...], kg_ref[...])
        for bi in range(nb):
            dst[bi] = o[bi * nrow:(bi + 1) * nrow].astype(BF16)


def _compress(proj3, w1p, pe8, w1f, w2p, kg2, bd):
    b, s, _ = proj3.shape
    nrow = s // CMP_STRIDE
    nb = math.gcd(b, CMP_BATCH)
    full = lambda a: pl.BlockSpec(a.shape, lambda i: (0,) * a.ndim)
    return pl.pallas_call(
        _compress_kernel,
        grid=(b // nb,),
        in_specs=[pl.BlockSpec((nb, s, LANES), lambda i: (i, 0, COL_K_CMP)),
                  pl.BlockSpec((nb, s, LANES), lambda i: (i, 0, COL_V_CMP)),
                  full(w1p), full(pe8), full(w1f), full(w2p), full(kg2), full(bd)],
        out_specs=[pl.BlockSpec((nb, nrow, LANES), lambda i: (i, 0, 0)),
                   pl.BlockSpec((nb, nrow, LANES), lambda i: (i, 0, 0))],
        out_shape=[jax.ShapeDtypeStruct((b, nrow, LANES), BF16),
                   jax.ShapeDtypeStruct((b, nrow, LANES), BF16)],
        scratch_shapes=[pltpu.VMEM((nb, s + CMP_LEN, LANES), F32)],
        compiler_params=_cparams(("parallel",)),
        name="compress",
    )(proj3, proj3, w1p, pe8, w1f, w2p, kg2, bd)


def _fcum_kernel(s_ref, fb_ref, u_ref, o_ref, *, chunk):
    z = s_ref[0].astype(F32) + fb_ref[...]
    lf = jax.nn.log_sigmoid(z)
    lft = lf.T[SMALL_FF0:SMALL_FF0 + FOX_HEADS]
    seq = lft.shape[1]
    carry = jnp.zeros((FOX_HEADS, 1), F32)
    for c in range(seq // chunk):
        fc = _dot01_3(lft[:, c * chunk:(c + 1) * chunk], u_ref[...]) + carry
        o_ref[0, :, c * chunk:(c + 1) * chunk] = -fc
        carry = fc[:, chunk - 1:chunk]


def _fcum(small3, fb_row, u):
    b, s, _ = small3.shape
    chunk = u.shape[0]
    return pl.pallas_call(
        functools.partial(_fcum_kernel, chunk=chunk),
        grid=(b,),
        in_specs=[pl.BlockSpec((1, s, LANES), lambda i: (i, 0, COL_SMALL)),
                  pl.BlockSpec((1, LANES), lambda i: (0, 0)),
                  pl.BlockSpec(u.shape, lambda i: (0, 0))],
        out_specs=pl.BlockSpec((1, FOX_HEADS, s), lambda i: (i, 0, 0)),
        out_shape=jax.ShapeDtypeStruct((b, FOX_HEADS, s), F32),
        compiler_params=_cparams(("parallel",)),
        name="fcum",
    )(small3, fb_row, u)


def _fox_kernel(q_ref, k_ref, v_ref, nf_ref, qg_ref, kg_ref, o_ref,
                kaug_scr, vt_scr, qt_scr, *, bounded):
    p = pl.program_id(1)
    seq = k_ref.shape[1]
    nt = seq // BLK

    kaug_scr[:, 0:LANES] = _headnorm_pair_t(k_ref[0].astype(F32), kg_ref[...]).T.astype(BF16)
    cb = [nf_ref[0, pl.ds(2 * p + e, 1), :] * L2E for e in range(2)]
    rows = []
    for e in range(2):
        rows.extend(_split3(cb[e]))
    feats = jnp.concatenate([_rows8(rows, seq), jnp.zeros((BOUND0 - 8, seq), F32),
                             _rows8([1.0] * 3, seq), jnp.zeros((LANES - BOUND0 - 8, seq), F32)],
                            axis=0)
    kaug_scr[:, LANES:KA] = feats.T.astype(BF16)
    vt = v_ref[0].astype(F32).T
    for e in range(2):
        _store_values_t(vt_scr.at[e], vt[e * HEAD_DIM:(e + 1) * HEAD_DIM], seq)

    qk_bound = _logit_bound(qg_ref, [kg_ref])
    qt = _headnorm_pair_t(q_ref[0].astype(F32), qg_ref[...])
    for e in range(2):
        qt_scr[e] = _query_feats(qt, e, _rows8([0.0] * (3 * e) + [1.0] * 3, seq), None,
                                 qk_bound + cb[e] if bounded else None)

    vt_rows = [lambda ks, e=e: vt_scr[e, :, ks] for e in range(2)]
    acc = _flash(_causal_blocks(nt, kaug_scr, vt_rows, qt_scr), 2 * nt, HEAD_DIM, bounded)
    for qh in range(nt):
        ot = jnp.concatenate([acc[e * nt + qh][0:HEAD_DIM] / acc[e * nt + qh][HEAD_DIM:HEAD_DIM + 1]
                              for e in range(2)], axis=0)
        o_ref[0, qh * BLK:(qh + 1) * BLK, :] = ot.T.astype(o_ref.dtype)


def _const_spec(a):
    return pl.BlockSpec(a.shape, lambda bi, p: (0,) * a.ndim)


def _col_spec(s, c):
    return pl.BlockSpec((1, s, LANES), lambda bi, p: (bi, 0, c + p))


def _fox_part(proj3, negf, qg, kg):
    s = proj3.shape[1]
    specs = [_col_spec(s, COL_FOX_Q), _col_spec(s, COL_FOX_K), _col_spec(s, COL_FOX_V),
             pl.BlockSpec((1, FOX_HEADS, s), lambda bi, p: (bi, 0, 0)),
             _const_spec(qg), _const_spec(kg)]
    scratch = [pltpu.VMEM((s, KA), BF16), pltpu.VMEM((2, HEAD_DIM + VAUG, s), BF16),
               pltpu.VMEM((2, KA, s), BF16)]
    return _fox_kernel, [proj3, proj3, proj3, negf, qg, kg], specs, scratch


def _alibi_at_query(sl_rows, seq):
    pos = lax.broadcasted_iota(jnp.int32, (1, seq), 1).astype(F32)
    return (sl_rows[0:1] + sl_rows[2:3] + sl_rows[4:5]) * pos


def _diff_kernel(q_ref, k_ref, v_ref, kx_ref, sl_ref, lam_ref, qg_ref, kg_ref, sg_ref,
                 o_ref, kaug_scr, vt_scr, qt_scr, *, lam_init, bounded):
    seq = k_ref.shape[1]
    nt = seq // BLK

    kaug_scr[:, 0:LANES] = _headnorm_pair_t(k_ref[0].astype(F32), kg_ref[...]).T.astype(BF16)
    kaug_scr[:, LANES:KA] = kx_ref[...]
    _store_values_t(vt_scr, v_ref[0].astype(F32).T, seq)
    sl = sl_ref[0]
    bound = _logit_bound(qg_ref, [kg_ref]) + _alibi_at_query(sl, seq) if bounded else None
    qt = _headnorm_pair_t(q_ref[0].astype(F32), qg_ref[...])
    for e in range(2):
        qt_scr[e] = _query_feats(qt, e, sl, None, bound)

    vt_rows = [lambda ks: vt_scr[:, ks]] * 2
    acc = _flash(_causal_blocks(nt, kaug_scr, vt_rows, qt_scr), 2 * nt, LANES, bounded)

    lv = lam_ref[...]
    lam = (jnp.exp(jnp.sum(lv[0:1] * lv[1:2], axis=-1, keepdims=True))
           - jnp.exp(jnp.sum(lv[2:3] * lv[3:4], axis=-1, keepdims=True)) + lam_init)
    for qh in range(nt):
        a0, a1 = acc[qh], acc[nt + qh]
        ob = (a0[0:LANES] / a0[LANES:LANES + 1] - lam * (a1[0:LANES] / a1[LANES:LANES + 1])).T
        ob = ob * lax.rsqrt(jnp.mean(ob * ob, axis=-1, keepdims=True) + EPS) * sg_ref[...]
        o_ref[0, qh * BLK:(qh + 1) * BLK, :] = (ob * (1.0 - lam_init)).astype(o_ref.dtype)


def _diff_part(proj3, kx, slrows, lam_p, qg, kg, sg, lam_init):
    s = proj3.shape[1]
    specs = [_col_spec(s, COL_DIFF_Q), _col_spec(s, COL_DIFF_K), _col_spec(s, COL_DIFF_V),
             _const_spec(kx), pl.BlockSpec((1, 8, s), lambda bi, p: (p, 0, 0)),
             _const_spec(lam_p), _const_spec(qg), _const_spec(kg), _const_spec(sg)]
    scratch = [pltpu.VMEM((s, KA), BF16), pltpu.VMEM((LANES + VAUG, s), BF16),
               pltpu.VMEM((2, KA, s), BF16)]
    return (functools.partial(_diff_kernel, lam_init=lam_init),
            [proj3, proj3, proj3, kx, slrows, lam_p, qg, kg, sg], specs, scratch)


def _nsa_select(imp_t, t0):
    ns, tq = imp_t.shape
    blk = lax.broadcasted_iota(jnp.int32, (ns, tq), 0)
    cur = (t0 + lax.broadcasted_iota(jnp.int32, (ns, tq), 1)) // SLC_BLOCK
    forced = (blk == 0) | (blk == cur) | (blk == cur - 1)
    score = jnp.where(blk <= cur, jnp.where(forced, imp_t + FORCE_BONUS, imp_t), NEG)
    sel = jnp.zeros((ns, tq), F32)
    blk_f = blk.astype(F32)
    for _ in range(min(SLC_TOPN, ns)):
        best = jnp.max(score, axis=0, keepdims=True)
        first = jnp.min(jnp.where(score == best, blk_f, float(ns)), axis=0, keepdims=True)
        take = blk_f == first
        sel = jnp.where(take, 1.0, sel)
        score = jnp.where(take, 2.0 * NEG, score)
    return sel


def _slope_rows(slope):
    hi, mid, lo = [float(np.float32(v)) for v in _np_split3(slope * L2E)]
    return [hi, hi, mid, mid, lo, lo]


def _np_split3(x):
    x = np.float32(x)
    hi = np.float32(x.astype(BF16))
    mid = np.float32((x - hi).astype(BF16))
    lo = np.float32((x - hi - mid).astype(BF16))
    return hi, mid, lo


CMP_ROWS = 512


def _nsa_prepare(qall_ref, kc_ref, vc_ref, ks_ref, vs_ref, kw_ref, vw_ref, kx_ref, cx_ref, nsl_ref,
                 qg_ref, ksg_ref, kwg_ref, bd_ref, ot_ref, ksaug_scr, kwaug_scr, vst_scr, vwt_scr,
                 notsel_scr, ocmp_scr, kcaug_scr, *, nc):
    bd = bd_ref[...]
    npair = NSA_HEADS // 2
    seq = ks_ref.shape[1]
    ksaug_scr[:, 0:LANES] = _headnorm_pair(ks_ref[0].astype(F32), bd, ksg_ref[...]).astype(BF16)
    kwaug_scr[:, 0:LANES] = _headnorm_pair(kw_ref[0].astype(F32), bd, kwg_ref[...]).astype(BF16)
    ksaug_scr[:, LANES:KA] = kx_ref[...]
    kwaug_scr[:, LANES:KA] = kx_ref[...]
    for v_ref, vt_scr in ((vs_ref, vst_scr), (vw_ref, vwt_scr)):
        vt = v_ref[0].astype(F32).T
        for e in range(2):
            _store_values_t(vt_scr.at[e], vt[e * HEAD_DIM:(e + 1) * HEAD_DIM], seq)
    kcaug_scr[:, 0:LANES] = kc_ref[0]
    kcaug_scr[:, LANES:KA] = cx_ref[...]
    vct = vc_ref[0].astype(F32).T.astype(BF16)

    nidx = lax.broadcasted_iota(jnp.int32, (LANES, CMP_ROWS), 0)
    cend = nidx * CMP_STRIDE + (CMP_LEN - 1)

    def chunk(c, carry):
        t0 = pl.multiple_of(c * CMP_ROWS, CMP_ROWS)
        tq = t0 + lax.broadcasted_iota(jnp.int32, (LANES, CMP_ROWS), 1)
        cmask = (tq >= cend) & (nidx < nc)
        psum = [jnp.zeros((LANES, CMP_ROWS), F32), jnp.zeros((LANES, CMP_ROWS), F32)]
        scores = []
        for r in range(npair):
            qt = _headnorm_pair_t(
                qall_ref[0, pl.ds(t0, CMP_ROWS), r * LANES:(r + 1) * LANES].astype(F32),
                qg_ref[...])
            for e in range(2):
                qf = _query_feats(qt, e, nsl_ref[r, e, :, 0:CMP_ROWS], None, None)
                scores.append(jnp.dot(kcaug_scr[...], qf, preferred_element_type=F32))
        probs = []
        for k, s in enumerate(scores):
            s = jnp.where(cmask, s, NEG)
            m = jnp.max(s, axis=0, keepdims=True)
            pe = jnp.where(cmask, jnp.exp2(s - m), 0.0)
            l = jnp.sum(pe, axis=0, keepdims=True)
            pn = pe / jnp.where(l > 0.0, l, 1.0)
            psum[k % 2] = psum[k % 2] + pn
            probs.append(pn.astype(BF16))
        for k, p in enumerate(probs):
            hs = slice((k % 2) * HEAD_DIM, (k % 2 + 1) * HEAD_DIM)
            ocmp_scr[k // 2, hs, pl.ds(t0, CMP_ROWS)] = jnp.dot(vct[hs], p,
                                                                preferred_element_type=F32)
        for g in range(NSA_GROUPS):
            hi, mid, lo = _split3(psum[g])
            imp_t = (jnp.dot(ot_ref[...], hi.astype(BF16), preferred_element_type=F32)
                     + jnp.dot(ot_ref[...], mid.astype(BF16), preferred_element_type=F32)
                     + jnp.dot(ot_ref[...], lo.astype(BF16), preferred_element_type=F32))
            notsel_scr[g, :, pl.ds(t0, CMP_ROWS)] = (_nsa_select(imp_t, t0) - 1.0) * -NEG
        return carry

    lax.fori_loop(0, seq // CMP_ROWS, chunk, 0)


def _nsa_kernel(qall_ref, q_ref, sm_ref, kc_ref, vc_ref, ks_ref, vs_ref, kw_ref, vw_ref, kx_ref,
                cx_ref, nsl_ref, slr_ref, qg_ref, ksg_ref, kwg_ref, bd_ref, ot_ref, o_ref,
                ksaug_scr, kwaug_scr, vst_scr, vwt_scr, qts_scr, qtw_scr, notsel_scr, ocmp_scr,
                kcaug_scr, gate_scr, *, nc, bounded):
    r = pl.program_id(1)
    seq = ks_ref.shape[1]
    nt = seq // BLK

    @pl.when(r == 0)
    def _():
        _nsa_prepare(qall_ref, kc_ref, vc_ref, ks_ref, vs_ref, kw_ref, vw_ref, kx_ref, cx_ref,
                     nsl_ref, qg_ref, ksg_ref, kwg_ref, bd_ref, ot_ref, ksaug_scr, kwaug_scr,
                     vst_scr, vwt_scr, notsel_scr, ocmp_scr, kcaug_scr, nc=nc)

    qk_bound = _logit_bound(qg_ref, [ksg_ref, kwg_ref])
    qt = _headnorm_pair_t(q_ref[0].astype(F32), qg_ref[...])
    for e in range(2):
        sl = slr_ref[0, e]
        bound = qk_bound + _alibi_at_query(sl, seq) if bounded else None
        qts_scr[e] = _query_feats(qt, e, sl, notsel_scr[e], bound)
        qtw_scr[e] = _query_feats(qt, e, sl, None, bound)

    key = lax.broadcasted_iota(jnp.int32, (BLK, BLK), 0)
    qry = lax.broadcasted_iota(jnp.int32, (BLK, BLK), 1)
    diag = key <= qry
    above = key > qry
    vs_rows = [lambda ks, e=e: vst_scr[e, :, ks] for e in range(2)]
    slc = _causal_blocks(nt, ksaug_scr, vs_rows, qts_scr)

    def win_block(j, qh, e, mask):
        ks = slice(j * BLK, (j + 1) * BLK)
        qs = slice(qh * BLK, (qh + 1) * BLK)
        return (lambda: kwaug_scr[ks, :], lambda: vwt_scr[e, :, ks], lambda: qtw_scr[e, :, qs],
                mask, 2 * nt + e * nt + qh)

    blocks = []
    pos = 0
    for j in range(nt):
        n_j = 2 * (nt - j)
        blocks.extend(slc[pos:pos + n_j])
        pos += n_j
        blocks.extend(win_block(j, j, e, diag) for e in range(2))
        if j > 0:
            blocks.extend(win_block(j - 1, j, e, above) for e in range(2))
    acc = _flash(blocks, 4 * nt, HEAD_DIM, bounded)
    normed = lambda a: a[0:HEAD_DIM] / a[HEAD_DIM:HEAD_DIM + 1]

    gate_scr[...] = jax.nn.sigmoid(sm_ref[0].astype(F32)).T
    low = lax.broadcasted_iota(jnp.int32, (LANES, seq), 0) < HEAD_DIM
    hpg = NSA_HEADS // NSA_GROUPS

    def branch_gate(c):
        base = SMALL_GATE0 + c * NSA_HEADS
        return jnp.where(low, gate_scr[pl.ds(base + r, 1), :], gate_scr[pl.ds(base + hpg + r, 1), :])

    g_cmp_t, g_slc_t, g_win_t = [branch_gate(c) for c in range(3)]
    for qh in range(nt):
        cols = slice(qh * BLK, (qh + 1) * BLK)
        o_s = jnp.concatenate([normed(acc[e * nt + qh]) for e in range(2)], axis=0)
        o_w = jnp.concatenate([normed(acc[2 * nt + e * nt + qh]) for e in range(2)], axis=0)
        mix = (g_cmp_t[:, cols] * ocmp_scr[r, :, cols] + g_slc_t[:, cols] * o_s
               + g_win_t[:, cols] * o_w)
        o_ref[0, cols, :] = mix.T.astype(o_ref.dtype)


def _nsa_part(proj3, kc, vc, kx, cx, nsl, qg, ksg, kwg, bd, ot):
    s = proj3.shape[1]
    npair = NSA_HEADS // 2
    ns = s // SLC_BLOCK
    nc = (s - CMP_LEN) // CMP_STRIDE + 1
    row = lambda a: pl.BlockSpec((1,) + a.shape[1:], lambda bi, p: (bi,) + (0,) * (a.ndim - 1))
    kvspec = lambda c: pl.BlockSpec((1, s, LANES), lambda bi, p: (bi, 0, c))
    specs = [pl.BlockSpec((1, s, npair * LANES), lambda bi, p: (bi, 0, COL_NSA_Q // npair)),
             _col_spec(s, COL_NSA_Q), kvspec(COL_SMALL), row(kc), row(vc),
             kvspec(COL_K_SLC), kvspec(COL_V_SLC), kvspec(COL_K_WIN), kvspec(COL_V_WIN),
             _const_spec(kx), _const_spec(cx), _const_spec(nsl),
             pl.BlockSpec((1, 2, 8, s), lambda bi, p: (p, 0, 0, 0)),
             _const_spec(qg), _const_spec(ksg), _const_spec(kwg), _const_spec(bd), _const_spec(ot)]
    scratch = [pltpu.VMEM((s, KA), BF16), pltpu.VMEM((s, KA), BF16),
               pltpu.VMEM((2, HEAD_DIM + VAUG, s), BF16), pltpu.VMEM((2, HEAD_DIM + VAUG, s), BF16),
               pltpu.VMEM((2, KA, s), BF16), pltpu.VMEM((2, KA, s), BF16),
               pltpu.VMEM((NSA_GROUPS, ns, s), F32), pltpu.VMEM((npair, LANES, s), F32),
               pltpu.VMEM((kc.shape[1], KA), BF16), pltpu.VMEM((LANES, s), F32)]
    arrays = [proj3, proj3, proj3, kc, vc, proj3, proj3, proj3, proj3, kx, cx, nsl, nsl, qg, ksg,
              kwg, bd, ot]
    return functools.partial(_nsa_kernel, nc=nc), arrays, specs, scratch


def _attention_kernel(*refs, layout, bounded):
    pos = 0
    inputs = []
    for _, n_in, _ in layout:
        inputs.append(refs[pos:pos + n_in])
        pos += n_in
    outputs = refs[pos:pos + len(layout)]
    pos += len(layout)
    for (fn, _, n_scr), ins, out in zip(layout, inputs, outputs):
        fn(*ins, out, *refs[pos:pos + n_scr], bounded=bounded)
        pos += n_scr


def _attention(layout, specs, scratch, b, s, *arrays, bounded):
    out = pl.BlockSpec((1, s, LANES), lambda bi, p: (bi, 0, p))
    npair = NSA_HEADS // 2
    return pl.pallas_call(
        functools.partial(_attention_kernel, layout=layout, bounded=bounded),
        grid=(b, npair),
        in_specs=specs,
        out_specs=[out] * len(layout),
        out_shape=[jax.ShapeDtypeStruct((b, s, npair * LANES), BF16)] * len(layout),
        scratch_shapes=scratch,
        compiler_params=_cparams(("parallel", "arbitrary")),
        name="attention",
    )(*arrays)


def _merge_kernel(oa_ref, ob_ref, oc_ref, g0_ref, g1_ref, g2_ref, h_ref, wbr_ref, wo_ref, o_ref):
    merged = None
    for c, (o_r, g_r) in enumerate(((oa_ref, g0_ref), (ob_ref, g1_ref), (oc_ref, g2_ref))):
        y = jnp.dot(o_r[...], wbr_ref[c], preferred_element_type=F32)
        term = jax.nn.sigmoid(g_r[...].astype(F32)) * y
        merged = term if merged is None else merged + term
    o_ref[...] = h_ref[...] + jnp.dot(merged.astype(BF16), wo_ref[...], preferred_element_type=F32)


def _merge(oa, ob, oc, proj2, h2, wbr, wo, *, tm=1024):
    t, d = h2.shape
    bw = oa.shape[1]
    row = lambda w: pl.BlockSpec((tm, w), lambda i: (i, 0))
    gate = lambda c: pl.BlockSpec((tm, d), lambda i: (i, COL_MG * LANES // d + c))
    resident = lambda a: pl.BlockSpec(a.shape, lambda i: (0,) * a.ndim, pipeline_mode=pl.Buffered(1))
    return pl.pallas_call(
        _merge_kernel,
        grid=(t // tm,),
        in_specs=[row(bw), row(bw), row(bw), gate(0), gate(1), gate(2), row(d),
                  resident(wbr), resident(wo)],
        out_specs=row(d),
        out_shape=jax.ShapeDtypeStruct((t, d), F32),
        compiler_params=_cparams(("parallel",)),
        name="merge",
    )(oa, ob, oc, proj2, proj2, proj2, h2, wbr, wo)


HALO = 16


def _ffn_kernel(x_ref, xh_ref, g_ref, wup_ref, cw_ref, cb_ref, wd_ref, o_ref, u_scr,
                *, tm, tf, tiles_per_seq):
    i = pl.program_id(0)
    dff = wd_ref.shape[0]

    def norm(x):
        return (x * lax.rsqrt(jnp.mean(x * x, axis=-1, keepdims=True) + EPS)
                * g_ref[...]).astype(BF16)

    a_main = norm(x_ref[...])
    a_all = jnp.concatenate([norm(xh_ref[...]), a_main], axis=0)
    seq_start = (i % tiles_per_seq) == 0
    acts = []
    for lo in range(0, dff, tf):
        hi = min(lo + tf, dff)
        cols = slice(lo, hi)
        u = jnp.dot(a_all, wup_ref[:, cols], preferred_element_type=F32)
        rows = lax.broadcasted_iota(jnp.int32, u.shape, 0)
        u_scr[:, 0:hi - lo] = jnp.where((rows < HALO) & seq_start, 0.0, u)
        uc = cb_ref[:, cols]
        for t in range(CONV_W):
            uc = uc + (cw_ref[t:t + 1, cols]
                       * u_scr[pl.ds(HALO - (CONV_W - 1) + t, tm), 0:hi - lo])
        gt = jnp.dot(a_main, wup_ref[:, dff + lo:dff + hi], preferred_element_type=F32)
        acts.append((jax.nn.gelu(uc) * gt).astype(BF16))
    act = jnp.concatenate(acts, axis=1)
    o_ref[...] = x_ref[...] + jnp.dot(act, wd_ref[...], preferred_element_type=F32)


FFN_CHUNK = 6 * 256


def _ffn(h2, g, wup, cw, cb, wd, seq, *, tm=512, tf=FFN_CHUNK):
    t, d = h2.shape
    resident = lambda a: pl.BlockSpec(a.shape, lambda i: (0,) * a.ndim, pipeline_mode=pl.Buffered(1))
    return pl.pallas_call(
        functools.partial(_ffn_kernel, tm=tm, tf=tf, tiles_per_seq=seq // tm),
        grid=(t // tm,),
        in_specs=[
            pl.BlockSpec((tm, d), lambda i: (i, 0)),
            pl.BlockSpec((HALO, d), lambda i: (jnp.maximum(i * (tm // HALO) - 1, 0), 0)),
            resident(g), resident(wup), resident(cw), resident(cb), resident(wd),
        ],
        out_specs=pl.BlockSpec((tm, d), lambda i: (i, 0)),
        out_shape=jax.ShapeDtypeStruct((t, d), F32),
        scratch_shapes=[pltpu.VMEM((HALO + tm, tf), F32)],
        compiler_params=_cparams(("parallel",)),
        name="ffn",
    )(h2, h2, g, wup, cw, cb, wd)


def _nsa_head_order():
    hpg = NSA_HEADS // NSA_GROUPS
    return [h for r in range(hpg) for h in (r, hpg + r)]


def _constants(seq):
    ns = seq // SLC_BLOCK
    nc = (seq - CMP_LEN) // CMP_STRIDE + 1
    assert MASK0 + ns <= LANES and MASK0 >= 8 and N_BIAS <= 8
    bd = np.kron(np.eye(2), np.ones((HEAD_DIM, HEAD_DIM))).astype(np.float32)
    c_start = np.arange(LANES) * CMP_STRIDE
    s_start = np.arange(ns) * SLC_BLOCK
    ot = ((c_start[None, :] < s_start[:, None] + SLC_BLOCK)
          & (c_start[None, :] + CMP_LEN > s_start[:, None])
          & (np.arange(LANES)[None, :] < nc)).astype(np.float32)
    pos = np.arange(seq)
    kx = np.zeros((seq, LANES), np.float32)
    for k in range(N_BIAS):
        kx[:, k] = (pos // 8) * 8 if k % 2 == 0 else pos % 8
    kx[pos, MASK0 + pos // SLC_BLOCK] = 1.0
    kx[:, BOUND0:BOUND0 + 3] = 1.0
    cx = kx[np.minimum(np.arange(LANES) * CMP_STRIDE + (CMP_LEN - 1), seq - 1)]
    order = _nsa_head_order()
    u = (np.arange(BLK)[:, None] <= np.arange(BLK)[None, :]).astype(np.float32)
    dsl = np.zeros((DIFF_HEADS, 8, seq), np.float32)
    for h in range(DIFF_HEADS):
        rows = _slope_rows(2.0 ** (-8.0 * (h + 1) / DIFF_HEADS))
        dsl[h, :N_BIAS, :] = np.asarray(rows, np.float32)[:, None]
    nsl = np.zeros((NSA_HEADS // 2, 2, 8, seq), np.float32)
    for slot, h in enumerate(order):
        rows = _slope_rows(2.0 ** (-8.0 * (h + 1) / NSA_HEADS))
        nsl[slot // 2, slot % 2, :N_BIAS, :] = np.asarray(rows, np.float32)[:, None]
    as_bf = lambda a: jnp.asarray(a, BF16)
    return dict(bd=as_bf(bd), ot=as_bf(ot), kx=as_bf(kx), cx=as_bf(cx), u=as_bf(u),
                dsl=jnp.asarray(dsl, F32), nsl=jnp.asarray(nsl, F32))


def _pack_w_in(w):
    d = w.shape[0]
    hd = HEAD_DIM
    w = w.astype(BF16)
    sizes = [512, 768, 24, 6 * 512, 8, 3 * d]
    offs = np.concatenate([[0], np.cumsum(sizes)])
    nq, nkv, ngate, six, ff, mg = [w[:, offs[k]:offs[k + 1]] for k in range(6)]
    hpg = NSA_HEADS // NSA_GROUPS
    nq = jnp.swapaxes(nq.reshape(d, NSA_GROUPS, hpg, hd), 1, 2).reshape(d, NSA_HEADS * hd)
    assert [g * hpg + r for r in range(hpg) for g in range(NSA_GROUPS)] == _nsa_head_order()
    assert COL_SMALL == COL_V_CMP + 1 and SMALL_FF0 == ngate.shape[1]
    pad = jnp.zeros((d, PROJ_UNITS * LANES - COL_SMALL * LANES - SMALL_FF0 - ff.shape[1]), BF16)
    return jnp.concatenate([mg, nq, six, nkv[:, 2 * LANES:], nkv[:, :2 * LANES], ngate, ff, pad],
                           axis=1)


def kernel(x, attn_norm_g, w_in, nsa_q_g, nsa_k_g, cmp_pe, cmp_w1, cmp_w2, diff_q_g, diff_k_g,
           diff_lam, diff_subln_g, fox_q_g, fox_k_g, fox_b, w_br, w_o, ffn_norm_g, w_up, conv_w,
           conv_b, w_down):
    b, s, d = x.shape
    depth = w_in.shape[0]
    hd = HEAD_DIM
    qscale = hd ** -0.5 * L2E
    cst = _constants(s)
    rows16 = s // CMP_STRIDE
    assert rows16 == LANES and s % CMP_ROWS == 0, "NSA kernel keeps all compressed blocks in one 128-lane tile"
    order = _nsa_head_order()
    tile2 = lambda g: jnp.tile(g, 2).reshape(1, LANES).astype(F32)

    h = x.reshape(b * s, d)
    for l in range(depth):
        proj = _proj(h, attn_norm_g[l].reshape(1, d), _pack_w_in(w_in[l]))
        proj3 = proj.reshape(b, s, PROJ_UNITS * LANES)

        w1 = cmp_w1[l].astype(BF16).reshape(2, CMP_LEN, hd, 2 * hd)
        z1 = jnp.zeros_like(w1)
        w1p = jnp.concatenate([jnp.concatenate([w1, z1], axis=3),
                               jnp.concatenate([z1, w1], axis=3)], axis=2)
        w2 = cmp_w2[l].astype(BF16)
        z2 = jnp.zeros_like(w2)
        w2p = jnp.concatenate([jnp.concatenate([w2, z2], axis=2),
                               jnp.concatenate([z2, w2], axis=2)], axis=1)
        pe8 = jnp.broadcast_to(cmp_pe[l].reshape(2, 1, CMP_LEN * hd), (2, 8, CMP_LEN * hd)).astype(BF16)
        kc, vc = _compress(proj3, w1p, pe8, cmp_w1[l].astype(BF16), w2p, tile2(nsa_k_g[l, 0]),
                           cst["bd"])

        fb_row = jnp.zeros((1, LANES), F32).at[0, SMALL_FF0:SMALL_FF0 + FOX_HEADS].set(fox_b[l])
        negf = _fcum(proj3, fb_row, cst["u"])

        nsa_qg, diff_qg, fox_qg = nsa_q_g[l] * qscale, diff_q_g[l] * qscale, fox_q_g[l] * qscale
        lam_init = 0.8 - 0.6 * math.exp(-0.3 * l)
        parts = [
            _nsa_part(proj3, kc, vc, cst["kx"], cst["cx"], cst["nsl"], tile2(nsa_qg),
                      tile2(nsa_k_g[l, 1]), tile2(nsa_k_g[l, 2]), cst["bd"], cst["ot"]),
            _diff_part(proj3, cst["kx"], cst["dsl"], diff_lam[l], tile2(diff_qg),
                       tile2(diff_k_g[l]), diff_subln_g[l].reshape(1, LANES), lam_init),
            _fox_part(proj3, negf, tile2(fox_qg), tile2(fox_k_g[l])),
        ]
        layout = tuple((fn, len(arrs), len(scr)) for fn, arrs, _, scr in parts)
        specs = [sp for _, _, sps, _ in parts for sp in sps]
        scratch = [sc for _, _, _, scs in parts for sc in scs]
        arrays = [a for _, arrs, _, _ in parts for a in arrs]
        amax = lambda g: jnp.max(jnp.abs(g))
        qk_max = functools.reduce(jnp.maximum, [
            amax(nsa_qg) * jnp.maximum(amax(nsa_k_g[l, 1]), amax(nsa_k_g[l, 2])),
            amax(diff_qg) * amax(diff_k_g[l]), amax(fox_qg) * amax(fox_k_g[l])])
        call = functools.partial(_attention, layout, specs, scratch, b, s)
        o_a, o_b, o_c = lax.cond(2.0 * HEAD_DIM * BOUND_SLACK * qk_max <= BOUND_LIMIT,
                                 functools.partial(call, bounded=True),
                                 functools.partial(call, bounded=False), *arrays)

        wbr = w_br[l]
        wbr_a = jnp.concatenate([wbr[0, hh * hd:(hh + 1) * hd] for hh in order], axis=0)
        wbr_p = jnp.stack([wbr_a, wbr[1], wbr[2]]).astype(BF16)
        bw = NSA_HEADS * hd
        h = _merge(o_a.reshape(b * s, bw), o_b.reshape(b * s, bw), o_c.reshape(b * s, bw),
                   proj, h, wbr_p, w_o[l].astype(BF16))
        h = _ffn(h, ffn_norm_g[l].reshape(1, d), w_up[l].astype(BF16), conv_w[l],
                 conv_b[l].reshape(1, -1), w_down[l].astype(BF16), s)
    return h.reshape(b, s, d)
```
